```python
import math
import jax, jax.numpy as jnp
from jax import lax
import numpy as np

D_MODEL = 1024
BATCH = 8
SEQ = 4096
DEPTH = 2

N_MEM = 256
MIX_WIDTH = D_MODEL
ML_HEADS = 4
ML_WIDTH = MIX_WIDTH // 2
ML_HEAD_DIM = ML_WIDTH // ML_HEADS
ML_CHUNK = 64
ML_CONV = 4
SWA_HEAD_DIM = 64
SWA_WIDTH = MIX_WIDTH - ML_WIDTH
SWA_HEADS = SWA_WIDTH // SWA_HEAD_DIM
SWA_KV_HEADS = SWA_HEADS // 4
SWA_GROUP = SWA_HEADS // SWA_KV_HEADS
SWA_KV_WIDTH = SWA_KV_HEADS * SWA_HEAD_DIM
WINDOW = 128
BLOCK = 128
REL_BUCKETS = 32
REL_MAX_DIST = 128
XA_HEADS = 4
XA_HEAD_DIM = D_MODEL // XA_HEADS
D_FF = 256 * ((8 * D_MODEL // 3 + 255) // 256)
FFN_CONV = 3
ALPHA = (2.0 * DEPTH) ** 0.25
BETA = (8.0 * DEPTH) ** -0.25
EPS = 1e-5
IN_SPLITS = (2 * ML_WIDTH, 3 * ML_WIDTH, 4 * ML_WIDTH, 4 * ML_WIDTH + ML_HEADS, 4 * ML_WIDTH + 2 * ML_HEADS, 4 * ML_WIDTH + 2 * ML_HEADS + SWA_WIDTH, 4 * ML_WIDTH + 2 * ML_HEADS + SWA_WIDTH + SWA_KV_WIDTH)
N_IN = IN_SPLITS[-1] + SWA_KV_WIDTH

kernel_name = "hybrid_mlstm_swa_deepnorm_block"

f32 = jnp.float32


def layer_norm(x, g, b):
    xf = x.astype(f32)
    mu = xf.mean(-1, keepdims=True)
    var = jnp.square(xf - mu).mean(-1, keepdims=True)
    return ((xf - mu) * lax.rsqrt(var + EPS) * g.astype(f32) + b.astype(f32)).astype(x.dtype)


def causal_dwconv(x, w, b):
    K = w.shape[0]
    S = x.shape[1]
    xp = jnp.pad(x, ((0, 0), (K - 1, 0), (0, 0)))
    y = b + xp[:, 0:S] * w[0]
    for j in range(1, K):
        y = y + xp[:, j:j + S] * w[j]
    return y


def t5_bucket(dist):
    n = jnp.maximum(dist, 0)
    max_exact = REL_BUCKETS // 2
    nf = jnp.maximum(n, 1).astype(f32)
    large = max_exact + (jnp.log(nf / max_exact) / math.log(REL_MAX_DIST / max_exact) * (REL_BUCKETS - max_exact)).astype(jnp.int32)
    large = jnp.minimum(large, REL_BUCKETS - 1)
    return jnp.where(n < max_exact, n, large)


def mlstm(q, k, v, o_pre, i_pre, f_pre, norm_g):
    B, S, _ = q.shape
    nc = S // ML_CHUNK
    L = ML_CHUNK

    def heads(t):
        return t.astype(f32).reshape(B, nc, L, ML_HEADS, ML_HEAD_DIM).transpose(1, 0, 3, 2, 4)

    def gates(t):
        return t.astype(f32).reshape(B, nc, L, ML_HEADS).transpose(1, 0, 3, 2)

    qc = heads(q)
    kc = heads(k) * (ML_HEAD_DIM ** -0.5)
    vc = heads(v)
    ic = gates(i_pre)
    lfc = jax.nn.log_sigmoid(gates(f_pre))
    causal = jnp.tril(jnp.ones((L, L), dtype=bool))

    def step(carry, inp):
        C, n, m = carry
        qb, kb, vb, ig, lf = inp
        b = jnp.cumsum(lf, axis=-1)
        Dm = jnp.where(causal, b[..., :, None] - b[..., None, :] + ig[..., None, :], -jnp.inf)
        inter = b + m[..., None]
        m_t = jnp.maximum(inter, Dm.max(-1))
        w_inter = jnp.exp(inter - m_t)
        s = jnp.einsum('bhtd,bhsd->bhts', qb, kb) * jnp.exp(Dm - m_t[..., None])
        num = w_inter[..., None] * jnp.einsum('bhtd,bhde->bhte', qb, C) + jnp.einsum('bhts,bhse->bhte', s, vb)
        den = w_inter * jnp.einsum('bhtd,bhd->bht', qb, n) + s.sum(-1)
        h = num / jnp.maximum(jnp.abs(den), jnp.exp(-m_t))[..., None]
        g = b[..., -1]
        a = g[..., None] - b + ig
        m_new = jnp.maximum(g + m, a.max(-1))
        decay = jnp.exp(g + m - m_new)
        wk = jnp.exp(a - m_new[..., None])
        C_new = decay[..., None, None] * C + jnp.einsum('bhsd,bhse->bhde', kb * wk[..., None], vb)
        n_new = decay[..., None] * n + jnp.einsum('bhs,bhsd->bhd', wk, kb)
        return (C_new, n_new, m_new), h

    init = (jnp.zeros((B, ML_HEADS, ML_HEAD_DIM, ML_HEAD_DIM), f32),
            jnp.zeros((B, ML_HEADS, ML_HEAD_DIM), f32),
            jnp.zeros((B, ML_HEADS), f32))
    _, h = lax.scan(step, init, (qc, kc, vc, ic, lfc))
    mu = h.mean(-1, keepdims=True)
    var = jnp.square(h - mu).mean(-1, keepdims=True)
    hn = ((h - mu) * lax.rsqrt(var + EPS)).transpose(1, 0, 3, 2, 4).reshape(B, S, ML_WIDTH)
    hn = hn * norm_g.astype(f32)
    return (jax.nn.sigmoid(o_pre.astype(f32)) * hn).astype(q.dtype)


def sliding_window_attention(q, k, v, sinks, rel_bias):
    B, S = q.shape[:2]
    nb = S // BLOCK
    qb = q.reshape(B, nb, BLOCK, SWA_KV_HEADS, SWA_GROUP, SWA_HEAD_DIM)

    def band(t):
        tb = t.reshape(B, nb, BLOCK, SWA_KV_HEADS, SWA_HEAD_DIM)
        prev = jnp.pad(tb, ((0, 0), (1, 0), (0, 0), (0, 0), (0, 0)))[:, :-1]
        return jnp.concatenate([prev, tb], axis=2)

    kb, vb = band(k), band(v)
    logits = jnp.einsum('bnqhgd,bnkhd->bnhgqk', qb, kb).astype(f32) * (SWA_HEAD_DIM ** -0.5)
    r = jnp.arange(BLOCK)[:, None]
    c = jnp.arange(2 * BLOCK)[None, :]
    dist = BLOCK + r - c
    bias = rel_bias.astype(f32)[t5_bucket(dist)]
    bias = bias.transpose(2, 0, 1).reshape(SWA_KV_HEADS, SWA_GROUP, BLOCK, 2 * BLOCK)
    kpos = jnp.arange(nb)[:, None, None] * BLOCK - BLOCK + c[None]
    valid = (dist >= 0) & (dist < WINDOW) & (kpos >= 0)
    logits = jnp.where(valid[None, :, None, None], logits + bias, -jnp.inf)
    sink = sinks.astype(f32).reshape(SWA_KV_HEADS, SWA_GROUP)[None, None, :, :, None, None]
    mx = jnp.maximum(logits.max(-1, keepdims=True), sink)
    p = jnp.exp(logits - mx)
    probs = (p / (p.sum(-1, keepdims=True) + jnp.exp(sink - mx))).astype(v.dtype)
    out = jnp.einsum('bnhgqk,bnkhd->bnqhgd', probs, vb)
    return out.reshape(B, S, SWA_WIDTH)


def hybrid_mixer(x, w_in, ml_conv_w, ml_conv_b, ml_i_bias, ml_f_bias, ml_norm_g, swa_sinks, rel_bias, w_out):
    B, S, _ = x.shape
    proj = x @ w_in
    ml_qk, ml_v, ml_o, ml_i, ml_f, sw_q, sw_k, sw_v = jnp.split(proj, IN_SPLITS, axis=-1)
    ml_qk = jax.nn.silu(causal_dwconv(ml_qk, ml_conv_w, ml_conv_b))
    ml_q, ml_k = jnp.split(ml_qk, 2, axis=-1)
    h_ml = mlstm(ml_q, ml_k, ml_v, ml_o, ml_i + ml_i_bias, ml_f + ml_f_bias, ml_norm_g)
    h_sw = sliding_window_attention(sw_q.reshape(B, S, SWA_HEADS, SWA_HEAD_DIM),
                                    sw_k.reshape(B, S, SWA_KV_HEADS, SWA_HEAD_DIM),
                                    sw_v.reshape(B, S, SWA_KV_HEADS, SWA_HEAD_DIM),
                                    swa_sinks, rel_bias)
    return jnp.concatenate([h_ml, h_sw], axis=-1) @ w_out


def memory_cross_attention(x, mem, wq, wkv, wo):
    B, S, _ = x.shape
    M = mem.shape[1]
    q = (x @ wq).reshape(B, S, XA_HEADS, XA_HEAD_DIM)
    k, v = jnp.split(mem @ wkv, 2, axis=-1)
    k = k.reshape(B, M, XA_HEADS, XA_HEAD_DIM)
    v = v.reshape(B, M, XA_HEADS, XA_HEAD_DIM)
    logits = jnp.einsum('bshd,bmhd->bhsm', q, k).astype(f32) * (XA_HEAD_DIM ** -0.5)
    p = jax.nn.softmax(logits, axis=-1).astype(x.dtype)
    o = jnp.einsum('bhsm,bmhd->bshd', p, v).reshape(B, S, D_MODEL)
    return o @ wo


def conv_ffn(x, w_up, conv_w, conv_b, w_down):
    u = causal_dwconv(x @ w_up, conv_w, conv_b)
    g, val = jnp.split(u, 2, axis=-1)
    return (jax.nn.gelu(g) * val) @ w_down


def setup_inputs(seed: int = 0) -> dict:
    key = jax.random.key(seed)
    ks = jax.random.split(key, 24)
    nrm = lambda k, shape, s: jax.random.normal(k, shape, f32) * s
    L = DEPTH
    return {
        'x': nrm(ks[0], (BATCH, SEQ, D_MODEL), 1.0),
        'mem': nrm(ks[1], (BATCH, N_MEM, D_MODEL), 1.0),
        'rel_bias': nrm(ks[2], (REL_BUCKETS, SWA_HEADS), 0.5),
        'w_in': nrm(ks[3], (L, D_MODEL, N_IN), D_MODEL ** -0.5),
        'ml_conv_w': nrm(ks[4], (L, ML_CONV, 2 * ML_WIDTH), ML_CONV ** -0.5),
        'ml_conv_b': nrm(ks[5], (L, 2 * ML_WIDTH), 0.02),
        'ml_i_bias': nrm(ks[6], (L, ML_HEADS), 0.1),
        'ml_f_bias': jnp.linspace(3.0, 6.0, ML_HEADS, dtype=f32)[None, :] + nrm(ks[7], (L, ML_HEADS), 0.1),
        'ml_norm_g': 1.0 + nrm(ks[8], (L, ML_WIDTH), 0.05),
        'swa_sinks': nrm(ks[9], (L, SWA_HEADS), 0.5),
        'w_out': nrm(ks[10], (L, MIX_WIDTH, D_MODEL), BETA * MIX_WIDTH ** -0.5),
        'ln1_g': 1.0 + nrm(ks[11], (L, D_MODEL), 0.05),
        'ln1_b': nrm(ks[12], (L, D_MODEL), 0.02),
        'xa_wq': nrm(ks[13], (L, D_MODEL, D_MODEL), D_MODEL ** -0.5),
        'xa_wkv': nrm(ks[14], (L, D_MODEL, 2 * D_MODEL), D_MODEL ** -0.5),
        'xa_wo': nrm(ks[15], (L, D_MODEL, D_MODEL), BETA * D_MODEL ** -0.5),
        'ln2_g': 1.0 + nrm(ks[16], (L, D_MODEL), 0.05),
        'ln2_b': nrm(ks[17], (L, D_MODEL), 0.02),
        'ffn_w_up': nrm(ks[18], (L, D_MODEL, 2 * D_FF), D_MODEL ** -0.5),
        'ffn_conv_w': nrm(ks[19], (L, FFN_CONV, 2 * D_FF), FFN_CONV ** -0.5),
        'ffn_conv_b': nrm(ks[20], (L, 2 * D_FF), 0.02),
        'ffn_w_down': nrm(ks[21], (L, D_FF, D_MODEL), BETA * D_FF ** -0.5),
        'ln3_g': 1.0 + nrm(ks[22], (L, D_MODEL), 0.05),
        'ln3_b': nrm(ks[23], (L, D_MODEL), 0.02),
    }


def reference(x, mem, rel_bias, w_in, ml_conv_w, ml_conv_b, ml_i_bias, ml_f_bias, ml_norm_g, swa_sinks, w_out, ln1_g, ln1_b, xa_wq, xa_wkv, xa_wo, ln2_g, ln2_b, ffn_w_up, ffn_conv_w, ffn_conv_b, ffn_w_down, ln3_g, ln3_b):
    for l in range(DEPTH):
        h = hybrid_mixer(x, w_in[l], ml_conv_w[l], ml_conv_b[l], ml_i_bias[l], ml_f_bias[l], ml_norm_g[l], swa_sinks[l], rel_bias, w_out[l])
        x = layer_norm(ALPHA * x + h, ln1_g[l], ln1_b[l])
        h = memory_cross_attention(x, mem, xa_wq[l], xa_wkv[l], xa_wo[l])
        x = layer_norm(ALPHA * x + h, ln2_g[l], ln2_b[l])
        h = conv_ffn(x, ffn_w_up[l], ffn_conv_w[l], ffn_conv_b[l], ffn_w_down[l])
        x = layer_norm(ALPHA * x + h, ln3_g[l], ln3_b[l])
    return x
```

```python
import functools
import math

import numpy as np
import jax
import jax.numpy as jnp
from jax import lax
from jax.experimental import pallas as pl
from jax.experimental.pallas import tpu as pltpu

f32 = jnp.float32
bf16 = jnp.bfloat16

ML_HEADS = 4
ML_HEAD_DIM = 128
ML_WIDTH = ML_HEADS * ML_HEAD_DIM
ML_CONV = 4
SWA_HEADS = 8
SWA_KV_HEADS = 2
SWA_GROUP = SWA_HEADS // SWA_KV_HEADS
SWA_HEAD_DIM = 64
SWA_WIDTH = SWA_HEADS * SWA_HEAD_DIM
WINDOW = 128
BLOCK = 128
REL_BUCKETS = 32
REL_MAX_DIST = 128
XA_HEADS = 4
FFN_CONV = 3
DEPTH = 2
ALPHA = (2.0 * DEPTH) ** 0.25
EPS = 1e-5

LANES = 128
SUBLANES = 8
TM_PROJ = 512
ML_CHUNK = 256
TQ_SWA = 512
TM_MIX = 512
TM_FFN = 512
FF_CHUNK = 256
VMEM_LIMIT = 56 * 1024 * 1024
NEG_BIG = -1e30


def _params(n_axes):
    return pltpu.CompilerParams(
        dimension_semantics=("arbitrary",) * n_axes, vmem_limit_bytes=VMEM_LIMIT)


def _const_spec(shape):
    nd = len(shape)
    return pl.BlockSpec(shape, lambda *_: (0,) * nd)


def _layer_norm(z, g, b):
    mu = jnp.mean(z, axis=-1, keepdims=True)
    zc = z - mu
    var = jnp.mean(zc * zc, axis=-1, keepdims=True)
    return zc * lax.rsqrt(var + EPS) * g + b


def _inproj_kernel(x_ref, wqk_ref, wvog_ref, wsw_ref, cw_ref, cb_ref,
                   qk_ref, vo_ref, g_ref, sw_ref, p_ref):
    tm = x_ref.shape[1]
    width = wqk_ref.shape[1]

    @pl.when(pl.program_id(1) == 0)
    def _():
        p_ref[0:SUBLANES, :] = jnp.zeros((SUBLANES, width), f32)

    xb = x_ref[0].astype(bf16)
    acc = jnp.dot(xb, wqk_ref[...], preferred_element_type=f32)
    p_ref[SUBLANES:SUBLANES + tm, :] = acc
    y = cb_ref[...] + p_ref[5:5 + tm, :] * cw_ref[0:1, :]
    y = y + p_ref[6:6 + tm, :] * cw_ref[1:2, :]
    y = y + p_ref[7:7 + tm, :] * cw_ref[2:3, :]
    y = y + acc * cw_ref[3:4, :]
    qk_ref[0] = (y * jax.nn.sigmoid(y)).astype(bf16)
    p_ref[0:SUBLANES, :] = p_ref[tm:tm + SUBLANES, :]

    accv = jnp.dot(xb, wvog_ref[...], preferred_element_type=f32)
    vo_w = vo_ref.shape[2]
    vo_ref[0] = accv[:, :vo_w].astype(bf16)
    g_ref[0] = accv[:, vo_w:].T[0:SUBLANES, :]
    sw_ref[0] = jnp.dot(xb, wsw_ref[...], preferred_element_type=f32).astype(bf16)


def _inproj(x, wqk, wvog, wsw, cw, cb):
    B, S, D = x.shape
    tm = min(TM_PROJ, S)
    grid = (B, S // tm)
    tok = lambda w: pl.BlockSpec((1, tm, w), lambda b, s: (b, s, 0))
    return pl.pallas_call(
        _inproj_kernel,
        grid=grid,
        in_specs=[tok(D), _const_spec(wqk.shape), _const_spec(wvog.shape), _const_spec(wsw.shape),
                  _const_spec(cw.shape), _const_spec(cb.shape)],
        out_specs=[tok(wqk.shape[1]), tok(2 * ML_WIDTH),
                   pl.BlockSpec((1, SUBLANES, tm), lambda b, s: (b, 0, s)),
                   tok(wsw.shape[1])],
        out_shape=[jax.ShapeDtypeStruct((B, S, wqk.shape[1]), bf16),
                   jax.ShapeDtypeStruct((B, S, 2 * ML_WIDTH), bf16),
                   jax.ShapeDtypeStruct((B, SUBLANES, S), f32),
                   jax.ShapeDtypeStruct((B, S, wsw.shape[1]), bf16)],
        scratch_shapes=[pltpu.VMEM((tm + SUBLANES, wqk.shape[1]), f32)],
        compiler_params=_params(2),
        name="inproj",
    )(x, wqk, wvog, wsw, cw, cb)


def _scan_lanes(x, op, fill):
    n = x.shape[1]
    lane = lax.broadcasted_iota(jnp.int32, x.shape, 1)
    sh = 1
    while sh < n:
        x = op(x, jnp.where(lane >= sh, pltpu.roll(x, sh, 1), fill))
        sh *= 2
    return x


def _mlstm_kernel(qk_ref, vo_ref, g_ref, gb_ref, ng_ref, out_ref, c_ref, m_ref):
    L = qk_ref.shape[1]
    dh = ML_HEAD_DIM
    scale = dh ** -0.5

    @pl.when(pl.program_id(1) == 0)
    def _():
        c_ref[...] = jnp.zeros(c_ref.shape, f32)
        m_ref[...] = jnp.zeros(m_ref.shape, f32)

    g = g_ref[0] + gb_ref[...]
    ig = jnp.concatenate([g[0:4], g[0:4]], axis=0)
    fg = jnp.concatenate([g[4:8], g[4:8]], axis=0)
    lf = jnp.minimum(fg, 0.0) - jnp.log1p(jnp.exp(-jnp.abs(fg)))
    b = _scan_lanes(lf, jnp.add, 0.0)
    c = ig - b
    m_prev = m_ref[:, 0:1]
    big_m = jnp.maximum(m_prev, _scan_lanes(c, jnp.maximum, -jnp.inf))
    m_last = big_m[:, L - 1:L]
    g_tot = b[:, L - 1:L]
    stack = jnp.concatenate(
        [c, big_m, b + big_m, jnp.zeros((LANES - 3 * SUBLANES, L), f32)], axis=0)
    cols = stack.T
    decay = jnp.exp(m_prev - m_last)

    row = lax.broadcasted_iota(jnp.int32, (L, L), 0)
    col = lax.broadcasted_iota(jnp.int32, (L, L), 1)
    causal = row >= col
    ones_col = (lax.broadcasted_iota(jnp.int32, (L, dh), 1) == 0).astype(bf16)

    for h in range(ML_HEADS):
        q = qk_ref[0, :, h * dh:(h + 1) * dh]
        k = qk_ref[0, :, ML_WIDTH + h * dh:ML_WIDTH + (h + 1) * dh]
        v = vo_ref[0, :, h * dh:(h + 1) * dh]
        o = vo_ref[0, :, ML_WIDTH + h * dh:ML_WIDTH + (h + 1) * dh]
        c_row = c[h:h + 1, :]
        c_col = cols[:, h:h + 1]
        m_col = cols[:, SUBLANES + h:SUBLANES + h + 1]
        bm_col = cols[:, 2 * SUBLANES + h:2 * SUBLANES + h + 1]

        p = jnp.where(causal, jnp.exp(c_row - m_col), 0.0)
        s = lax.dot_general(q, k, (((1,), (1,)), ((), ())), preferred_element_type=f32)
        s = (s * scale * p).astype(bf16)
        v_aug = jnp.concatenate([v, ones_col], axis=1)
        c_aug = c_ref[h]
        w_inter = jnp.exp(m_prev[h:h + 1, :] - m_col)
        nd = w_inter * jnp.dot(q, c_aug.astype(bf16), preferred_element_type=f32)
        nd = nd + jnp.dot(s, v_aug, preferred_element_type=f32)
        num = nd[:, :dh]
        den = nd[:, dh:dh + 1]
        hh = num / jnp.maximum(jnp.abs(den), jnp.exp(-bm_col))
        mu = jnp.mean(hh, axis=-1, keepdims=True)
        hc = hh - mu
        var = jnp.mean(hc * hc, axis=-1, keepdims=True)
        hn = hc * lax.rsqrt(var + EPS) * ng_ref[:, h * dh:(h + 1) * dh]
        out_ref[0, :, h * dh:(h + 1) * dh] = (jax.nn.sigmoid(o.astype(f32)) * hn).astype(bf16)

        wk = jnp.exp(c_col - m_last[h:h + 1, :]) * scale
        kw_t = (k.astype(f32) * wk).T.astype(bf16)
        c_ref[h] = decay[h:h + 1, :] * c_aug + jnp.dot(kw_t, v_aug, preferred_element_type=f32)

    m_ref[...] = jnp.broadcast_to(g_tot + m_last, m_ref.shape)


def _mlstm(qk, vo, gates, gbias, norm_g):
    B, S, _ = qk.shape
    L = min(ML_CHUNK, S)
    grid = (B, S // L)
    return pl.pallas_call(
        _mlstm_kernel,
        grid=grid,
        in_specs=[pl.BlockSpec((1, L, 2 * ML_WIDTH), lambda b, s: (b, s, 0)),
                  pl.BlockSpec((1, L, 2 * ML_WIDTH), lambda b, s: (b, s, 0)),
                  pl.BlockSpec((1, SUBLANES, L), lambda b, s: (b, 0, s)),
                  _const_spec(gbias.shape), _const_spec(norm_g.shape)],
        out_specs=pl.BlockSpec((1, L, ML_WIDTH), lambda b, s: (b, s, 0)),
        out_shape=jax.ShapeDtypeStruct((B, S, ML_WIDTH), bf16),
        scratch_shapes=[pltpu.VMEM((ML_HEADS, ML_HEAD_DIM, 2 * ML_HEAD_DIM), f32),
                        pltpu.VMEM((SUBLANES, LANES), f32)],
        compiler_params=_params(2),
        name="mlstm",
    )(qk, vo, gates, gbias, norm_g)


def _t5_bucket_table():
    r = np.arange(BLOCK)[:, None]
    c = np.arange(2 * BLOCK)[None, :]
    dist = BLOCK + r - c
    n = np.maximum(dist, 0)
    max_exact = REL_BUCKETS // 2
    nf = np.maximum(n, 1).astype(np.float32)
    large = max_exact + (np.log(nf / np.float32(max_exact)) / np.float32(math.log(REL_MAX_DIST / max_exact))
                         * np.float32(REL_BUCKETS - max_exact)).astype(np.int32)
    large = np.minimum(large, REL_BUCKETS - 1)
    bucket = np.where(n < max_exact, n, large)
    valid = (dist >= 0) & (dist < WINDOW)
    return np.where(valid, bucket, -1).astype(np.int32)


def _bias_kernel(bucket_ref, rb_ref, out_ref):
    bucket = bucket_ref[...]
    for h in range(SWA_HEADS):
        acc = jnp.full(bucket.shape, NEG_BIG, f32)
        for i in range(REL_BUCKETS):
            acc = jnp.where(bucket == i, rb_ref[i, h], acc)
        out_ref[h] = acc


def _swa_bias(rel_bias):
    bucket = jnp.asarray(_t5_bucket_table())
    return pl.pallas_call(
        _bias_kernel,
        in_specs=[pl.BlockSpec(memory_space=pltpu.VMEM), pl.BlockSpec(memory_space=pltpu.SMEM)],
        out_specs=pl.BlockSpec(memory_space=pltpu.VMEM),
        out_shape=jax.ShapeDtypeStruct((SWA_HEADS, BLOCK, 2 * BLOCK), f32),
        name="swa_bias",
    )(bucket, rel_bias)


def _swa_kernel(sw_ref, prev_ref, bias_ref, sink_ref, out_ref):
    tq = sw_ref.shape[1]
    scale = SWA_HEAD_DIM ** -0.5
    pair = 2 * SWA_HEAD_DIM
    k_off = SWA_WIDTH
    v_off = SWA_WIDTH + SWA_KV_HEADS * pair
    first_tile = pl.program_id(1) == 0
    lo = lax.broadcasted_iota(jnp.int32, (BLOCK, pair), 1) < SWA_HEAD_DIM
    left = lax.broadcasted_iota(jnp.int32, (BLOCK, 2 * BLOCK), 1) < BLOCK

    for j in range(tq // BLOCK):
        rows = slice(j * BLOCK, (j + 1) * BLOCK)
        prows = slice((j - 1) * BLOCK, j * BLOCK)
        for kv in range(SWA_KV_HEADS):
            kc = slice(k_off + kv * pair, k_off + (kv + 1) * pair)
            vc = slice(v_off + kv * pair, v_off + (kv + 1) * pair)
            if j == 0:
                k_prev = prev_ref[0, :, kv * pair:(kv + 1) * pair]
                v_prev = prev_ref[0, :, (SWA_KV_HEADS + kv) * pair:(SWA_KV_HEADS + kv + 1) * pair]
            else:
                k_prev = sw_ref[0, prows, kc]
                v_prev = sw_ref[0, prows, vc]
            kband = jnp.concatenate([k_prev, sw_ref[0, rows, kc]], axis=0)
            vband = jnp.concatenate([v_prev, sw_ref[0, rows, vc]], axis=0)
            q0 = sw_ref[0, rows, kv * 2 * pair:kv * 2 * pair + pair]
            q1 = sw_ref[0, rows, kv * 2 * pair + pair:(kv + 1) * 2 * pair]
            zero = jnp.zeros_like(q0)
            qs = jnp.concatenate([jnp.where(lo, q0, zero), jnp.where(lo, zero, q0),
                                  jnp.where(lo, q1, zero), jnp.where(lo, zero, q1)], axis=0)
            logits = lax.dot_general(qs, kband, (((1,), (1,)), ((), ())),
                                     preferred_element_type=f32)
            ps, dens = [], []
            for gq in range(SWA_GROUP):
                head = kv * SWA_GROUP + gq
                lg = logits[gq * BLOCK:(gq + 1) * BLOCK] * scale + bias_ref[head]
                if j == 0:
                    lg = jnp.where(jnp.logical_and(first_tile, left), NEG_BIG, lg)
                sink = sink_ref[head]
                mx = jnp.maximum(jnp.max(lg, axis=-1, keepdims=True), sink)
                p = jnp.exp(lg - mx)
                dens.append(jnp.sum(p, axis=-1, keepdims=True) + jnp.exp(sink - mx))
                ps.append(p.astype(bf16))
            pv = jnp.dot(jnp.concatenate(ps, axis=0), vband, preferred_element_type=f32)
            o = [pv[gq * BLOCK:(gq + 1) * BLOCK] / dens[gq] for gq in range(SWA_GROUP)]
            out_ref[0, rows, kv * 2 * pair:kv * 2 * pair + pair] = jnp.where(lo, o[0], o[1]).astype(bf16)
            out_ref[0, rows, kv * 2 * pair + pair:(kv + 1) * 2 * pair] = jnp.where(lo, o[2], o[3]).astype(bf16)


def _swa(sw, bias, sinks):
    B, S, W = sw.shape
    tq = min(TQ_SWA, S)
    nblk = tq // BLOCK
    kvw = W - SWA_WIDTH
    assert kvw == SWA_WIDTH, "k/v column group must be as wide as the q group"
    return pl.pallas_call(
        _swa_kernel,
        grid=(B, S // tq),
        in_specs=[pl.BlockSpec((1, tq, W), lambda b, s: (b, s, 0)),
                  pl.BlockSpec((1, BLOCK, kvw), lambda b, s: (b, jnp.maximum(s * nblk - 1, 0), 1)),
                  _const_spec(bias.shape),
                  pl.BlockSpec(memory_space=pltpu.SMEM)],
        out_specs=pl.BlockSpec((1, tq, SWA_WIDTH), lambda b, s: (b, s, 0)),
        out_shape=jax.ShapeDtypeStruct((B, S, SWA_WIDTH), bf16),
        compiler_params=_params(2),
        name="swa",
    )(sw, sw, bias, sinks)


def _memkv_kernel(mem_ref, wkv_ref, k_ref, v_ref):
    d = k_ref.shape[2]
    kv = jnp.dot(mem_ref[0].astype(bf16), wkv_ref[...], preferred_element_type=f32)
    k_ref[0] = kv[:, :d].astype(bf16)
    v_ref[0] = kv[:, d:].astype(bf16)


def _memkv(mem, wkv):
    B, M, D = mem.shape
    blk = pl.BlockSpec((1, M, D), lambda b: (b, 0, 0))
    return pl.pallas_call(
        _memkv_kernel,
        grid=(B,),
        in_specs=[blk, _const_spec(wkv.shape)],
        out_specs=[blk, blk],
        out_shape=[jax.ShapeDtypeStruct((B, M, D), bf16)] * 2,
        compiler_params=_params(1),
        name="memkv",
    )(mem, wkv)


def _mix_kernel(hml_ref, hsw_ref, x_ref, woml_ref, wosw_ref, g1_ref, b1_ref,
                wq_ref, k_ref, v_ref, wo_ref, g2_ref, b2_ref, out_ref):
    d = x_ref.shape[2]
    dh = d // XA_HEADS
    h = jnp.dot(hml_ref[0], woml_ref[...], preferred_element_type=f32)
    h = h + jnp.dot(hsw_ref[0], wosw_ref[...], preferred_element_type=f32)
    x1 = _layer_norm(ALPHA * x_ref[0] + h, g1_ref[...], b1_ref[...])

    q = jnp.dot(x1.astype(bf16), wq_ref[...], preferred_element_type=f32).astype(bf16)
    outs = []
    for a in range(XA_HEADS):
        cs = slice(a * dh, (a + 1) * dh)
        lg = lax.dot_general(q[:, cs], k_ref[0, :, cs], (((1,), (1,)), ((), ())),
                             preferred_element_type=f32) * (dh ** -0.5)
        mx = jnp.max(lg, axis=-1, keepdims=True)
        p = jnp.exp(lg - mx)
        den = jnp.sum(p, axis=-1, keepdims=True)
        o = jnp.dot(p.astype(bf16), v_ref[0, :, cs], preferred_element_type=f32) / den
        outs.append(o.astype(bf16))
    o = jnp.concatenate(outs, axis=1)
    h2 = jnp.dot(o, wo_ref[...], preferred_element_type=f32)
    out_ref[0] = _layer_norm(ALPHA * x1 + h2, g2_ref[...], b2_ref[...])


def _mix(hml, hsw, x, woml, wosw, g1, b1, wq, k, v, wo, g2, b2):
    B, S, D = x.shape
    tm = min(TM_MIX, S)
    M = k.shape[1]
    tok = lambda w: pl.BlockSpec((1, tm, w), lambda b, s: (b, s, 0))
    memblk = pl.BlockSpec((1, M, D), lambda b, s: (b, 0, 0))
    consts = [woml, wosw, g1, b1, wq]
    return pl.pallas_call(
        _mix_kernel,
        grid=(B, S // tm),
        in_specs=[tok(hml.shape[2]), tok(hsw.shape[2]), tok(D)]
                 + [_const_spec(a.shape) for a in consts]
                 + [memblk, memblk, _const_spec(wo.shape), _const_spec(g2.shape), _const_spec(b2.shape)],
        out_specs=tok(D),
        out_shape=jax.ShapeDtypeStruct((B, S, D), f32),
        compiler_params=_params(2),
        name="mix",
    )(hml, hsw, x, woml, wosw, g1, b1, wq, k, v, wo, g2, b2)


def _gelu_tanh(x):
    c = math.sqrt(2.0 / math.pi)
    return x * (0.5 * (1.0 + jnp.tanh(c * (x + 0.044715 * (x * x * x)))))


def _ffn_kernel(x_ref, wup_ref, cw_ref, cb_ref, wdn_ref, g_ref, b_ref, out_ref,
                carry_ref, buf_ref, acc_ref):
    tm = x_ref.shape[1]
    n_chunks = wdn_ref.shape[0]
    cwid = wdn_ref.shape[1]

    @pl.when(pl.program_id(1) == 0)
    def _():
        carry_ref[...] = jnp.zeros(carry_ref.shape, f32)

    x = x_ref[0]
    xb = x.astype(bf16)

    def conv_half(idx):
        u = jnp.dot(xb, wup_ref[idx], preferred_element_type=f32)
        buf_ref[0:SUBLANES, :] = carry_ref[idx]
        buf_ref[SUBLANES:SUBLANES + tm, :] = u
        carry_ref[idx] = u[tm - SUBLANES:, :]
        y = cb_ref[idx] + buf_ref[6:6 + tm, :] * cw_ref[idx, 0:1, :]
        y = y + buf_ref[7:7 + tm, :] * cw_ref[idx, 1:2, :]
        return y + u * cw_ref[idx, 2:3, :]

    for ci in range(n_chunks):
        gate = conv_half(ci)
        val = conv_half(n_chunks + ci)
        hmid = (_gelu_tanh(gate) * val).astype(bf16)
        part = jnp.dot(hmid, wdn_ref[ci], preferred_element_type=f32)
        if ci == 0:
            acc_ref[...] = part
        else:
            acc_ref[...] += part
    out_ref[0] = _layer_norm(ALPHA * x + acc_ref[...], g_ref[...], b_ref[...])


def _ffn(x, wup, cw, cb, wdn, g, b):
    B, S, D = x.shape
    tm = min(TM_FFN, S)
    n2, _, cwid = wup.shape
    tok = pl.BlockSpec((1, tm, D), lambda b_, s: (b_, s, 0))
    return pl.pallas_call(
        _ffn_kernel,
        grid=(B, S // tm),
        in_specs=[tok] + [_const_spec(a.shape) for a in (wup, cw, cb, wdn, g, b)],
        out_specs=tok,
        out_shape=jax.ShapeDtypeStruct((B, S, D), f32),
        scratch_shapes=[pltpu.VMEM((n2, SUBLANES, cwid), f32),
                        pltpu.VMEM((tm + SUBLANES, cwid), f32),
                        pltpu.VMEM((tm, D), f32)],
        compiler_params=_params(2),
        name="ffn",
    )(x, wup, cw, cb, wdn, g, b)


def _prep_inproj(w_in):
    w = w_in.astype(bf16)
    D = w.shape[0]
    o = 4 * ML_WIDTH
    gates = w[:, o:o + 2 * ML_HEADS]
    wvog = jnp.concatenate(
        [w[:, 2 * ML_WIDTH:o], gates, jnp.zeros((D, LANES - 2 * ML_HEADS), bf16)], axis=1)
    o += 2 * ML_HEADS
    swq = w[:, o:o + SWA_WIDTH]
    o += SWA_WIDTH
    dup = []
    for _ in range(2):
        for kv in range(SWA_KV_HEADS):
            head = w[:, o + kv * SWA_HEAD_DIM:o + (kv + 1) * SWA_HEAD_DIM]
            dup += [head, head]
        o += SWA_KV_HEADS * SWA_HEAD_DIM
    return w[:, :2 * ML_WIDTH], wvog, jnp.concatenate([swq] + dup, axis=1)


def _chunk_cols(a, cwid):
    lead = a.shape[:-1]
    n = a.shape[-1] // cwid
    return jnp.moveaxis(a.reshape(lead + (n, cwid)), -2, 0)


def kernel(x, mem, rel_bias, w_in, ml_conv_w, ml_conv_b, ml_i_bias, ml_f_bias, ml_norm_g, swa_sinks, w_out, ln1_g, ln1_b, xa_wq, xa_wkv, xa_wo, ln2_g, ln2_b, ffn_w_up, ffn_conv_w, ffn_conv_b, ffn_w_down, ln3_g, ln3_b):
    depth = w_in.shape[0]
    row = lambda a: a.reshape(1, -1)
    bias = _swa_bias(rel_bias)
    for l in range(depth):
        wqk, wvog, wsw = _prep_inproj(w_in[l])
        qk, vo, gates, sw = _inproj(x, wqk, wvog, wsw, ml_conv_w[l], row(ml_conv_b[l]))
        gbias = jnp.concatenate([ml_i_bias[l], ml_f_bias[l]]).reshape(2 * ML_HEADS, 1)
        hml = _mlstm(qk, vo, gates, gbias, row(ml_norm_g[l]))
        hsw = _swa(sw, bias, swa_sinks[l])
        wo = w_out[l].astype(bf16)
        k, v = _memkv(mem, xa_wkv[l].astype(bf16))
        x = _mix(hml, hsw, x, wo[:ML_WIDTH], wo[ML_WIDTH:], row(ln1_g[l]), row(ln1_b[l]),
                 xa_wq[l].astype(bf16), k, v, xa_wo[l].astype(bf16), row(ln2_g[l]), row(ln2_b[l]))
        wup = _chunk_cols(ffn_w_up[l].astype(bf16), FF_CHUNK)
        cw = _chunk_cols(ffn_conv_w[l], FF_CHUNK)
        cb = _chunk_cols(row(ffn_conv_b[l]), FF_CHUNK)
        wdn = ffn_w_down[l].astype(bf16).reshape(-1, FF_CHUNK, ffn_w_down.shape[2])
        x = _ffn(x, wup, cw, cb, wdn, row(ln3_g[l]), row(ln3_b[l]))
    return x
```

```python
import functools
import math

import numpy as np
import jax
import jax.numpy as jnp
from jax import lax
from jax.experimental import pallas as pl
from jax.experimental.pallas import tpu as pltpu

f32 = jnp.float32
bf16 = jnp.bfloat16

ML_HEADS = 4
ML_HEAD_DIM = 128
ML_WIDTH = ML_HEADS * ML_HEAD_DIM
ML_CONV = 4
SWA_HEADS = 8
SWA_KV_HEADS = 2
SWA_GROUP = SWA_HEADS // SWA_KV_HEADS
SWA_HEAD_DIM = 64
SWA_WIDTH = SWA_HEADS * SWA_HEAD_DIM
WINDOW = 128
BLOCK = 128
REL_BUCKETS = 32
REL_MAX_DIST = 128
XA_HEADS = 4
FFN_CONV = 3
DEPTH = 2
ALPHA = (2.0 * DEPTH) ** 0.25
EPS = 1e-5

LANES = 128
SUBLANES = 8
TM_PROJ = 512
ML_CHUNK = 256
TQ_SWA = 512
TM_MIX = 512
TM_FFN = 512
FF_CHUNK = 256
FF_ROWS = 64
VMEM_LIMIT = 56 * 1024 * 1024
NEG_BIG = -1e30


def _params(n_axes, flags=None):
    return pltpu.CompilerParams(
        dimension_semantics=("arbitrary",) * n_axes, vmem_limit_bytes=VMEM_LIMIT, flags=flags)


def _const_spec(shape):
    nd = len(shape)
    return pl.BlockSpec(shape, lambda *_: (0,) * nd, pipeline_mode=pl.Buffered(1))


def _layer_norm(z, g, b):
    mu = jnp.mean(z, axis=-1, keepdims=True)
    zc = z - mu
    var = jnp.mean(zc * zc, axis=-1, keepdims=True)
    return zc * lax.rsqrt(var + EPS) * g + b


def _inproj_kernel(x_ref, wqk_ref, wvog_ref, wsw_ref, cw_ref, cb_ref,
                   qk_ref, vo_ref, g_ref, sw_ref, p_ref):
    tm = x_ref.shape[1]
    width = wqk_ref.shape[1]

    @pl.when(pl.program_id(1) == 0)
    def _():
        p_ref[0:SUBLANES, :] = jnp.zeros((SUBLANES, width), f32)

    xb = x_ref[0].astype(bf16)
    acc = jnp.dot(xb, wqk_ref[...], preferred_element_type=f32)
    p_ref[SUBLANES:SUBLANES + tm, :] = acc
    y = cb_ref[...] + p_ref[5:5 + tm, :] * cw_ref[0:1, :]
    y = y + p_ref[6:6 + tm, :] * cw_ref[1:2, :]
    y = y + p_ref[7:7 + tm, :] * cw_ref[2:3, :]
    y = y + acc * cw_ref[3:4, :]
    qk_ref[0] = (y * jax.nn.sigmoid(y)).astype(bf16)
    p_ref[0:SUBLANES, :] = p_ref[tm:tm + SUBLANES, :]

    accv = jnp.dot(xb, wvog_ref[...], preferred_element_type=f32)
    vo_w = vo_ref.shape[2]
    vo_ref[0] = accv[:, :vo_w].astype(bf16)
    g_ref[0] = accv[:, vo_w:].T[0:SUBLANES, :]
    sw_ref[0] = jnp.dot(xb, wsw_ref[...], preferred_element_type=f32).astype(bf16)


def _inproj(x, wqk, wvog, wsw, cw, cb):
    B, S, D = x.shape
    tm = min(TM_PROJ, S)
    grid = (B, S // tm)
    tok = lambda w: pl.BlockSpec((1, tm, w), lambda b, s: (b, s, 0))
    return pl.pallas_call(
        _inproj_kernel,
        grid=grid,
        in_specs=[tok(D), _const_spec(wqk.shape), _const_spec(wvog.shape), _const_spec(wsw.shape),
                  _const_spec(cw.shape), _const_spec(cb.shape)],
        out_specs=[tok(wqk.shape[1]), tok(2 * ML_WIDTH),
                   pl.BlockSpec((1, SUBLANES, tm), lambda b, s: (b, 0, s)),
                   tok(wsw.shape[1])],
        out_shape=[jax.ShapeDtypeStruct((B, S, wqk.shape[1]), bf16),
                   jax.ShapeDtypeStruct((B, S, 2 * ML_WIDTH), bf16),
                   jax.ShapeDtypeStruct((B, SUBLANES, S), f32),
                   jax.ShapeDtypeStruct((B, S, wsw.shape[1]), bf16)],
        scratch_shapes=[pltpu.VMEM((tm + SUBLANES, wqk.shape[1]), f32)],
        compiler_params=_params(2),
        name="inproj",
    )(x, wqk, wvog, wsw, cw, cb)


def _scan_lanes(x, op, fill):
    n = x.shape[1]
    lane = lax.broadcasted_iota(jnp.int32, x.shape, 1)
    sh = 1
    while sh < n:
        x = op(x, jnp.where(lane >= sh, pltpu.roll(x, sh, 1), fill))
        sh *= 2
    return x


def _mlstm_kernel(qk_ref, vo_ref, g_ref, gb_ref, ng_ref, out_ref, c_ref, m_ref):
    L = qk_ref.shape[1]
    dh = ML_HEAD_DIM
    scale = dh ** -0.5

    @pl.when(pl.program_id(1) == 0)
    def _():
        c_ref[...] = jnp.zeros(c_ref.shape, f32)
        m_ref[...] = jnp.zeros(m_ref.shape, f32)

    g = g_ref[0] + gb_ref[...]
    ig = jnp.concatenate([g[0:4], g[0:4]], axis=0)
    fg = jnp.concatenate([g[4:8], g[4:8]], axis=0)
    lf = jnp.minimum(fg, 0.0) - jnp.log1p(jnp.exp(-jnp.abs(fg)))
    b = _scan_lanes(lf, jnp.add, 0.0)
    c = ig - b
    m_prev = m_ref[:, 0:1]
    big_m = jnp.maximum(m_prev, _scan_lanes(c, jnp.maximum, -jnp.inf))
    m_last = big_m[:, L - 1:L]
    g_tot = b[:, L - 1:L]
    stack = jnp.concatenate(
        [c, big_m, b + big_m, jnp.zeros((LANES - 3 * SUBLANES, L), f32)], axis=0)
    cols = stack.T
    decay = jnp.exp(m_prev - m_last)

    row = lax.broadcasted_iota(jnp.int32, (L, L), 0)
    col = lax.broadcasted_iota(jnp.int32, (L, L), 1)
    causal = row >= col
    ones_col = (lax.broadcasted_iota(jnp.int32, (L, dh), 1) == 0).astype(bf16)

    for h in range(ML_HEADS):
        q = qk_ref[0, :, h * dh:(h + 1) * dh]
        k = qk_ref[0, :, ML_WIDTH + h * dh:ML_WIDTH + (h + 1) * dh]
        v = vo_ref[0, :, h * dh:(h + 1) * dh]
        o = vo_ref[0, :, ML_WIDTH + h * dh:ML_WIDTH + (h + 1) * dh]
        c_row = c[h:h + 1, :]
        c_col = cols[:, h:h + 1]
        m_col = cols[:, SUBLANES + h:SUBLANES + h + 1]
        bm_col = cols[:, 2 * SUBLANES + h:2 * SUBLANES + h + 1]

        p = jnp.where(causal, jnp.exp(c_row - m_col), 0.0)
        s = lax.dot_general(q, k, (((1,), (1,)), ((), ())), preferred_element_type=f32)
        s = (s * scale * p).astype(bf16)
        v_aug = jnp.concatenate([v, ones_col], axis=1)
        c_aug = c_ref[h]
        w_inter = jnp.exp(m_prev[h:h + 1, :] - m_col)
        nd = w_inter * jnp.dot(q, c_aug.astype(bf16), preferred_element_type=f32)
        nd = nd + jnp.dot(s, v_aug, preferred_element_type=f32)
        num = nd[:, :dh]
        den = nd[:, dh:dh + 1]
        hh = num / jnp.maximum(jnp.abs(den), jnp.exp(-bm_col))
        mu = jnp.mean(hh, axis=-1, keepdims=True)
        hc = hh - mu
        var = jnp.mean(hc * hc, axis=-1, keepdims=True)
        hn = hc * lax.rsqrt(var + EPS) * ng_ref[:, h * dh:(h + 1) * dh]
        out_ref[0, :, h * dh:(h + 1) * dh] = (jax.nn.sigmoid(o.astype(f32)) * hn).astype(bf16)

        wk = jnp.exp(c_col - m_last[h:h + 1, :]) * scale
        kw_t = (k.astype(f32) * wk).T.astype(bf16)
        c_ref[h] = decay[h:h + 1, :] * c_aug + jnp.dot(kw_t, v_aug, preferred_element_type=f32)

    m_ref[...] = jnp.broadcast_to(g_tot + m_last, m_ref.shape)


def _mlstm(qk, vo, gates, gbias, norm_g):
    B, S, _ = qk.shape
    L = min(ML_CHUNK, S)
    grid = (B, S // L)
    return pl.pallas_call(
        _mlstm_kernel,
        grid=grid,
        in_specs=[pl.BlockSpec((1, L, 2 * ML_WIDTH), lambda b, s: (b, s, 0)),
                  pl.BlockSpec((1, L, 2 * ML_WIDTH), lambda b, s: (b, s, 0)),
                  pl.BlockSpec((1, SUBLANES, L), lambda b, s: (b, 0, s)),
                  _const_spec(gbias.shape), _const_spec(norm_g.shape)],
        out_specs=pl.BlockSpec((1, L, ML_WIDTH), lambda b, s: (b, s, 0)),
        out_shape=jax.ShapeDtypeStruct((B, S, ML_WIDTH), bf16),
        scratch_shapes=[pltpu.VMEM((ML_HEADS, ML_HEAD_DIM, 2 * ML_HEAD_DIM), f32),
                        pltpu.VMEM((SUBLANES, LANES), f32)],
        compiler_params=_params(2),
        name="mlstm",
    )(qk, vo, gates, gbias, norm_g)


def _t5_bucket_table():
    r = np.arange(BLOCK)[:, None]
    c = np.arange(2 * BLOCK)[None, :]
    dist = BLOCK + r - c
    n = np.maximum(dist, 0)
    max_exact = REL_BUCKETS // 2
    nf = np.maximum(n, 1).astype(np.float32)
    large = max_exact + (np.log(nf / np.float32(max_exact)) / np.float32(math.log(REL_MAX_DIST / max_exact))
                         * np.float32(REL_BUCKETS - max_exact)).astype(np.int32)
    large = np.minimum(large, REL_BUCKETS - 1)
    bucket = np.where(n < max_exact, n, large)
    valid = (dist >= 0) & (dist < WINDOW)
    return np.where(valid, bucket, -1).astype(np.int32)


def _bias_kernel(bucket_ref, rb_ref, out_ref):
    bucket = bucket_ref[...]
    for h in range(SWA_HEADS):
        acc = jnp.full(bucket.shape, NEG_BIG, f32)
        for i in range(REL_BUCKETS):
            acc = jnp.where(bucket == i, rb_ref[i, h], acc)
        out_ref[h] = acc


def _swa_bias(rel_bias):
    bucket = jnp.asarray(_t5_bucket_table())
    return pl.pallas_call(
        _bias_kernel,
        in_specs=[pl.BlockSpec(memory_space=pltpu.VMEM), pl.BlockSpec(memory_space=pltpu.SMEM)],
        out_specs=pl.BlockSpec(memory_space=pltpu.VMEM),
        out_shape=jax.ShapeDtypeStruct((SWA_HEADS, BLOCK, 2 * BLOCK), f32),
        name="swa_bias",
    )(bucket, rel_bias)


def _swa_kernel(sw_ref, prev_ref, bias_ref, sink_ref, out_ref):
    tq = sw_ref.shape[1]
    scale = SWA_HEAD_DIM ** -0.5
    pair = 2 * SWA_HEAD_DIM
    k_off = SWA_WIDTH
    v_off = SWA_WIDTH + SWA_KV_HEADS * pair
    first_tile = pl.program_id(1) == 0
    lo = lax.broadcasted_iota(jnp.int32, (BLOCK, pair), 1) < SWA_HEAD_DIM
    left = lax.broadcasted_iota(jnp.int32, (BLOCK, 2 * BLOCK), 1) < BLOCK

    for j in range(tq // BLOCK):
        rows = slice(j * BLOCK, (j + 1) * BLOCK)
        prows = slice((j - 1) * BLOCK, j * BLOCK)
        for kv in range(SWA_KV_HEADS):
            kc = slice(k_off + kv * pair, k_off + (kv + 1) * pair)
            vc = slice(v_off + kv * pair, v_off + (kv + 1) * pair)
            if j == 0:
                k_prev = prev_ref[0, :, kv * pair:(kv + 1) * pair]
                v_prev = prev_ref[0, :, (SWA_KV_HEADS + kv) * pair:(SWA_KV_HEADS + kv + 1) * pair]
            else:
                k_prev = sw_ref[0, prows, kc]
                v_prev = sw_ref[0, prows, vc]
            kband = jnp.concatenate([k_prev, sw_ref[0, rows, kc]], axis=0)
            vband = jnp.concatenate([v_prev, sw_ref[0, rows, vc]], axis=0)
            q0 = sw_ref[0, rows, kv * 2 * pair:kv * 2 * pair + pair]
            q1 = sw_ref[0, rows, kv * 2 * pair + pair:(kv + 1) * 2 * pair]
            zero = jnp.zeros_like(q0)
            qs = jnp.concatenate([jnp.where(lo, q0, zero), jnp.where(lo, zero, q0),
                                  jnp.where(lo, q1, zero), jnp.where(lo, zero, q1)], axis=0)
            logits = lax.dot_general(qs, kband, (((1,), (1,)), ((), ())),
                                     preferred_element_type=f32)
            ps, dens = [], []
            for gq in range(SWA_GROUP):
                head = kv * SWA_GROUP + gq
                lg = logits[gq * BLOCK:(gq + 1) * BLOCK] * scale + bias_ref[head]
                if j == 0:
                    lg = jnp.where(jnp.logical_and(first_tile, left), NEG_BIG, lg)
                sink = sink_ref[head]
                mx = jnp.maximum(jnp.max(lg, axis=-1, keepdims=True), sink)
                p = jnp.exp(lg - mx)
                dens.append(jnp.sum(p, axis=-1, keepdims=True) + jnp.exp(sink - mx))
                ps.append(p.astype(bf16))
            pv = jnp.dot(jnp.concatenate(ps, axis=0), vband, preferred_element_type=f32)
            o = [pv[gq * BLOCK:(gq + 1) * BLOCK] / dens[gq] for gq in range(SWA_GROUP)]
            out_ref[0, rows, kv * 2 * pair:kv * 2 * pair + pair] = jnp.where(lo, o[0], o[1]).astype(bf16)
            out_ref[0, rows, kv * 2 * pair + pair:(kv + 1) * 2 * pair] = jnp.where(lo, o[2], o[3]).astype(bf16)


def _swa(sw, bias, sinks):
    B, S, W = sw.shape
    tq = min(TQ_SWA, S)
    nblk = tq // BLOCK
    kvw = W - SWA_WIDTH
    assert kvw == SWA_WIDTH, "k/v column group must be as wide as the q group"
    return pl.pallas_call(
        _swa_kernel,
        grid=(B, S // tq),
        in_specs=[pl.BlockSpec((1, tq, W), lambda b, s: (b, s, 0)),
                  pl.BlockSpec((1, BLOCK, kvw), lambda b, s: (b, jnp.maximum(s * nblk - 1, 0), 1)),
                  _const_spec(bias.shape),
                  pl.BlockSpec(memory_space=pltpu.SMEM)],
        out_specs=pl.BlockSpec((1, tq, SWA_WIDTH), lambda b, s: (b, s, 0)),
        out_shape=jax.ShapeDtypeStruct((B, S, SWA_WIDTH), bf16),
        compiler_params=_params(2),
        name="swa",
    )(sw, sw, bias, sinks)


def _memkv_kernel(mem_ref, wkv_ref, k_ref, v_ref):
    d = k_ref.shape[2]
    kv = jnp.dot(mem_ref[0].astype(bf16), wkv_ref[...], preferred_element_type=f32)
    k_ref[0] = kv[:, :d].astype(bf16)
    v_ref[0] = kv[:, d:].astype(bf16)


def _memkv(mem, wkv):
    B, M, D = mem.shape
    blk = pl.BlockSpec((1, M, D), lambda b: (b, 0, 0))
    return pl.pallas_call(
        _memkv_kernel,
        grid=(B,),
        in_specs=[blk, _const_spec(wkv.shape)],
        out_specs=[blk, blk],
        out_shape=[jax.ShapeDtypeStruct((B, M, D), bf16)] * 2,
        compiler_params=_params(1),
        name="memkv",
    )(mem, wkv)


def _mix_kernel(hml_ref, hsw_ref, x_ref, woml_ref, wosw_ref, g1_ref, b1_ref,
                wq_ref, k_ref, v_ref, wo_ref, g2_ref, b2_ref, out_ref):
    d = x_ref.shape[2]
    dh = d // XA_HEADS
    h = jnp.dot(hml_ref[0], woml_ref[...], preferred_element_type=f32)
    h = h + jnp.dot(hsw_ref[0], wosw_ref[...], preferred_element_type=f32)
    x1 = _layer_norm(ALPHA * x_ref[0] + h, g1_ref[...], b1_ref[...])

    q = jnp.dot(x1.astype(bf16), wq_ref[...], preferred_element_type=f32).astype(bf16)
    outs = []
    for a in range(XA_HEADS):
        cs = slice(a * dh, (a + 1) * dh)
        lg = lax.dot_general(q[:, cs], k_ref[0, :, cs], (((1,), (1,)), ((), ())),
                             preferred_element_type=f32) * (dh ** -0.5)
        mx = jnp.max(lg, axis=-1, keepdims=True)
        p = jnp.exp(lg - mx)
        den = jnp.sum(p, axis=-1, keepdims=True)
        o = jnp.dot(p.astype(bf16), v_ref[0, :, cs], preferred_element_type=f32) / den
        outs.append(o.astype(bf16))
    o = jnp.concatenate(outs, axis=1)
    h2 = jnp.dot(o, wo_ref[...], preferred_element_type=f32)
    out_ref[0] = _layer_norm(ALPHA * x1 + h2, g2_ref[...], b2_ref[...])


def _mix(hml, hsw, x, woml, wosw, g1, b1, wq, k, v, wo, g2, b2):
    B, S, D = x.shape
    tm = min(TM_MIX, S)
    M = k.shape[1]
    tok = lambda w: pl.BlockSpec((1, tm, w), lambda b, s: (b, s, 0))
    memblk = pl.BlockSpec((1, M, D), lambda b, s: (b, 0, 0))
    consts = [woml, wosw, g1, b1, wq]
    return pl.pallas_call(
        _mix_kernel,
        grid=(B, S // tm),
        in_specs=[tok(hml.shape[2]), tok(hsw.shape[2]), tok(D)]
                 + [_const_spec(a.shape) for a in consts]
                 + [memblk, memblk, _const_spec(wo.shape), _const_spec(g2.shape), _const_spec(b2.shape)],
        out_specs=tok(D),
        out_shape=jax.ShapeDtypeStruct((B, S, D), f32),
        compiler_params=_params(2),
        name="mix",
    )(hml, hsw, x, woml, wosw, g1, b1, wq, k, v, wo, g2, b2)


def _gelu_tanh(x):
    c = math.sqrt(2.0 / math.pi)
    return x * (0.5 * (1.0 + jnp.tanh(c * (x + 0.044715 * (x * x * x)))))


def _ffn_kernel(x_ref, wup_ref, cw_ref, cb_ref, wdn_ref, g_ref, b_ref, out_ref,
                carry_ref, xb_ref, u00_ref, u01_ref, u10_ref, u11_ref, h_ref):
    tm = x_ref.shape[1]
    n_chunks = wup_ref.shape[0] // 2
    cwid = wup_ref.shape[2]
    halo = SUBLANES
    u_refs = ((u00_ref, u01_ref), (u10_ref, u11_ref))

    @pl.when(pl.program_id(1) == 0)
    def _():
        carry_ref[...] = jnp.zeros(carry_ref.shape, f32)

    xb_ref[...] = x_ref[0].astype(bf16)

    n_slabs = cwid // LANES

    def up(ci):
        for u_ref, idx in zip(u_refs[ci % 2], (ci, n_chunks + ci)):
            u = jnp.dot(xb_ref[...], wup_ref[idx], preferred_element_type=f32)
            for sl in range(n_slabs):
                lanes = slice(sl * LANES, (sl + 1) * LANES)
                u_ref[sl, 0:halo, :] = carry_ref[idx, :, lanes]
                u_ref[sl, halo:halo + tm, :] = u[:, lanes]
                carry_ref[idx, :, lanes] = u_ref[sl, tm:tm + halo, :]

    def act(ci):
        for r0 in range(0, tm, FF_ROWS):
            for sl in range(n_slabs):
                lanes = slice(sl * LANES, (sl + 1) * LANES)
                ys = []
                for u_ref, idx in zip(u_refs[ci % 2], (ci, n_chunks + ci)):
                    y = cb_ref[idx, :, lanes] + u_ref[sl, r0 + 6:r0 + 6 + FF_ROWS, :] * cw_ref[idx, 0:1, lanes]
                    y = y + u_ref[sl, r0 + 7:r0 + 7 + FF_ROWS, :] * cw_ref[idx, 1:2, lanes]
                    y = y + u_ref[sl, r0 + 8:r0 + 8 + FF_ROWS, :] * cw_ref[idx, 2:3, lanes]
                    ys.append(y)
                h_ref[r0:r0 + FF_ROWS, ci * cwid + sl * LANES:ci * cwid + (sl + 1) * LANES] = (
                    _gelu_tanh(ys[0]) * ys[1]).astype(bf16)

    up(0)
    for ci in range(n_chunks):
        if ci + 1 < n_chunks:
            up(ci + 1)
        act(ci)
    h = jnp.dot(h_ref[...], wdn_ref[...], preferred_element_type=f32)
    out_ref[0] = _layer_norm(ALPHA * x_ref[0] + h, g_ref[...], b_ref[...])


def _ffn(x, wup, cw, cb, wdn, g, b):
    B, S, D = x.shape
    tm = min(TM_FFN, S)
    n2, _, cwid = wup.shape
    tok = pl.BlockSpec((1, tm, D), lambda b_, s: (b_, s, 0))
    return pl.pallas_call(
        _ffn_kernel,
        grid=(B, S // tm),
        in_specs=[tok] + [_const_spec(a.shape) for a in (wup, cw, cb, wdn, g, b)],
        out_specs=tok,
        out_shape=jax.ShapeDtypeStruct((B, S, D), f32),
        scratch_shapes=[pltpu.VMEM((n2, SUBLANES, cwid), f32),
                        pltpu.VMEM((tm, D), bf16),
                        ]
                       + [pltpu.VMEM((cwid // LANES, tm + SUBLANES, LANES), f32)] * 4
                       + [pltpu.VMEM((tm, wdn.shape[0]), bf16)],
        compiler_params=_params(2),
        name="ffn",
    )(x, wup, cw, cb, wdn, g, b)


def _prep_inproj(w_in):
    w = w_in.astype(bf16)
    D = w.shape[0]
    o = 4 * ML_WIDTH
    gates = w[:, o:o + 2 * ML_HEADS]
    wvog = jnp.concatenate(
        [w[:, 2 * ML_WIDTH:o], gates, jnp.zeros((D, LANES - 2 * ML_HEADS), bf16)], axis=1)
    o += 2 * ML_HEADS
    swq = w[:, o:o + SWA_WIDTH]
    o += SWA_WIDTH
    dup = []
    for _ in range(2):
        for kv in range(SWA_KV_HEADS):
            head = w[:, o + kv * SWA_HEAD_DIM:o + (kv + 1) * SWA_HEAD_DIM]
            dup += [head, head]
        o += SWA_KV_HEADS * SWA_HEAD_DIM
    return w[:, :2 * ML_WIDTH], wvog, jnp.concatenate([swq] + dup, axis=1)


def _chunk_cols(a, cwid):
    lead = a.shape[:-1]
    n = a.shape[-1] // cwid
    return jnp.moveaxis(a.reshape(lead + (n, cwid)), -2, 0)


def kernel(x, mem, rel_bias, w_in, ml_conv_w, ml_conv_b, ml_i_bias, ml_f_bias, ml_norm_g, swa_sinks, w_out, ln1_g, ln1_b, xa_wq, xa_wkv, xa_wo, ln2_g, ln2_b, ffn_w_up, ffn_conv_w, ffn_conv_b, ffn_w_down, ln3_g, ln3_b):
    depth = w_in.shape[0]
    row = lambda a: a.reshape(1, -1)
    bias = _swa_bias(rel_bias)
    for l in range(depth):
        wqk, wvog, wsw = _prep_inproj(w_in[l])
        qk, vo, gates, sw = _inproj(x, wqk, wvog, wsw, ml_conv_w[l], row(ml_conv_b[l]))
        gbias = jnp.concatenate([ml_i_bias[l], ml_f_bias[l]]).reshape(2 * ML_HEADS, 1)
        hml = _mlstm(qk, vo, gates, gbias, row(ml_norm_g[l]))
        hsw = _swa(sw, bias, swa_sinks[l])
        wo = w_out[l].astype(bf16)
        k, v = _memkv(mem, xa_wkv[l].astype(bf16))
        x = _mix(hml, hsw, x, wo[:ML_WIDTH], wo[ML_WIDTH:], row(ln1_g[l]), row(ln1_b[l]),
                 xa_wq[l].astype(bf16), k, v, xa_wo[l].astype(bf16), row(ln2_g[l]), row(ln2_b[l]))
        wup = _chunk_cols(ffn_w_up[l].astype(bf16), FF_CHUNK)
        cw = _chunk_cols(ffn_conv_w[l], FF_CHUNK)
        cb = _chunk_cols(row(ffn_conv_b[l]), FF_CHUNK)
        wdn = ffn_w_down[l].astype(bf16)
        x = _ffn(x, wup, cw, cb, wdn, row(ln3_g[l]), row(ln3_b[l]))
    return x
```

```python
import functools
import math

import numpy as np
import jax
import jax.numpy as jnp
from jax import lax
from jax.experimental import pallas as pl
from jax.experimental.pallas import tpu as pltpu

f32 = jnp.float32
bf16 = jnp.bfloat16

ML_HEADS = 4
ML_HEAD_DIM = 128
ML_WIDTH = ML_HEADS * ML_HEAD_DIM
ML_CONV = 4
SWA_HEADS = 8
SWA_KV_HEADS = 2
SWA_GROUP = SWA_HEADS // SWA_KV_HEADS
SWA_HEAD_DIM = 64
SWA_WIDTH = SWA_HEADS * SWA_HEAD_DIM
WINDOW = 128
BLOCK = 128
REL_BUCKETS = 32
REL_MAX_DIST = 128
XA_HEADS = 4
FFN_CONV = 3
DEPTH = 2
ALPHA = (2.0 * DEPTH) ** 0.25
EPS = 1e-5

LANES = 128
SUBLANES = 8
TM_PROJ = 512
PROJ_ROWS = 128
ML_CHUNK = 256
TQ_SWA = 512
TM_MIX = 512
TM_FFN = 512
FF_CHUNK = 256
FF_ROWS = 64
VMEM_LIMIT = 56 * 1024 * 1024
NEG_BIG = -1e30


def _params(n_axes, flags=None):
    return pltpu.CompilerParams(
        dimension_semantics=("arbitrary",) * n_axes, vmem_limit_bytes=VMEM_LIMIT, flags=flags)


def _const_spec(shape):
    nd = len(shape)
    return pl.BlockSpec(shape, lambda *_: (0,) * nd, pipeline_mode=pl.Buffered(1))


def _layer_norm(z, g, b):
    mu = jnp.mean(z, axis=-1, keepdims=True)
    zc = z - mu
    var = jnp.mean(zc * zc, axis=-1, keepdims=True)
    return zc * lax.rsqrt(var + EPS) * g + b


def _inproj_kernel(x_ref, wqk_ref, wvog_ref, wsw_ref, cw_ref, cb_ref,
                   qk_ref, vo_ref, g_ref, sw_ref, xb_ref, p_ref):
    tm = x_ref.shape[1]
    n_slabs = p_ref.shape[0]
    halo = SUBLANES

    @pl.when(pl.program_id(1) == 0)
    def _():
        p_ref[:, 0:halo, :] = jnp.zeros((n_slabs, halo, LANES), f32)

    xb_ref[...] = x_ref[0].astype(bf16)
    acc = jnp.dot(xb_ref[...], wqk_ref[...], preferred_element_type=f32)
    for sl in range(n_slabs):
        p_ref[sl, halo:halo + tm, :] = acc[:, sl * LANES:(sl + 1) * LANES]

    accv = jnp.dot(xb_ref[...], wvog_ref[...], preferred_element_type=f32)
    vo_w = vo_ref.shape[2]
    vo_ref[0] = accv[:, :vo_w].astype(bf16)
    g_ref[0] = accv[:, vo_w:].T[0:SUBLANES, :]
    sw_ref[0] = jnp.dot(xb_ref[...], wsw_ref[...], preferred_element_type=f32).astype(bf16)

    for sl in range(n_slabs):
        lanes = slice(sl * LANES, (sl + 1) * LANES)
        for r0 in range(0, tm, PROJ_ROWS):
            y = cb_ref[:, lanes] + p_ref[sl, r0 + 5:r0 + 5 + PROJ_ROWS, :] * cw_ref[0:1, lanes]
            y = y + p_ref[sl, r0 + 6:r0 + 6 + PROJ_ROWS, :] * cw_ref[1:2, lanes]
            y = y + p_ref[sl, r0 + 7:r0 + 7 + PROJ_ROWS, :] * cw_ref[2:3, lanes]
            y = y + p_ref[sl, r0 + 8:r0 + 8 + PROJ_ROWS, :] * cw_ref[3:4, lanes]
            qk_ref[0, r0:r0 + PROJ_ROWS, lanes] = (y * jax.nn.sigmoid(y)).astype(bf16)
        p_ref[sl, 0:halo, :] = p_ref[sl, tm:tm + halo, :]


def _inproj(x, wqk, wvog, wsw, cw, cb):
    B, S, D = x.shape
    tm = min(TM_PROJ, S)
    grid = (B, S // tm)
    tok = lambda w: pl.BlockSpec((1, tm, w), lambda b, s: (b, s, 0))
    return pl.pallas_call(
        _inproj_kernel,
        grid=grid,
        in_specs=[tok(D), _const_spec(wqk.shape), _const_spec(wvog.shape), _const_spec(wsw.shape),
                  _const_spec(cw.shape), _const_spec(cb.shape)],
        out_specs=[tok(wqk.shape[1]), tok(2 * ML_WIDTH),
                   pl.BlockSpec((1, SUBLANES, tm), lambda b, s: (b, 0, s)),
                   tok(wsw.shape[1])],
        out_shape=[jax.ShapeDtypeStruct((B, S, wqk.shape[1]), bf16),
                   jax.ShapeDtypeStruct((B, S, 2 * ML_WIDTH), bf16),
                   jax.ShapeDtypeStruct((B, SUBLANES, S), f32),
                   jax.ShapeDtypeStruct((B, S, wsw.shape[1]), bf16)],
        scratch_shapes=[pltpu.VMEM((tm, D), bf16),
                        pltpu.VMEM((wqk.shape[1] // LANES, tm + SUBLANES, LANES), f32)],
        compiler_params=_params(2),
        name="inproj",
    )(x, wqk, wvog, wsw, cw, cb)


C_LANE, CM_LANE, B_LANE = 0, SUBLANES, 2 * SUBLANES


def _scan_chunks(x, op, fill, chunk):
    pos = lax.broadcasted_iota(jnp.int32, x.shape, 1) % chunk
    sh = 1
    while sh < chunk:
        x = op(x, jnp.where(pos >= sh, pltpu.roll(x, sh, 1), fill))
        sh *= 2
    return x


def _mlgate_kernel(g_ref, gb_ref, rows_ref, cols_ref, *, chunk):
    g = g_ref[0] + gb_ref[...]
    ig = jnp.concatenate([g[0:4], g[0:4]], axis=0)
    fg = jnp.concatenate([g[4:8], g[4:8]], axis=0)
    lf = jnp.minimum(fg, 0.0) - jnp.log1p(jnp.exp(-jnp.abs(fg)))
    b = _scan_chunks(lf, jnp.add, 0.0, chunk)
    c = ig - b
    cm = _scan_chunks(c, jnp.maximum, -jnp.inf, chunk)
    rows_ref[0] = c
    stack = jnp.concatenate(
        [c, cm, b, jnp.zeros((LANES - 3 * SUBLANES, c.shape[1]), f32)], axis=0)
    cols_ref[0] = stack.T


def _mlgate(gates, gbias, chunk):
    B, _, S = gates.shape
    return pl.pallas_call(
        functools.partial(_mlgate_kernel, chunk=chunk),
        grid=(B,),
        in_specs=[pl.BlockSpec((1, SUBLANES, S), lambda b: (b, 0, 0)), _const_spec(gbias.shape)],
        out_specs=[pl.BlockSpec((1, SUBLANES, S), lambda b: (b, 0, 0)),
                   pl.BlockSpec((1, S, LANES), lambda b: (b, 0, 0))],
        out_shape=[jax.ShapeDtypeStruct((B, SUBLANES, S), f32),
                   jax.ShapeDtypeStruct((B, S, LANES), f32)],
        compiler_params=_params(1),
        name="mlgate",
    )(gates, gbias)


def _mlstm_kernel(qk_ref, vo_ref, rows_ref, cols_ref, ng_ref, out_ref, c_ref, m_ref):
    L = qk_ref.shape[1]
    dh = ML_HEAD_DIM
    scale = dh ** -0.5

    @pl.when(pl.program_id(1) == 0)
    def _():
        c_ref[...] = jnp.zeros(c_ref.shape, f32)
        m_ref[...] = jnp.zeros(m_ref.shape, f32)

    row = lax.broadcasted_iota(jnp.int32, (L, L), 0)
    col = lax.broadcasted_iota(jnp.int32, (L, L), 1)
    causal = row >= col
    ones_blk = jnp.ones((L, dh), bf16)

    heads = range(ML_HEADS)
    hs = lambda h: slice(h * dh, (h + 1) * dh)
    ks = lambda h: slice(ML_WIDTH + h * dh, ML_WIDTH + (h + 1) * dh)
    m_all = m_ref[...]

    c_aug = [c_ref[h] for h in heads]
    v_aug = [jnp.concatenate([vo_ref[0, :, hs(h)], ones_blk], axis=1) for h in heads]
    qk_t = [lax.dot_general(qk_ref[0, :, hs(h)], qk_ref[0, :, ks(h)], (((1,), (1,)), ((), ())),
                            preferred_element_type=f32) for h in heads]
    q_c = [jnp.dot(qk_ref[0, :, hs(h)], c_aug[h].astype(bf16), preferred_element_type=f32)
           for h in heads]

    c_row, cm_b, big_m, m_prev, s_mat = [], [], [], [], []
    for h in heads:
        c_row.append(rows_ref[0, h:h + 1, :])
        cm_b.append(jnp.broadcast_to(cols_ref[0, :, CM_LANE + h:CM_LANE + h + 1], (L, dh)))
        m_prev.append(m_all[h:h + 1, 0:1])
        big_m.append(jnp.maximum(m_prev[h], cm_b[h]))
        cm_wide = jnp.concatenate([cm_b[h]] * (L // dh), axis=1)
        p = jnp.where(causal, jnp.exp(c_row[h] - cm_wide), 0.0)
        s_mat.append((qk_t[h] * scale * p).astype(bf16))
    s_v = [jnp.dot(s_mat[h], v_aug[h], preferred_element_type=f32) for h in heads]

    m_last, kw_t = [], []
    for h in heads:
        m_last.append(big_m[h][L - 1:L, 0:1])
        wk = jnp.exp(c_row[h] - m_last[h]) * scale
        kw_t.append((qk_ref[0, :, ks(h)].astype(f32).T * wk).astype(bf16))
    k_v = [jnp.dot(kw_t[h], v_aug[h], preferred_element_type=f32) for h in heads]

    m_new = []
    for h in heads:
        b_col = cols_ref[0, :, B_LANE + h:B_LANE + h + 1]
        b_b = jnp.broadcast_to(b_col, (L, dh))
        w_intra = jnp.exp(cm_b[h] - big_m[h])
        w_inter = jnp.exp(m_prev[h] - big_m[h])
        num = w_inter * q_c[h][:, :dh] + w_intra * s_v[h][:, :dh]
        den = w_inter * q_c[h][:, dh:] + w_intra * s_v[h][:, dh:]
        hh = num / jnp.maximum(jnp.abs(den), jnp.exp(-(b_b + big_m[h])))
        mu = jnp.mean(hh, axis=-1, keepdims=True)
        hc = hh - mu
        var = jnp.mean(hc * hc, axis=-1, keepdims=True)
        hn = hc * lax.rsqrt(var + EPS) * ng_ref[:, hs(h)]
        o = vo_ref[0, :, ks(h)]
        out_ref[0, :, hs(h)] = (jax.nn.sigmoid(o.astype(f32)) * hn).astype(bf16)
        c_ref[h] = jnp.exp(m_prev[h] - m_last[h]) * c_aug[h] + k_v[h]
        m_new.append(jnp.broadcast_to(b_col[L - 1:L, :] + m_last[h], (1, LANES)))
    m_ref[0:ML_HEADS, :] = jnp.concatenate(m_new, axis=0)


def _mlstm(qk, vo, gates, gbias, norm_g):
    B, S, _ = qk.shape
    L = min(ML_CHUNK, S)
    rows, cols = _mlgate(gates, gbias, L)
    grid = (B, S // L)
    return pl.pallas_call(
        _mlstm_kernel,
        grid=grid,
        in_specs=[pl.BlockSpec((1, L, 2 * ML_WIDTH), lambda b, s: (b, s, 0)),
                  pl.BlockSpec((1, L, 2 * ML_WIDTH), lambda b, s: (b, s, 0)),
                  pl.BlockSpec((1, SUBLANES, L), lambda b, s: (b, 0, s)),
                  pl.BlockSpec((1, L, LANES), lambda b, s: (b, s, 0)),
                  _const_spec(norm_g.shape)],
        out_specs=pl.BlockSpec((1, L, ML_WIDTH), lambda b, s: (b, s, 0)),
        out_shape=jax.ShapeDtypeStruct((B, S, ML_WIDTH), bf16),
        scratch_shapes=[pltpu.VMEM((ML_HEADS, ML_HEAD_DIM, 2 * ML_HEAD_DIM), f32),
                        pltpu.VMEM((SUBLANES, LANES), f32)],
        compiler_params=_params(2),
        name="mlstm",
    )(qk, vo, rows, cols, norm_g)


def _t5_bucket_table():
    r = np.arange(BLOCK)[:, None]
    c = np.arange(2 * BLOCK)[None, :]
    dist = BLOCK + r - c
    n = np.maximum(dist, 0)
    max_exact = REL_BUCKETS // 2
    nf = np.maximum(n, 1).astype(np.float32)
    large = max_exact + (np.log(nf / np.float32(max_exact)) / np.float32(math.log(REL_MAX_DIST / max_exact))
                         * np.float32(REL_BUCKETS - max_exact)).astype(np.int32)
    large = np.minimum(large, REL_BUCKETS - 1)
    bucket = np.where(n < max_exact, n, large)
    valid = (dist >= 0) & (dist < WINDOW)
    return np.where(valid, bucket, -1).astype(np.int32)


def _bias_kernel(bucket_ref, rb_ref, out_ref):
    bucket = bucket_ref[...]
    for h in range(SWA_HEADS):
        acc = jnp.full(bucket.shape, NEG_BIG, f32)
        for i in range(REL_BUCKETS):
            acc = jnp.where(bucket == i, rb_ref[i, h], acc)
        out_ref[h] = acc


def _swa_bias(rel_bias):
    bucket = jnp.asarray(_t5_bucket_table())
    return pl.pallas_call(
        _bias_kernel,
        in_specs=[pl.BlockSpec(memory_space=pltpu.VMEM), pl.BlockSpec(memory_space=pltpu.SMEM)],
        out_specs=pl.BlockSpec(memory_space=pltpu.VMEM),
        out_shape=jax.ShapeDtypeStruct((SWA_HEADS, BLOCK, 2 * BLOCK), f32),
        name="swa_bias",
    )(bucket, rel_bias)


def _swa_kernel(sw_ref, prev_ref, bias_ref, sink_ref, out_ref):
    tq = sw_ref.shape[1]
    scale = SWA_HEAD_DIM ** -0.5
    pair = 2 * SWA_HEAD_DIM
    k_off = SWA_WIDTH
    v_off = SWA_WIDTH + SWA_KV_HEADS * pair
    first_tile = pl.program_id(1) == 0
    lo = lax.broadcasted_iota(jnp.int32, (BLOCK, pair), 1) < SWA_HEAD_DIM
    left = lax.broadcasted_iota(jnp.int32, (BLOCK, 2 * BLOCK), 1) < BLOCK

    for j in range(tq // BLOCK):
        rows = slice(j * BLOCK, (j + 1) * BLOCK)
        prows = slice((j - 1) * BLOCK, j * BLOCK)
        for kv in range(SWA_KV_HEADS):
            kc = slice(k_off + kv * pair, k_off + (kv + 1) * pair)
            vc = slice(v_off + kv * pair, v_off + (kv + 1) * pair)
            if j == 0:
                k_prev = prev_ref[0, :, kv * pair:(kv + 1) * pair]
                v_prev = prev_ref[0, :, (SWA_KV_HEADS + kv) * pair:(SWA_KV_HEADS + kv + 1) * pair]
            else:
                k_prev = sw_ref[0, prows, kc]
                v_prev = sw_ref[0, prows, vc]
            kband = jnp.concatenate([k_prev, sw_ref[0, rows, kc]], axis=0)
            vband = jnp.concatenate([v_prev, sw_ref[0, rows, vc]], axis=0)
            q0 = sw_ref[0, rows, kv * 2 * pair:kv * 2 * pair + pair]
            q1 = sw_ref[0, rows, kv * 2 * pair + pair:(kv + 1) * 2 * pair]
            zero = jnp.zeros_like(q0)
            qs = jnp.concatenate([jnp.where(lo, q0, zero), jnp.where(lo, zero, q0),
                                  jnp.where(lo, q1, zero), jnp.where(lo, zero, q1)], axis=0)
            logits = lax.dot_general(qs, kband, (((1,), (1,)), ((), ())),
                                     preferred_element_type=f32)
            ps, dens = [], []
            for gq in range(SWA_GROUP):
                head = kv * SWA_GROUP + gq
                lg = logits[gq * BLOCK:(gq + 1) * BLOCK] * scale + bias_ref[head]
                if j == 0:
                    lg = jnp.where(jnp.logical_and(first_tile, left), NEG_BIG, lg)
                sink = sink_ref[head]
                mx = jnp.maximum(jnp.max(lg, axis=-1, keepdims=True), sink)
                p = jnp.exp(lg - mx)
                dens.append(jnp.sum(p, axis=-1, keepdims=True) + jnp.exp(sink - mx))
                ps.append(p.astype(bf16))
            pv = jnp.dot(jnp.concatenate(ps, axis=0), vband, preferred_element_type=f32)
            o = [pv[gq * BLOCK:(gq + 1) * BLOCK] / dens[gq] for gq in range(SWA_GROUP)]
            out_ref[0, rows, kv * 2 * pair:kv * 2 * pair + pair] = jnp.where(lo, o[0], o[1]).astype(bf16)
            out_ref[0, rows, kv * 2 * pair + pair:(kv + 1) * 2 * pair] = jnp.where(lo, o[2], o[3]).astype(bf16)


def _swa(sw, bias, sinks):
    B, S, W = sw.shape
    tq = min(TQ_SWA, S)
    nblk = tq // BLOCK
    kvw = W - SWA_WIDTH
    assert kvw == SWA_WIDTH, "k/v column group must be as wide as the q group"
    return pl.pallas_call(
        _swa_kernel,
        grid=(B, S // tq),
        in_specs=[pl.BlockSpec((1, tq, W), lambda b, s: (b, s, 0)),
                  pl.BlockSpec((1, BLOCK, kvw), lambda b, s: (b, jnp.maximum(s * nblk - 1, 0), 1)),
                  _const_spec(bias.shape),
                  pl.BlockSpec(memory_space=pltpu.SMEM)],
        out_specs=pl.BlockSpec((1, tq, SWA_WIDTH), lambda b, s: (b, s, 0)),
        out_shape=jax.ShapeDtypeStruct((B, S, SWA_WIDTH), bf16),
        compiler_params=_params(2),
        name="swa",
    )(sw, sw, bias, sinks)


def _memkv_kernel(mem_ref, wkv_ref, k_ref, v_ref):
    d = k_ref.shape[2]
    kv = jnp.dot(mem_ref[0].astype(bf16), wkv_ref[...], preferred_element_type=f32)
    k_ref[0] = kv[:, :d].astype(bf16)
    v_ref[0] = kv[:, d:].astype(bf16)


def _memkv(mem, wkv):
    B, M, D = mem.shape
    blk = pl.BlockSpec((1, M, D), lambda b: (b, 0, 0))
    return pl.pallas_call(
        _memkv_kernel,
        grid=(B,),
        in_specs=[blk, _const_spec(wkv.shape)],
        out_specs=[blk, blk],
        out_shape=[jax.ShapeDtypeStruct((B, M, D), bf16)] * 2,
        compiler_params=_params(1),
        name="memkv",
    )(mem, wkv)


def _mix_kernel(hml_ref, hsw_ref, x_ref, woml_ref, wosw_ref, g1_ref, b1_ref,
                wq_ref, k_ref, v_ref, wo_ref, g2_ref, b2_ref, out_ref):
    d = x_ref.shape[2]
    dh = d // XA_HEADS
    h = jnp.dot(hml_ref[0], woml_ref[...], preferred_element_type=f32)
    h = h + jnp.dot(hsw_ref[0], wosw_ref[...], preferred_element_type=f32)
    x1 = _layer_norm(ALPHA * x_ref[0] + h, g1_ref[...], b1_ref[...])

    q = jnp.dot(x1.astype(bf16), wq_ref[...], preferred_element_type=f32).astype(bf16)
    outs = []
    for a in range(XA_HEADS):
        cs = slice(a * dh, (a + 1) * dh)
        lg = lax.dot_general(q[:, cs], k_ref[0, :, cs], (((1,), (1,)), ((), ())),
                             preferred_element_type=f32) * (dh ** -0.5)
        mx = jnp.max(lg, axis=-1, keepdims=True)
        p = jnp.exp(lg - mx)
        den = jnp.sum(p, axis=-1, keepdims=True)
        o = jnp.dot(p.astype(bf16), v_ref[0, :, cs], preferred_element_type=f32) / den
        outs.append(o.astype(bf16))
    o = jnp.concatenate(outs, axis=1)
    h2 = jnp.dot(o, wo_ref[...], preferred_element_type=f32)
    out_ref[0] = _layer_norm(ALPHA * x1 + h2, g2_ref[...], b2_ref[...])


def _mix(hml, hsw, x, woml, wosw, g1, b1, wq, k, v, wo, g2, b2):
    B, S, D = x.shape
    tm = min(TM_MIX, S)
    M = k.shape[1]
    tok = lambda w: pl.BlockSpec((1, tm, w), lambda b, s: (b, s, 0))
    memblk = pl.BlockSpec((1, M, D), lambda b, s: (b, 0, 0))
    consts = [woml, wosw, g1, b1, wq]
    return pl.pallas_call(
        _mix_kernel,
        grid=(B, S // tm),
        in_specs=[tok(hml.shape[2]), tok(hsw.shape[2]), tok(D)]
                 + [_const_spec(a.shape) for a in consts]
                 + [memblk, memblk, _const_spec(wo.shape), _const_spec(g2.shape), _const_spec(b2.shape)],
        out_specs=tok(D),
        out_shape=jax.ShapeDtypeStruct((B, S, D), f32),
        compiler_params=_params(2),
        name="mix",
    )(hml, hsw, x, woml, wosw, g1, b1, wq, k, v, wo, g2, b2)


def _gelu_tanh(x):
    c = math.sqrt(2.0 / math.pi)
    return x * (0.5 * (1.0 + jnp.tanh(c * (x + 0.044715 * (x * x * x)))))


def _ffn_kernel(x_ref, wup_ref, cw_ref, cb_ref, wdn_ref, g_ref, b_ref, out_ref,
                carry_ref, xb_ref, u00_ref, u01_ref, u10_ref, u11_ref, h_ref):
    tm = x_ref.shape[1]
    n_chunks = wup_ref.shape[0] // 2
    cwid = wup_ref.shape[2]
    halo = SUBLANES
    u_refs = ((u00_ref, u01_ref), (u10_ref, u11_ref))

    @pl.when(pl.program_id(1) == 0)
    def _():
        carry_ref[...] = jnp.zeros(carry_ref.shape, f32)

    xb_ref[...] = x_ref[0].astype(bf16)

    n_slabs = cwid // LANES

    def up(ci):
        for u_ref, idx in zip(u_refs[ci % 2], (ci, n_chunks + ci)):
            u = jnp.dot(xb_ref[...], wup_ref[idx], preferred_element_type=f32)
            for sl in range(n_slabs):
                lanes = slice(sl * LANES, (sl + 1) * LANES)
                u_ref[sl, 0:halo, :] = carry_ref[idx, :, lanes]
                u_ref[sl, halo:halo + tm, :] = u[:, lanes]
                carry_ref[idx, :, lanes] = u_ref[sl, tm:tm + halo, :]

    def act(ci):
        for r0 in range(0, tm, FF_ROWS):
            for sl in range(n_slabs):
                lanes = slice(sl * LANES, (sl + 1) * LANES)
                ys = []
                for u_ref, idx in zip(u_refs[ci % 2], (ci, n_chunks + ci)):
                    y = cb_ref[idx, :, lanes] + u_ref[sl, r0 + 6:r0 + 6 + FF_ROWS, :] * cw_ref[idx, 0:1, lanes]
                    y = y + u_ref[sl, r0 + 7:r0 + 7 + FF_ROWS, :] * cw_ref[idx, 1:2, lanes]
                    y = y + u_ref[sl, r0 + 8:r0 + 8 + FF_ROWS, :] * cw_ref[idx, 2:3, lanes]
                    ys.append(y)
                h_ref[r0:r0 + FF_ROWS, ci * cwid + sl * LANES:ci * cwid + (sl + 1) * LANES] = (
                    _gelu_tanh(ys[0]) * ys[1]).astype(bf16)

    up(0)
    for ci in range(n_chunks):
        if ci + 1 < n_chunks:
            up(ci + 1)
        act(ci)
    h = jnp.dot(h_ref[...], wdn_ref[...], preferred_element_type=f32)
    out_ref[0] = _layer_norm(ALPHA * x_ref[0] + h, g_ref[...], b_ref[...])


def _ffn(x, wup, cw, cb, wdn, g, b):
    B, S, D = x.shape
    tm = min(TM_FFN, S)
    n2, _, cwid = wup.shape
    tok = pl.BlockSpec((1, tm, D), lambda b_, s: (b_, s, 0))
    return pl.pallas_call(
        _ffn_kernel,
        grid=(B, S // tm),
        in_specs=[tok] + [_const_spec(a.shape) for a in (wup, cw, cb, wdn, g, b)],
        out_specs=tok,
        out_shape=jax.ShapeDtypeStruct((B, S, D), f32),
        scratch_shapes=[pltpu.VMEM((n2, SUBLANES, cwid), f32),
                        pltpu.VMEM((tm, D), bf16),
                        ]
                       + [pltpu.VMEM((cwid // LANES, tm + SUBLANES, LANES), f32)] * 4
                       + [pltpu.VMEM((tm, wdn.shape[0]), bf16)],
        compiler_params=_params(2),
        name="ffn",
    )(x, wup, cw, cb, wdn, g, b)


def _prep_inproj(w_in):
    w = w_in.astype(bf16)
    D = w.shape[0]
    o = 4 * ML_WIDTH
    gates = w[:, o:o + 2 * ML_HEADS]
    wvog = jnp.concatenate(
        [w[:, 2 * ML_WIDTH:o], gates, jnp.zeros((D, LANES - 2 * ML_HEADS), bf16)], axis=1)
    o += 2 * ML_HEADS
    swq = w[:, o:o + SWA_WIDTH]
    o += SWA_WIDTH
    dup = []
    for _ in range(2):
        for kv in range(SWA_KV_HEADS):
            head = w[:, o + kv * SWA_HEAD_DIM:o + (kv + 1) * SWA_HEAD_DIM]
            dup += [head, head]
        o += SWA_KV_HEADS * SWA_HEAD_DIM
    return w[:, :2 * ML_WIDTH], wvog, jnp.concatenate([swq] + dup, axis=1)


def _chunk_cols(a, cwid):
    lead = a.shape[:-1]
    n = a.shape[-1] // cwid
    return jnp.moveaxis(a.reshape(lead + (n, cwid)), -2, 0)


def kernel(x, mem, rel_bias, w_in, ml_conv_w, ml_conv_b, ml_i_bias, ml_f_bias, ml_norm_g, swa_sinks, w_out, ln1_g, ln1_b, xa_wq, xa_wkv, xa_wo, ln2_g, ln2_b, ffn_w_up, ffn_conv_w, ffn_conv_b, ffn_w_down, ln3_g, ln3_b):
    depth = w_in.shape[0]
    row = lambda a: a.reshape(1, -1)
    bias = _swa_bias(rel_bias)
    for l in range(depth):
        wqk, wvog, wsw = _prep_inproj(w_in[l])
        qk, vo, gates, sw = _inproj(x, wqk, wvog, wsw, ml_conv_w[l], row(ml_conv_b[l]))
        gbias = jnp.concatenate([ml_i_bias[l], ml_f_bias[l]]).reshape(2 * ML_HEADS, 1)
        hml = _mlstm(qk, vo, gates, gbias, row(ml_norm_g[l]))
        hsw = _swa(sw, bias, swa_sinks[l])
        wo = w_out[l].astype(bf16)
        k, v = _memkv(mem, xa_wkv[l].astype(bf16))
        x = _mix(hml, hsw, x, wo[:ML_WIDTH], wo[ML_WIDTH:], row(ln1_g[l]), row(ln1_b[l]),
                 xa_wq[l].astype(bf16), k, v, xa_wo[l].astype(bf16), row(ln2_g[l]), row(ln2_b[l]))
        wup = _chunk_cols(ffn_w_up[l].astype(bf16), FF_CHUNK)
        cw = _chunk_cols(ffn_conv_w[l], FF_CHUNK)
        cb = _chunk_cols(row(ffn_conv_b[l]), FF_CHUNK)
        wdn = ffn_w_down[l].astype(bf16)
        x = _ffn(x, wup, cw, cb, wdn, row(ln3_g[l]), row(ln3_b[l]))
    return x
```

```python
import functools
import math

import numpy as np
import jax
import jax.numpy as jnp
from jax import lax
from jax.experimental import pallas as pl
from jax.experimental.pallas import tpu as pltpu

f32 = jnp.float32
bf16 = jnp.bfloat16

ML_HEADS = 4
ML_HEAD_DIM = 128
ML_WIDTH = ML_HEADS * ML_HEAD_DIM
ML_CONV = 4
SWA_HEADS = 8
SWA_KV_HEADS = 2
SWA_GROUP = SWA_HEADS // SWA_KV_HEADS
SWA_HEAD_DIM = 64
SWA_WIDTH = SWA_HEADS * SWA_HEAD_DIM
WINDOW = 128
BLOCK = 128
REL_BUCKETS = 32
REL_MAX_DIST = 128
XA_HEADS = 4
FFN_CONV = 3
DEPTH = 2
ALPHA = (2.0 * DEPTH) ** 0.25
EPS = 1e-5

LANES = 128
SUBLANES = 8
TM_PROJ = 512
PROJ_ROWS = 128
ML_CHUNK = 256
TQ_SWA = 512
TM_MIX = 512
MIX_STREAMS = 2
TM_FFN = 512
FF_CHUNK = 256
FF_ROWS = 64
VMEM_LIMIT = 56 * 1024 * 1024
NEG_BIG = -1e30


def _params(n_axes, flags=None):
    return pltpu.CompilerParams(
        dimension_semantics=("arbitrary",) * n_axes, vmem_limit_bytes=VMEM_LIMIT, flags=flags)


def _const_spec(shape):
    nd = len(shape)
    return pl.BlockSpec(shape, lambda *_: (0,) * nd, pipeline_mode=pl.Buffered(1))


def _layer_norm(z, g, b):
    mu = jnp.mean(z, axis=-1, keepdims=True)
    zc = z - mu
    var = jnp.mean(zc * zc, axis=-1, keepdims=True)
    return zc * lax.rsqrt(var + EPS) * g + b


def _inproj_kernel(x_ref, wqk_ref, wvog_ref, wsw_ref, cw_ref, cb_ref,
                   qk_ref, vo_ref, g_ref, sw_ref, xb_ref, p_ref):
    tm = x_ref.shape[1]
    n_slabs = p_ref.shape[0]
    halo = SUBLANES

    @pl.when(pl.program_id(1) == 0)
    def _():
        p_ref[:, 0:halo, :] = jnp.zeros((n_slabs, halo, LANES), f32)

    xb_ref[...] = x_ref[0].astype(bf16)
    acc = jnp.dot(xb_ref[...], wqk_ref[...], preferred_element_type=f32)
    for sl in range(n_slabs):
        p_ref[sl, halo:halo + tm, :] = acc[:, sl * LANES:(sl + 1) * LANES]

    accv = jnp.dot(xb_ref[...], wvog_ref[...], preferred_element_type=f32)
    vo_w = vo_ref.shape[2]
    vo_ref[0] = accv[:, :vo_w].astype(bf16)
    g_ref[0] = accv[:, vo_w:].T[0:SUBLANES, :]
    sw_ref[0] = jnp.dot(xb_ref[...], wsw_ref[...], preferred_element_type=f32).astype(bf16)

    for sl in range(n_slabs):
        lanes = slice(sl * LANES, (sl + 1) * LANES)
        for r0 in range(0, tm, PROJ_ROWS):
            y = cb_ref[:, lanes] + p_ref[sl, r0 + 5:r0 + 5 + PROJ_ROWS, :] * cw_ref[0:1, lanes]
            y = y + p_ref[sl, r0 + 6:r0 + 6 + PROJ_ROWS, :] * cw_ref[1:2, lanes]
            y = y + p_ref[sl, r0 + 7:r0 + 7 + PROJ_ROWS, :] * cw_ref[2:3, lanes]
            y = y + p_ref[sl, r0 + 8:r0 + 8 + PROJ_ROWS, :] * cw_ref[3:4, lanes]
            qk_ref[0, r0:r0 + PROJ_ROWS, lanes] = (y * jax.nn.sigmoid(y)).astype(bf16)
        p_ref[sl, 0:halo, :] = p_ref[sl, tm:tm + halo, :]


def _inproj(x, wqk, wvog, wsw, cw, cb):
    B, S, D = x.shape
    tm = min(TM_PROJ, S)
    grid = (B, S // tm)
    tok = lambda w: pl.BlockSpec((1, tm, w), lambda b, s: (b, s, 0))
    return pl.pallas_call(
        _inproj_kernel,
        grid=grid,
        in_specs=[tok(D), _const_spec(wqk.shape), _const_spec(wvog.shape), _const_spec(wsw.shape),
                  _const_spec(cw.shape), _const_spec(cb.shape)],
        out_specs=[tok(wqk.shape[1]), tok(2 * ML_WIDTH),
                   pl.BlockSpec((1, SUBLANES, tm), lambda b, s: (b, 0, s)),
                   tok(wsw.shape[1])],
        out_shape=[jax.ShapeDtypeStruct((B, S, wqk.shape[1]), bf16),
                   jax.ShapeDtypeStruct((B, S, 2 * ML_WIDTH), bf16),
                   jax.ShapeDtypeStruct((B, SUBLANES, S), f32),
                   jax.ShapeDtypeStruct((B, S, wsw.shape[1]), bf16)],
        scratch_shapes=[pltpu.VMEM((tm, D), bf16),
                        pltpu.VMEM((wqk.shape[1] // LANES, tm + SUBLANES, LANES), f32)],
        compiler_params=_params(2),
        name="inproj",
    )(x, wqk, wvog, wsw, cw, cb)


C_LANE, CM_LANE, B_LANE = 0, SUBLANES, 2 * SUBLANES


def _scan_chunks(x, op, fill, chunk):
    pos = lax.broadcasted_iota(jnp.int32, x.shape, 1) % chunk
    sh = 1
    while sh < chunk:
        x = op(x, jnp.where(pos >= sh, pltpu.roll(x, sh, 1), fill))
        sh *= 2
    return x


def _mlgate_kernel(g_ref, gb_ref, rows_ref, cols_ref, *, chunk):
    g = g_ref[0] + gb_ref[...]
    ig = jnp.concatenate([g[0:4], g[0:4]], axis=0)
    fg = jnp.concatenate([g[4:8], g[4:8]], axis=0)
    lf = jnp.minimum(fg, 0.0) - jnp.log1p(jnp.exp(-jnp.abs(fg)))
    b = _scan_chunks(lf, jnp.add, 0.0, chunk)
    c = ig - b
    cm = _scan_chunks(c, jnp.maximum, -jnp.inf, chunk)
    rows_ref[0] = c
    stack = jnp.concatenate(
        [c, cm, b, jnp.zeros((LANES - 3 * SUBLANES, c.shape[1]), f32)], axis=0)
    cols_ref[0] = stack.T


def _mlgate(gates, gbias, chunk):
    B, _, S = gates.shape
    return pl.pallas_call(
        functools.partial(_mlgate_kernel, chunk=chunk),
        grid=(B,),
        in_specs=[pl.BlockSpec((1, SUBLANES, S), lambda b: (b, 0, 0)), _const_spec(gbias.shape)],
        out_specs=[pl.BlockSpec((1, SUBLANES, S), lambda b: (b, 0, 0)),
                   pl.BlockSpec((1, S, LANES), lambda b: (b, 0, 0))],
        out_shape=[jax.ShapeDtypeStruct((B, SUBLANES, S), f32),
                   jax.ShapeDtypeStruct((B, S, LANES), f32)],
        compiler_params=_params(1),
        name="mlgate",
    )(gates, gbias)


def _mlstm_kernel(qk_ref, vo_ref, rows_ref, cols_ref, ng_ref, out_ref, c_ref, m_ref):
    L = qk_ref.shape[1]
    dh = ML_HEAD_DIM
    scale = dh ** -0.5

    @pl.when(pl.program_id(1) == 0)
    def _():
        c_ref[...] = jnp.zeros(c_ref.shape, f32)
        m_ref[...] = jnp.zeros(m_ref.shape, f32)

    row = lax.broadcasted_iota(jnp.int32, (L, L), 0)
    col = lax.broadcasted_iota(jnp.int32, (L, L), 1)
    causal = row >= col
    ones_blk = jnp.ones((L, dh), bf16)

    heads = range(ML_HEADS)
    hs = lambda h: slice(h * dh, (h + 1) * dh)
    ks = lambda h: slice(ML_WIDTH + h * dh, ML_WIDTH + (h + 1) * dh)
    m_all = m_ref[...]

    c_aug = [c_ref[h] for h in heads]
    v_aug = [jnp.concatenate([vo_ref[0, :, hs(h)], ones_blk], axis=1) for h in heads]
    qk_t = [lax.dot_general(qk_ref[0, :, hs(h)], qk_ref[0, :, ks(h)], (((1,), (1,)), ((), ())),
                            preferred_element_type=f32) for h in heads]
    q_c = [jnp.dot(qk_ref[0, :, hs(h)], c_aug[h].astype(bf16), preferred_element_type=f32)
           for h in heads]

    c_row, cm_b, big_m, m_prev, s_mat = [], [], [], [], []
    for h in heads:
        c_row.append(rows_ref[0, h:h + 1, :])
        cm_b.append(jnp.broadcast_to(cols_ref[0, :, CM_LANE + h:CM_LANE + h + 1], (L, dh)))
        m_prev.append(m_all[h:h + 1, 0:1])
        big_m.append(jnp.maximum(m_prev[h], cm_b[h]))
        cm_wide = jnp.concatenate([cm_b[h]] * (L // dh), axis=1)
        p = jnp.where(causal, jnp.exp(c_row[h] - cm_wide), 0.0)
        s_mat.append((qk_t[h] * scale * p).astype(bf16))
    s_v = [jnp.dot(s_mat[h], v_aug[h], preferred_element_type=f32) for h in heads]

    m_last, kw_t = [], []
    for h in heads:
        m_last.append(big_m[h][L - 1:L, 0:1])
        wk = jnp.exp(c_row[h] - m_last[h]) * scale
        kw_t.append((qk_ref[0, :, ks(h)].astype(f32).T * wk).astype(bf16))
    k_v = [jnp.dot(kw_t[h], v_aug[h], preferred_element_type=f32) for h in heads]

    m_new = []
    for h in heads:
        b_col = cols_ref[0, :, B_LANE + h:B_LANE + h + 1]
        b_b = jnp.broadcast_to(b_col, (L, dh))
        w_intra = jnp.exp(cm_b[h] - big_m[h])
        w_inter = jnp.exp(m_prev[h] - big_m[h])
        num = w_inter * q_c[h][:, :dh] + w_intra * s_v[h][:, :dh]
        den = w_inter * q_c[h][:, dh:] + w_intra * s_v[h][:, dh:]
        hh = num / jnp.maximum(jnp.abs(den), jnp.exp(-(b_b + big_m[h])))
        mu = jnp.mean(hh, axis=-1, keepdims=True)
        hc = hh - mu
        var = jnp.mean(hc * hc, axis=-1, keepdims=True)
        hn = hc * lax.rsqrt(var + EPS) * ng_ref[:, hs(h)]
        o = vo_ref[0, :, ks(h)]
        out_ref[0, :, hs(h)] = (jax.nn.sigmoid(o.astype(f32)) * hn).astype(bf16)
        c_ref[h] = jnp.exp(m_prev[h] - m_last[h]) * c_aug[h] + k_v[h]
        m_new.append(jnp.broadcast_to(b_col[L - 1:L, :] + m_last[h], (1, LANES)))
    m_ref[0:ML_HEADS, :] = jnp.concatenate(m_new, axis=0)


def _mlstm(qk, vo, gates, gbias, norm_g):
    B, S, _ = qk.shape
    L = min(ML_CHUNK, S)
    rows, cols = _mlgate(gates, gbias, L)
    grid = (B, S // L)
    return pl.pallas_call(
        _mlstm_kernel,
        grid=grid,
        in_specs=[pl.BlockSpec((1, L, 2 * ML_WIDTH), lambda b, s: (b, s, 0)),
                  pl.BlockSpec((1, L, 2 * ML_WIDTH), lambda b, s: (b, s, 0)),
                  pl.BlockSpec((1, SUBLANES, L), lambda b, s: (b, 0, s)),
                  pl.BlockSpec((1, L, LANES), lambda b, s: (b, s, 0)),
                  _const_spec(norm_g.shape)],
        out_specs=pl.BlockSpec((1, L, ML_WIDTH), lambda b, s: (b, s, 0)),
        out_shape=jax.ShapeDtypeStruct((B, S, ML_WIDTH), bf16),
        scratch_shapes=[pltpu.VMEM((ML_HEADS, ML_HEAD_DIM, 2 * ML_HEAD_DIM), f32),
                        pltpu.VMEM((SUBLANES, LANES), f32)],
        compiler_params=_params(2),
        name="mlstm",
    )(qk, vo, rows, cols, norm_g)


def _t5_bucket_table():
    r = np.arange(BLOCK)[:, None]
    c = np.arange(2 * BLOCK)[None, :]
    dist = BLOCK + r - c
    n = np.maximum(dist, 0)
    max_exact = REL_BUCKETS // 2
    nf = np.maximum(n, 1).astype(np.float32)
    large = max_exact + (np.log(nf / np.float32(max_exact)) / np.float32(math.log(REL_MAX_DIST / max_exact))
                         * np.float32(REL_BUCKETS - max_exact)).astype(np.int32)
    large = np.minimum(large, REL_BUCKETS - 1)
    bucket = np.where(n < max_exact, n, large)
    valid = (dist >= 0) & (dist < WINDOW)
    return np.where(valid, bucket, -1).astype(np.int32)


def _bias_kernel(bucket_ref, rb_ref, out_ref):
    bucket = bucket_ref[...]
    for h in range(SWA_HEADS):
        acc = jnp.full(bucket.shape, NEG_BIG, f32)
        for i in range(REL_BUCKETS):
            acc = jnp.where(bucket == i, rb_ref[i, h], acc)
        out_ref[h] = acc


def _swa_bias(rel_bias):
    bucket = jnp.asarray(_t5_bucket_table())
    return pl.pallas_call(
        _bias_kernel,
        in_specs=[pl.BlockSpec(memory_space=pltpu.VMEM), pl.BlockSpec(memory_space=pltpu.SMEM)],
        out_specs=pl.BlockSpec(memory_space=pltpu.VMEM),
        out_shape=jax.ShapeDtypeStruct((SWA_HEADS, BLOCK, 2 * BLOCK), f32),
        name="swa_bias",
    )(bucket, rel_bias)


def _swa_kernel(sw_ref, prev_ref, bias_ref, sink_ref, out_ref):
    tq = sw_ref.shape[1]
    scale = SWA_HEAD_DIM ** -0.5
    pair = 2 * SWA_HEAD_DIM
    k_off = SWA_WIDTH
    v_off = SWA_WIDTH + SWA_KV_HEADS * pair
    first_tile = pl.program_id(1) == 0
    lo = lax.broadcasted_iota(jnp.int32, (BLOCK, pair), 1) < SWA_HEAD_DIM
    left = lax.broadcasted_iota(jnp.int32, (BLOCK, 2 * BLOCK), 1) < BLOCK

    for j in range(tq // BLOCK):
        rows = slice(j * BLOCK, (j + 1) * BLOCK)
        prows = slice((j - 1) * BLOCK, j * BLOCK)
        for kv in range(SWA_KV_HEADS):
            kc = slice(k_off + kv * pair, k_off + (kv + 1) * pair)
            vc = slice(v_off + kv * pair, v_off + (kv + 1) * pair)
            if j == 0:
                k_prev = prev_ref[0, :, kv * pair:(kv + 1) * pair]
                v_prev = prev_ref[0, :, (SWA_KV_HEADS + kv) * pair:(SWA_KV_HEADS + kv + 1) * pair]
            else:
                k_prev = sw_ref[0, prows, kc]
                v_prev = sw_ref[0, prows, vc]
            kband = jnp.concatenate([k_prev, sw_ref[0, rows, kc]], axis=0)
            vband = jnp.concatenate([v_prev, sw_ref[0, rows, vc]], axis=0)
            q0 = sw_ref[0, rows, kv * 2 * pair:kv * 2 * pair + pair]
            q1 = sw_ref[0, rows, kv * 2 * pair + pair:(kv + 1) * 2 * pair]
            zero = jnp.zeros_like(q0)
            qs = jnp.concatenate([jnp.where(lo, q0, zero), jnp.where(lo, zero, q0),
                                  jnp.where(lo, q1, zero), jnp.where(lo, zero, q1)], axis=0)
            logits = lax.dot_general(qs, kband, (((1,), (1,)), ((), ())),
                                     preferred_element_type=f32)
            ps, dens = [], []
            for gq in range(SWA_GROUP):
                head = kv * SWA_GROUP + gq
                lg = logits[gq * BLOCK:(gq + 1) * BLOCK] * scale + bias_ref[head]
                if j == 0:
                    lg = jnp.where(jnp.logical_and(first_tile, left), NEG_BIG, lg)
                sink = sink_ref[head]
                mx = jnp.maximum(jnp.max(lg, axis=-1, keepdims=True), sink)
                p = jnp.exp(lg - mx)
                dens.append(jnp.sum(p, axis=-1, keepdims=True) + jnp.exp(sink - mx))
                ps.append(p.astype(bf16))
            pv = jnp.dot(jnp.concatenate(ps, axis=0), vband, preferred_element_type=f32)
            o = [pv[gq * BLOCK:(gq + 1) * BLOCK] / dens[gq] for gq in range(SWA_GROUP)]
            out_ref[0, rows, kv * 2 * pair:kv * 2 * pair + pair] = jnp.where(lo, o[0], o[1]).astype(bf16)
            out_ref[0, rows, kv * 2 * pair + pair:(kv + 1) * 2 * pair] = jnp.where(lo, o[2], o[3]).astype(bf16)


def _swa(sw, bias, sinks):
    B, S, W = sw.shape
    tq = min(TQ_SWA, S)
    nblk = tq // BLOCK
    kvw = W - SWA_WIDTH
    assert kvw == SWA_WIDTH, "k/v column group must be as wide as the q group"
    return pl.pallas_call(
        _swa_kernel,
        grid=(B, S // tq),
        in_specs=[pl.BlockSpec((1, tq, W), lambda b, s: (b, s, 0)),
                  pl.BlockSpec((1, BLOCK, kvw), lambda b, s: (b, jnp.maximum(s * nblk - 1, 0), 1)),
                  _const_spec(bias.shape),
                  pl.BlockSpec(memory_space=pltpu.SMEM)],
        out_specs=pl.BlockSpec((1, tq, SWA_WIDTH), lambda b, s: (b, s, 0)),
        out_shape=jax.ShapeDtypeStruct((B, S, SWA_WIDTH), bf16),
        compiler_params=_params(2),
        name="swa",
    )(sw, sw, bias, sinks)


def _memkv_kernel(mem_ref, wkv_ref, k_ref, v_ref):
    d = k_ref.shape[2]
    kv = jnp.dot(mem_ref[0].astype(bf16), wkv_ref[...], preferred_element_type=f32)
    k_ref[0] = kv[:, :d].astype(bf16)
    v_ref[0] = kv[:, d:].astype(bf16)


def _memkv(mem, wkv):
    B, M, D = mem.shape
    blk = pl.BlockSpec((1, M, D), lambda b: (b, 0, 0))
    return pl.pallas_call(
        _memkv_kernel,
        grid=(B,),
        in_specs=[blk, _const_spec(wkv.shape)],
        out_specs=[blk, blk],
        out_shape=[jax.ShapeDtypeStruct((B, M, D), bf16)] * 2,
        compiler_params=_params(1),
        name="memkv",
    )(mem, wkv)


def _mix_kernel(hml_ref, hsw_ref, x_ref, woml_ref, wosw_ref, g1_ref, b1_ref,
                wq_ref, k_ref, v_ref, wo_ref, g2_ref, b2_ref, out_ref):
    tm, d = x_ref.shape[1], x_ref.shape[2]
    dh = d // XA_HEADS
    rows = [slice(i * tm // MIX_STREAMS, (i + 1) * tm // MIX_STREAMS) for i in range(MIX_STREAMS)]
    streams = range(MIX_STREAMS)
    cs = lambda a: slice(a * dh, (a + 1) * dh)

    h = [jnp.dot(hml_ref[0, r, :], woml_ref[...], preferred_element_type=f32)
         + jnp.dot(hsw_ref[0, r, :], wosw_ref[...], preferred_element_type=f32) for r in rows]
    x1, q = [], []
    for i in streams:
        x1.append(_layer_norm(ALPHA * x_ref[0, rows[i], :] + h[i], g1_ref[...], b1_ref[...]))
        q.append(jnp.dot(x1[i].astype(bf16), wq_ref[...], preferred_element_type=f32).astype(bf16))

    items = [(a, i) for a in range(XA_HEADS) for i in streams]
    lag = 2
    prob, den, outs = {}, {}, {}

    def logits(a, i):
        lg = lax.dot_general(q[i][:, cs(a)], k_ref[0, :, cs(a)], (((1,), (1,)), ((), ())),
                             preferred_element_type=f32) * (dh ** -0.5)
        p = jnp.exp(lg - jnp.max(lg, axis=-1, keepdims=True))
        den[a, i] = jnp.sum(p, axis=-1, keepdims=True)
        prob[a, i] = p.astype(bf16)

    def values(a, i):
        o = jnp.dot(prob[a, i], v_ref[0, :, cs(a)], preferred_element_type=f32) / den[a, i]
        outs[a, i] = o.astype(bf16)

    for n, item in enumerate(items):
        logits(*item)
        if n >= lag:
            values(*items[n - lag])
    for item in items[-lag:]:
        values(*item)

    h2 = [jnp.dot(jnp.concatenate([outs[a, i] for a in range(XA_HEADS)], axis=1), wo_ref[...],
                  preferred_element_type=f32) for i in streams]
    for i in streams:
        out_ref[0, rows[i], :] = _layer_norm(ALPHA * x1[i] + h2[i], g2_ref[...], b2_ref[...])


def _mix(hml, hsw, x, woml, wosw, g1, b1, wq, k, v, wo, g2, b2):
    B, S, D = x.shape
    tm = min(TM_MIX, S)
    M = k.shape[1]
    tok = lambda w: pl.BlockSpec((1, tm, w), lambda b, s: (b, s, 0))
    memblk = pl.BlockSpec((1, M, D), lambda b, s: (b, 0, 0))
    consts = [woml, wosw, g1, b1, wq]
    return pl.pallas_call(
        _mix_kernel,
        grid=(B, S // tm),
        in_specs=[tok(hml.shape[2]), tok(hsw.shape[2]), tok(D)]
                 + [_const_spec(a.shape) for a in consts]
                 + [memblk, memblk, _const_spec(wo.shape), _const_spec(g2.shape), _const_spec(b2.shape)],
        out_specs=tok(D),
        out_shape=jax.ShapeDtypeStruct((B, S, D), f32),
        compiler_params=_params(2),
        name="mix",
    )(hml, hsw, x, woml, wosw, g1, b1, wq, k, v, wo, g2, b2)


GELU_C0 = math.sqrt(2.0 / math.pi)
GELU_C1 = 0.044715 * GELU_C0


def _ffn_kernel(x_ref, wup_ref, cw_ref, cb_ref, wdn_ref, g_ref, b_ref, out_ref,
                carry_ref, xb_ref, u00_ref, u01_ref, u10_ref, u11_ref, h_ref):
    tm = x_ref.shape[1]
    d_ff = wdn_ref.shape[0]
    cwid = FF_CHUNK
    n_chunks = d_ff // cwid
    n_slabs = cwid // LANES
    halo = SUBLANES
    u_refs = ((u00_ref, u01_ref), (u10_ref, u11_ref))

    @pl.when(pl.program_id(1) == 0)
    def _():
        carry_ref[...] = jnp.zeros(carry_ref.shape, f32)

    xb_ref[...] = x_ref[0].astype(bf16)

    def up(ci):
        for u_ref, c0 in zip(u_refs[ci % 2], (ci * cwid, d_ff + ci * cwid)):
            u = jnp.dot(xb_ref[...], wup_ref[:, c0:c0 + cwid], preferred_element_type=f32)
            for sl in range(n_slabs):
                cols = slice(c0 + sl * LANES, c0 + (sl + 1) * LANES)
                u_ref[sl, 0:halo, :] = carry_ref[:, cols]
                u_ref[sl, halo:halo + tm, :] = u[:, sl * LANES:(sl + 1) * LANES]
                carry_ref[:, cols] = u_ref[sl, tm:tm + halo, :]

    def act(ci):
        gu_ref, vu_ref = u_refs[ci % 2]
        for sl in range(n_slabs):
            gc = slice(ci * cwid + sl * LANES, ci * cwid + (sl + 1) * LANES)
            vc = slice(d_ff + gc.start, d_ff + gc.stop)
            gb, gw = cb_ref[:, gc], [cw_ref[j:j + 1, gc] for j in range(FFN_CONV)]
            vb, vw = 0.5 * cb_ref[:, vc], [0.5 * cw_ref[j:j + 1, vc] for j in range(FFN_CONV)]
            for r0 in range(0, tm, FF_ROWS):
                g, v = gb, vb
                for j in range(FFN_CONV):
                    rows = slice(r0 + halo - (FFN_CONV - 1) + j, r0 + halo - (FFN_CONV - 1) + j + FF_ROWS)
                    g = g + gu_ref[sl, rows, :] * gw[j]
                    v = v + vu_ref[sl, rows, :] * vw[j]
                t = jnp.tanh(g * (GELU_C0 + GELU_C1 * (g * g)))
                h_ref[r0:r0 + FF_ROWS, gc] = ((g * v) * (1.0 + t)).astype(bf16)

    up(0)
    for ci in range(n_chunks - 1):
        up(ci + 1)
        act(ci)
    top, bot = slice(0, tm // 2), slice(tm // 2, tm)
    k1 = (n_chunks - 1) * cwid
    h_top = jnp.dot(h_ref[top, :k1], wdn_ref[:k1, :], preferred_element_type=f32)
    act(n_chunks - 1)
    h_top = h_top + jnp.dot(h_ref[top, k1:], wdn_ref[k1:, :], preferred_element_type=f32)
    h_bot = jnp.dot(h_ref[bot, :], wdn_ref[...], preferred_element_type=f32)
    out_ref[0, top, :] = _layer_norm(ALPHA * x_ref[0, top, :] + h_top, g_ref[...], b_ref[...])
    out_ref[0, bot, :] = _layer_norm(ALPHA * x_ref[0, bot, :] + h_bot, g_ref[...], b_ref[...])


def _ffn(x, wup, cw, cb, wdn, g, b):
    B, S, D = x.shape
    tm = min(TM_FFN, S)
    cwid = FF_CHUNK
    assert wdn.shape[0] % cwid == 0 and wup.shape[1] == 2 * wdn.shape[0]
    tok = pl.BlockSpec((1, tm, D), lambda b_, s: (b_, s, 0))
    return pl.pallas_call(
        _ffn_kernel,
        grid=(B, S // tm),
        in_specs=[tok] + [_const_spec(a.shape) for a in (wup, cw, cb, wdn, g, b)],
        out_specs=tok,
        out_shape=jax.ShapeDtypeStruct((B, S, D), f32),
        scratch_shapes=[pltpu.VMEM((SUBLANES, wup.shape[1]), f32),
                        pltpu.VMEM((tm, D), bf16),
                        ]
                       + [pltpu.VMEM((cwid // LANES, tm + SUBLANES, LANES), f32)] * 4
                       + [pltpu.VMEM((tm, wdn.shape[0]), bf16)],
        compiler_params=_params(2),
        name="ffn",
    )(x, wup, cw, cb, wdn, g, b)


def _prep_inproj(w_in):
    w = w_in.astype(bf16)
    D = w.shape[0]
    o = 4 * ML_WIDTH
    gates = w[:, o:o + 2 * ML_HEADS]
    wvog = jnp.concatenate(
        [w[:, 2 * ML_WIDTH:o], gates, jnp.zeros((D, LANES - 2 * ML_HEADS), bf16)], axis=1)
    o += 2 * ML_HEADS
    swq = w[:, o:o + SWA_WIDTH]
    o += SWA_WIDTH
    dup = []
    for _ in range(2):
        for kv in range(SWA_KV_HEADS):
            head = w[:, o + kv * SWA_HEAD_DIM:o + (kv + 1) * SWA_HEAD_DIM]
            dup += [head, head]
        o += SWA_KV_HEADS * SWA_HEAD_DIM
    return w[:, :2 * ML_WIDTH], wvog, jnp.concatenate([swq] + dup, axis=1)


def kernel(x, mem, rel_bias, w_in, ml_conv_w, ml_conv_b, ml_i_bias, ml_f_bias, ml_norm_g, swa_sinks, w_out, ln1_g, ln1_b, xa_wq, xa_wkv, xa_wo, ln2_g, ln2_b, ffn_w_up, ffn_conv_w, ffn_conv_b, ffn_w_down, ln3_g, ln3_b):
    depth = w_in.shape[0]
    row = lambda a: a.reshape(1, -1)
    bias = _swa_bias(rel_bias)
    for l in range(depth):
        wqk, wvog, wsw = _prep_inproj(w_in[l])
        qk, vo, gates, sw = _inproj(x, wqk, wvog, wsw, ml_conv_w[l], row(ml_conv_b[l]))
        gbias = jnp.concatenate([ml_i_bias[l], ml_f_bias[l]]).reshape(2 * ML_HEADS, 1)
        hml = _mlstm(qk, vo, gates, gbias, row(ml_norm_g[l]))
        hsw = _swa(sw, bias, swa_sinks[l])
        wo = w_out[l].astype(bf16)
        k, v = _memkv(mem, xa_wkv[l].astype(bf16))
        x = _mix(hml, hsw, x, wo[:ML_WIDTH], wo[ML_WIDTH:], row(ln1_g[l]), row(ln1_b[l]),
                 xa_wq[l].astype(bf16), k, v, xa_wo[l].astype(bf16), row(ln2_g[l]), row(ln2_b[l]))
        x = _ffn(x, ffn_w_up[l].astype(bf16), ffn_conv_w[l], row(ffn_conv_b[l]),
                 ffn_w_down[l].astype(bf16), row(ln3_g[l]), row(ln3_b[l]))
    return x
```

```python
import functools
import math

import numpy as np
import jax
import jax.numpy as jnp
from jax import lax
from jax.experimental import pallas as pl
from jax.experimental.pallas import tpu as pltpu

f32 = jnp.float32
bf16 = jnp.bfloat16

ML_HEADS = 4
ML_HEAD_DIM = 128
ML_WIDTH = ML_HEADS * ML_HEAD_DIM
ML_CONV = 4
SWA_HEADS = 8
SWA_KV_HEADS = 2
SWA_GROUP = SWA_HEADS // SWA_KV_HEADS
SWA_HEAD_DIM = 64
SWA_WIDTH = SWA_HEADS * SWA_HEAD_DIM
WINDOW = 128
BLOCK = 128
REL_BUCKETS = 32
REL_MAX_DIST = 128
XA_HEADS = 4
FFN_CONV = 3
DEPTH = 2
ALPHA = (2.0 * DEPTH) ** 0.25
EPS = 1e-5

LANES = 128
SUBLANES = 8
TM_PROJ = 512
PROJ_ROWS = 128
ML_CHUNK = 512
TQ_SWA = 512
TM_MIX = 512
MIX_STREAMS = 2
TM_FFN = 512
FF_CHUNK = 256
FF_ROWS = 64
VMEM_LIMIT = 56 * 1024 * 1024
NEG_BIG = -1e30


def _params(n_axes, flags=None):
    return pltpu.CompilerParams(
        dimension_semantics=("arbitrary",) * n_axes, vmem_limit_bytes=VMEM_LIMIT, flags=flags)


def _const_spec(shape):
    nd = len(shape)
    return pl.BlockSpec(shape, lambda *_: (0,) * nd, pipeline_mode=pl.Buffered(1))


def _layer_norm(z, g, b):
    mu = jnp.mean(z, axis=-1, keepdims=True)
    zc = z - mu
    var = jnp.mean(zc * zc, axis=-1, keepdims=True)
    return zc * lax.rsqrt(var + EPS) * g + b


def _inproj_kernel(x_ref, wqk_ref, wvo_ref, wswg_ref, cw_ref, cb_ref,
                   qk_ref, vo_ref, g_ref, sw_ref, xb_ref, p_ref):
    tm = x_ref.shape[1]
    n_slabs = p_ref.shape[0]
    halo = SUBLANES

    @pl.when(pl.program_id(1) == 0)
    def _():
        p_ref[:, 0:halo, :] = jnp.zeros((n_slabs, halo, LANES), f32)

    xb_ref[...] = x_ref[0].astype(bf16)
    acc = jnp.dot(xb_ref[...], wqk_ref[...], preferred_element_type=f32)
    for sl in range(n_slabs):
        p_ref[sl, halo:halo + tm, :] = acc[:, sl * LANES:(sl + 1) * LANES]

    vo_ref[0] = jnp.dot(xb_ref[...], wvo_ref[...], preferred_element_type=f32).astype(bf16)
    accs = jnp.dot(xb_ref[...], wswg_ref[...], preferred_element_type=f32)
    sw_w = sw_ref.shape[2]
    sw_ref[0] = accs[:, :sw_w].astype(bf16)
    g_ref[0] = accs[:, sw_w:sw_w + LANES].T[0:SUBLANES, :]

    for sl in range(n_slabs):
        lanes = slice(sl * LANES, (sl + 1) * LANES)
        for r0 in range(0, tm, PROJ_ROWS):
            y = cb_ref[:, lanes] + p_ref[sl, r0 + 5:r0 + 5 + PROJ_ROWS, :] * cw_ref[0:1, lanes]
            y = y + p_ref[sl, r0 + 6:r0 + 6 + PROJ_ROWS, :] * cw_ref[1:2, lanes]
            y = y + p_ref[sl, r0 + 7:r0 + 7 + PROJ_ROWS, :] * cw_ref[2:3, lanes]
            y = y + p_ref[sl, r0 + 8:r0 + 8 + PROJ_ROWS, :] * cw_ref[3:4, lanes]
            qk_ref[0, r0:r0 + PROJ_ROWS, lanes] = (y * jax.nn.sigmoid(y)).astype(bf16)
        p_ref[sl, 0:halo, :] = p_ref[sl, tm:tm + halo, :]


def _inproj(x, wqk, wvo, wswg, cw, cb):
    B, S, D = x.shape
    tm = min(TM_PROJ, S)
    grid = (B, S // tm)
    sw_w = SWA_WIDTH + 4 * SWA_HEAD_DIM
    tok = lambda w: pl.BlockSpec((1, tm, w), lambda b, s: (b, s, 0))
    return pl.pallas_call(
        _inproj_kernel,
        grid=grid,
        in_specs=[tok(D), _const_spec(wqk.shape), _const_spec(wvo.shape), _const_spec(wswg.shape),
                  _const_spec(cw.shape), _const_spec(cb.shape)],
        out_specs=[tok(wqk.shape[1]), tok(wvo.shape[1]),
                   pl.BlockSpec((1, SUBLANES, tm), lambda b, s: (b, 0, s)),
                   tok(sw_w)],
        out_shape=[jax.ShapeDtypeStruct((B, S, wqk.shape[1]), bf16),
                   jax.ShapeDtypeStruct((B, S, wvo.shape[1]), bf16),
                   jax.ShapeDtypeStruct((B, SUBLANES, S), f32),
                   jax.ShapeDtypeStruct((B, S, sw_w), bf16)],
        scratch_shapes=[pltpu.VMEM((tm, D), bf16),
                        pltpu.VMEM((wqk.shape[1] // LANES, tm + SUBLANES, LANES), f32)],
        compiler_params=_params(2),
        name="inproj",
    )(x, wqk, wvo, wswg, cw, cb)


C_LANE, CM_LANE, B_LANE = 0, SUBLANES, 2 * SUBLANES


def _scan_chunks(x, op, fill, chunk):
    pos = lax.broadcasted_iota(jnp.int32, x.shape, 1) % chunk
    sh = 1
    while sh < chunk:
        x = op(x, jnp.where(pos >= sh, pltpu.roll(x, sh, 1), fill))
        sh *= 2
    return x


def _mlgate_kernel(g_ref, gb_ref, rows_ref, cols_ref, *, chunk):
    g = g_ref[0] + gb_ref[...]
    ig = jnp.concatenate([g[0:4], g[0:4]], axis=0)
    fg = jnp.concatenate([g[4:8], g[4:8]], axis=0)
    lf = jnp.minimum(fg, 0.0) - jnp.log1p(jnp.exp(-jnp.abs(fg)))
    b = _scan_chunks(lf, jnp.add, 0.0, chunk)
    c = ig - b
    cm = _scan_chunks(c, jnp.maximum, -jnp.inf, chunk)
    rows_ref[0] = c
    stack = jnp.concatenate(
        [c, cm, b, jnp.zeros((LANES - 3 * SUBLANES, c.shape[1]), f32)], axis=0)
    cols_ref[0] = stack.T


def _mlgate(gates, gbias, chunk):
    B, _, S = gates.shape
    return pl.pallas_call(
        functools.partial(_mlgate_kernel, chunk=chunk),
        grid=(B,),
        in_specs=[pl.BlockSpec((1, SUBLANES, S), lambda b: (b, 0, 0)), _const_spec(gbias.shape)],
        out_specs=[pl.BlockSpec((1, SUBLANES, S), lambda b: (b, 0, 0)),
                   pl.BlockSpec((1, S, LANES), lambda b: (b, 0, 0))],
        out_shape=[jax.ShapeDtypeStruct((B, SUBLANES, S), f32),
                   jax.ShapeDtypeStruct((B, S, LANES), f32)],
        compiler_params=_params(1),
        name="mlgate",
    )(gates, gbias)


def _mlstm_kernel(qk_ref, vo_ref, rows_ref, cols_ref, ng_ref, out_ref, c_ref, m_ref):
    L = qk_ref.shape[1]
    dh = ML_HEAD_DIM
    scale = dh ** -0.5

    @pl.when(pl.program_id(1) == 0)
    def _():
        c_ref[...] = jnp.zeros(c_ref.shape, f32)
        m_ref[...] = jnp.zeros(m_ref.shape, f32)

    row = lax.broadcasted_iota(jnp.int32, (L, L), 0)
    col = lax.broadcasted_iota(jnp.int32, (L, L), 1)
    causal = row >= col
    ones_blk = jnp.ones((L, dh), bf16)

    heads = range(ML_HEADS)
    hs = lambda h: slice(h * dh, (h + 1) * dh)
    ks = lambda h: slice(ML_WIDTH + h * dh, ML_WIDTH + (h + 1) * dh)
    m_all = m_ref[...]

    c_aug = [c_ref[h] for h in heads]
    v_aug = [jnp.concatenate([vo_ref[0, :, hs(h)], ones_blk], axis=1) for h in heads]
    qk_t = [lax.dot_general(qk_ref[0, :, hs(h)], qk_ref[0, :, ks(h)], (((1,), (1,)), ((), ())),
                            preferred_element_type=f32) for h in heads]
    q_c = [jnp.dot(qk_ref[0, :, hs(h)], c_aug[h].astype(bf16), preferred_element_type=f32)
           for h in heads]

    c_row, cm_b, big_m, m_prev, s_mat = [], [], [], [], []
    for h in heads:
        c_row.append(rows_ref[0, h:h + 1, :])
        cm_b.append(jnp.broadcast_to(cols_ref[0, :, CM_LANE + h:CM_LANE + h + 1], (L, dh)))
        m_prev.append(m_all[h:h + 1, 0:1])
        big_m.append(jnp.maximum(m_prev[h], cm_b[h]))
        cm_wide = jnp.concatenate([cm_b[h]] * (L // dh), axis=1)
        p = jnp.where(causal, jnp.exp((c_row[h] + math.log(scale)) - cm_wide), 0.0)
        s_mat.append((qk_t[h] * p).astype(bf16))
    s_v = [jnp.dot(s_mat[h], v_aug[h], preferred_element_type=f32) for h in heads]

    m_last, kw_t = [], []
    for h in heads:
        m_last.append(big_m[h][L - 1:L, 0:1])
        wk = jnp.exp(c_row[h] - m_last[h]) * scale
        kw_t.append((qk_ref[0, :, ks(h)].astype(f32).T * wk).astype(bf16))
    k_v = [jnp.dot(kw_t[h], v_aug[h], preferred_element_type=f32) for h in heads]

    m_new = []
    for h in heads:
        b_col = cols_ref[0, :, B_LANE + h:B_LANE + h + 1]
        b_b = jnp.broadcast_to(b_col, (L, dh))
        w_intra = jnp.exp(cm_b[h] - big_m[h])
        w_inter = jnp.exp(m_prev[h] - big_m[h])
        num = w_inter * q_c[h][:, :dh] + w_intra * s_v[h][:, :dh]
        den = w_inter * q_c[h][:, dh:] + w_intra * s_v[h][:, dh:]
        hh = num / jnp.maximum(jnp.abs(den), jnp.exp(-(b_b + big_m[h])))
        mu = jnp.mean(hh, axis=-1, keepdims=True)
        hc = hh - mu
        var = jnp.mean(hc * hc, axis=-1, keepdims=True)
        hn = hc * lax.rsqrt(var + EPS) * ng_ref[:, hs(h)]
        o = vo_ref[0, :, ks(h)]
        out_ref[0, :, hs(h)] = (jax.nn.sigmoid(o.astype(f32)) * hn).astype(bf16)
        c_ref[h] = jnp.exp(m_prev[h] - m_last[h]) * c_aug[h] + k_v[h]
        m_new.append(jnp.broadcast_to(b_col[L - 1:L, :] + m_last[h], (1, LANES)))
    m_ref[0:ML_HEADS, :] = jnp.concatenate(m_new, axis=0)


def _mlstm(qk, vo, gates, gbias, norm_g):
    B, S, _ = qk.shape
    L = min(ML_CHUNK, S)
    rows, cols = _mlgate(gates, gbias, L)
    grid = (B, S // L)
    return pl.pallas_call(
        _mlstm_kernel,
        grid=grid,
        in_specs=[pl.BlockSpec((1, L, 2 * ML_WIDTH), lambda b, s: (b, s, 0)),
                  pl.BlockSpec((1, L, 2 * ML_WIDTH), lambda b, s: (b, s, 0)),
                  pl.BlockSpec((1, SUBLANES, L), lambda b, s: (b, 0, s)),
                  pl.BlockSpec((1, L, LANES), lambda b, s: (b, s, 0)),
                  _const_spec(norm_g.shape)],
        out_specs=pl.BlockSpec((1, L, ML_WIDTH), lambda b, s: (b, s, 0)),
        out_shape=jax.ShapeDtypeStruct((B, S, ML_WIDTH), bf16),
        scratch_shapes=[pltpu.VMEM((ML_HEADS, ML_HEAD_DIM, 2 * ML_HEAD_DIM), f32),
                        pltpu.VMEM((SUBLANES, LANES), f32)],
        compiler_params=_params(2),
        name="mlstm",
    )(qk, vo, rows, cols, norm_g)


def _t5_bucket_table():
    r = np.arange(BLOCK)[:, None]
    c = np.arange(2 * BLOCK)[None, :]
    dist = BLOCK + r - c
    n = np.maximum(dist, 0)
    max_exact = REL_BUCKETS // 2
    nf = np.maximum(n, 1).astype(np.float32)
    large = max_exact + (np.log(nf / np.float32(max_exact)) / np.float32(math.log(REL_MAX_DIST / max_exact))
                         * np.float32(REL_BUCKETS - max_exact)).astype(np.int32)
    large = np.minimum(large, REL_BUCKETS - 1)
    bucket = np.where(n < max_exact, n, large)
    valid = (dist >= 0) & (dist < WINDOW)
    return np.where(valid, bucket, -1).astype(np.int32)


def _bias_kernel(bucket_ref, rb_ref, out_ref):
    bucket = bucket_ref[...]
    for h in range(SWA_HEADS):
        acc = jnp.full(bucket.shape, NEG_BIG, f32)
        for i in range(REL_BUCKETS):
            acc = jnp.where(bucket == i, rb_ref[i, h], acc)
        out_ref[h] = acc


def _swa_bias(rel_bias):
    bucket = jnp.asarray(_t5_bucket_table())
    return pl.pallas_call(
        _bias_kernel,
        in_specs=[pl.BlockSpec(memory_space=pltpu.VMEM), pl.BlockSpec(memory_space=pltpu.SMEM)],
        out_specs=pl.BlockSpec(memory_space=pltpu.VMEM),
        out_shape=jax.ShapeDtypeStruct((SWA_HEADS, BLOCK, 2 * BLOCK), f32),
        name="swa_bias",
    )(bucket, rel_bias)


def _swa_kernel(sw_ref, prev_ref, bias_ref, sink_ref, out_ref):
    tq = sw_ref.shape[1]
    scale = SWA_HEAD_DIM ** -0.5
    pair = 2 * SWA_HEAD_DIM
    kc = slice(SWA_WIDTH, SWA_WIDTH + pair)
    vc = slice(SWA_WIDTH + pair, SWA_WIDTH + 2 * pair)
    first_tile = pl.program_id(1) == 0
    lo = lax.broadcasted_iota(jnp.int32, (BLOCK, pair), 1) < SWA_HEAD_DIM
    left = lax.broadcasted_iota(jnp.int32, (BLOCK, 2 * BLOCK), 1) < BLOCK

    for j in range(tq // BLOCK):
        rows = slice(j * BLOCK, (j + 1) * BLOCK)
        prows = slice((j - 1) * BLOCK, j * BLOCK)
        if j == 0:
            k_prev, v_prev = prev_ref[0, :, 0:pair], prev_ref[0, :, pair:2 * pair]
        else:
            k_prev, v_prev = sw_ref[0, prows, kc], sw_ref[0, prows, vc]
        kband = jnp.concatenate([k_prev, sw_ref[0, rows, kc]], axis=0)
        vband = jnp.concatenate([v_prev, sw_ref[0, rows, vc]], axis=0)
        qs, heads = [], []
        for p in range(SWA_GROUP):
            qp = sw_ref[0, rows, p * pair:(p + 1) * pair] * scale
            zero = jnp.zeros_like(qp)
            qs += [jnp.where(lo, qp, zero), jnp.where(lo, zero, qp)]
            heads += [p, p + SWA_GROUP]
        logits = lax.dot_general(jnp.concatenate(qs, axis=0), kband, (((1,), (1,)), ((), ())),
                                 preferred_element_type=f32)
        ps, dens = [], []
        for n, head in enumerate(heads):
            lg = logits[n * BLOCK:(n + 1) * BLOCK] + bias_ref[head]
            if j == 0:
                lg = jnp.where(jnp.logical_and(first_tile, left), NEG_BIG, lg)
            sink = sink_ref[head]
            mx = jnp.maximum(jnp.max(lg, axis=-1, keepdims=True), sink)
            pr = jnp.exp(lg - mx)
            dens.append(jnp.sum(pr, axis=-1, keepdims=True) + jnp.exp(sink - mx))
            ps.append(pr.astype(bf16))
        pv = jnp.dot(jnp.concatenate(ps, axis=0), vband, preferred_element_type=f32)
        o = [pv[n * BLOCK:(n + 1) * BLOCK] / dens[n] for n in range(len(heads))]
        for p in range(SWA_GROUP):
            out_ref[0, rows, p * pair:(p + 1) * pair] = jnp.where(lo, o[2 * p], o[2 * p + 1]).astype(bf16)


def _swa(sw, bias, sinks):
    B, S, W = sw.shape
    tq = min(TQ_SWA, S)
    nblk = tq // BLOCK
    kvw = W - SWA_WIDTH
    assert kvw == 4 * SWA_HEAD_DIM and SWA_WIDTH % kvw == 0, "k/v group = [k0|k1|v0|v1]"
    return pl.pallas_call(
        _swa_kernel,
        grid=(B, S // tq),
        in_specs=[pl.BlockSpec((1, tq, W), lambda b, s: (b, s, 0)),
                  pl.BlockSpec((1, BLOCK, kvw),
                               lambda b, s: (b, jnp.maximum(s * nblk - 1, 0), SWA_WIDTH // kvw)),
                  _const_spec(bias.shape),
                  pl.BlockSpec(memory_space=pltpu.SMEM)],
        out_specs=pl.BlockSpec((1, tq, SWA_WIDTH), lambda b, s: (b, s, 0)),
        out_shape=jax.ShapeDtypeStruct((B, S, SWA_WIDTH), bf16),
        compiler_params=_params(2),
        name="swa",
    )(sw, sw, bias, sinks)


def _memkv_kernel(mem_ref, wkv_ref, k_ref, v_ref):
    d = k_ref.shape[2]
    kv = jnp.dot(mem_ref[0].astype(bf16), wkv_ref[...], preferred_element_type=f32)
    k_ref[0] = kv[:, :d].astype(bf16)
    v_ref[0] = kv[:, d:].astype(bf16)


def _memkv(mem, wkv):
    B, M, D = mem.shape
    blk = pl.BlockSpec((1, M, D), lambda b: (b, 0, 0))
    return pl.pallas_call(
        _memkv_kernel,
        grid=(B,),
        in_specs=[blk, _const_spec(wkv.shape)],
        out_specs=[blk, blk],
        out_shape=[jax.ShapeDtypeStruct((B, M, D), bf16)] * 2,
        compiler_params=_params(1),
        name="memkv",
    )(mem, wkv)


def _mix_kernel(hml_ref, hsw_ref, x_ref, woml_ref, wosw_ref, g1_ref, b1_ref,
                wq_ref, k_ref, v_ref, wo_ref, g2_ref, b2_ref, out_ref):
    tm, d = x_ref.shape[1], x_ref.shape[2]
    dh = d // XA_HEADS
    rows = [slice(i * tm // MIX_STREAMS, (i + 1) * tm // MIX_STREAMS) for i in range(MIX_STREAMS)]
    streams = range(MIX_STREAMS)
    cs = lambda a: slice(a * dh, (a + 1) * dh)

    h = [jnp.dot(hml_ref[0, r, :], woml_ref[...], preferred_element_type=f32)
         + jnp.dot(hsw_ref[0, r, :], wosw_ref[...], preferred_element_type=f32) for r in rows]
    x1, q = [], []
    for i in streams:
        x1.append(_layer_norm(ALPHA * x_ref[0, rows[i], :] + h[i], g1_ref[...], b1_ref[...]))
        q.append(jnp.dot(x1[i].astype(bf16), wq_ref[...], preferred_element_type=f32).astype(bf16))

    items = [(a, i) for a in range(XA_HEADS) for i in streams]
    lag = 2
    prob, den, outs = {}, {}, {}

    def logits(a, i):
        lg = lax.dot_general(q[i][:, cs(a)], k_ref[0, :, cs(a)], (((1,), (1,)), ((), ())),
                             preferred_element_type=f32) * (dh ** -0.5)
        p = jnp.exp(lg - jnp.max(lg, axis=-1, keepdims=True))
        den[a, i] = jnp.sum(p, axis=-1, keepdims=True)
        prob[a, i] = p.astype(bf16)

    def values(a, i):
        o = jnp.dot(prob[a, i], v_ref[0, :, cs(a)], preferred_element_type=f32) / den[a, i]
        outs[a, i] = o.astype(bf16)

    for n, item in enumerate(items):
        logits(*item)
        if n >= lag:
            values(*items[n - lag])
    for item in items[-lag:]:
        values(*item)

    h2 = [jnp.dot(jnp.concatenate([outs[a, i] for a in range(XA_HEADS)], axis=1), wo_ref[...],
                  preferred_element_type=f32) for i in streams]
    for i in streams:
        out_ref[0, rows[i], :] = _layer_norm(ALPHA * x1[i] + h2[i], g2_ref[...], b2_ref[...])


def _mix(hml, hsw, x, woml, wosw, g1, b1, wq, k, v, wo, g2, b2):
    B, S, D = x.shape
    tm = min(TM_MIX, S)
    M = k.shape[1]
    tok = lambda w: pl.BlockSpec((1, tm, w), lambda b, s: (b, s, 0))
    memblk = pl.BlockSpec((1, M, D), lambda b, s: (b, 0, 0))
    consts = [woml, wosw, g1, b1, wq]
    return pl.pallas_call(
        _mix_kernel,
        grid=(B, S // tm),
        in_specs=[tok(hml.shape[2]), tok(hsw.shape[2]), tok(D)]
                 + [_const_spec(a.shape) for a in consts]
                 + [memblk, memblk, _const_spec(wo.shape), _const_spec(g2.shape), _const_spec(b2.shape)],
        out_specs=tok(D),
        out_shape=jax.ShapeDtypeStruct((B, S, D), f32),
        compiler_params=_params(2),
        name="mix",
    )(hml, hsw, x, woml, wosw, g1, b1, wq, k, v, wo, g2, b2)


GELU_C0 = math.sqrt(2.0 / math.pi)
GELU_C1 = 0.044715 * GELU_C0


def _ffn_kernel(x_ref, wup_ref, cw_ref, cb_ref, wdn_ref, g_ref, b_ref, out_ref,
                carry_ref, xb_ref, u00_ref, u01_ref, u10_ref, u11_ref, h_ref):
    tm = x_ref.shape[1]
    d_ff = wdn_ref.shape[0]
    cwid = FF_CHUNK
    n_chunks = d_ff // cwid
    n_slabs = cwid // LANES
    halo = SUBLANES
    u_refs = ((u00_ref, u01_ref), (u10_ref, u11_ref))

    @pl.when(pl.program_id(1) == 0)
    def _():
        carry_ref[...] = jnp.zeros(carry_ref.shape, f32)

    xb_ref[...] = x_ref[0].astype(bf16)

    def up(ci):
        for u_ref, c0 in zip(u_refs[ci % 2], (ci * cwid, d_ff + ci * cwid)):
            u = jnp.dot(xb_ref[...], wup_ref[:, c0:c0 + cwid], preferred_element_type=f32)
            for sl in range(n_slabs):
                cols = slice(c0 + sl * LANES, c0 + (sl + 1) * LANES)
                u_ref[sl, 0:halo, :] = carry_ref[:, cols]
                u_ref[sl, halo:halo + tm, :] = u[:, sl * LANES:(sl + 1) * LANES]
                carry_ref[:, cols] = u_ref[sl, tm:tm + halo, :]

    def act(ci):
        gu_ref, vu_ref = u_refs[ci % 2]
        for sl in range(n_slabs):
            gc = slice(ci * cwid + sl * LANES, ci * cwid + (sl + 1) * LANES)
            vc = slice(d_ff + gc.start, d_ff + gc.stop)
            gb, gw = cb_ref[:, gc], [cw_ref[j:j + 1, gc] for j in range(FFN_CONV)]
            vb, vw = 0.5 * cb_ref[:, vc], [0.5 * cw_ref[j:j + 1, vc] for j in range(FFN_CONV)]
            for r0 in range(0, tm, FF_ROWS):
                g, v = gb, vb
                for j in range(FFN_CONV):
                    rows = slice(r0 + halo - (FFN_CONV - 1) + j, r0 + halo - (FFN_CONV - 1) + j + FF_ROWS)
                    g = g + gu_ref[sl, rows, :] * gw[j]
                    v = v + vu_ref[sl, rows, :] * vw[j]
                t = jnp.tanh(g * (GELU_C0 + GELU_C1 * (g * g)))
                h_ref[r0:r0 + FF_ROWS, gc] = ((g * v) * (1.0 + t)).astype(bf16)

    up(0)
    for ci in range(n_chunks - 1):
        up(ci + 1)
        act(ci)
    top, bot = slice(0, tm // 2), slice(tm // 2, tm)
    k1 = (n_chunks - 1) * cwid
    h_top = jnp.dot(h_ref[top, :k1], wdn_ref[:k1, :], preferred_element_type=f32)
    act(n_chunks - 1)
    h_top = h_top + jnp.dot(h_ref[top, k1:], wdn_ref[k1:, :], preferred_element_type=f32)
    h_bot = jnp.dot(h_ref[bot, :], wdn_ref[...], preferred_element_type=f32)
    out_ref[0, top, :] = _layer_norm(ALPHA * x_ref[0, top, :] + h_top, g_ref[...], b_ref[...])
    out_ref[0, bot, :] = _layer_norm(ALPHA * x_ref[0, bot, :] + h_bot, g_ref[...], b_ref[...])


def _ffn(x, wup, cw, cb, wdn, g, b):
    B, S, D = x.shape
    tm = min(TM_FFN, S)
    cwid = FF_CHUNK
    assert wdn.shape[0] % cwid == 0 and wup.shape[1] == 2 * wdn.shape[0]
    tok = pl.BlockSpec((1, tm, D), lambda b_, s: (b_, s, 0))
    return pl.pallas_call(
        _ffn_kernel,
        grid=(B, S // tm),
        in_specs=[tok] + [_const_spec(a.shape) for a in (wup, cw, cb, wdn, g, b)],
        out_specs=tok,
        out_shape=jax.ShapeDtypeStruct((B, S, D), f32),
        scratch_shapes=[pltpu.VMEM((SUBLANES, wup.shape[1]), f32),
                        pltpu.VMEM((tm, D), bf16),
                        ]
                       + [pltpu.VMEM((cwid // LANES, tm + SUBLANES, LANES), f32)] * 4
                       + [pltpu.VMEM((tm, wdn.shape[0]), bf16)],
        compiler_params=_params(2),
        name="ffn",
    )(x, wup, cw, cb, wdn, g, b)


def _prep_inproj(w_in):
    w = w_in.astype(bf16)
    D = w.shape[0]
    o = 4 * ML_WIDTH
    gates = w[:, o:o + 2 * ML_HEADS]
    o += 2 * ML_HEADS
    swq = w[:, o:o + SWA_WIDTH].reshape(D, SWA_KV_HEADS, SWA_GROUP, SWA_HEAD_DIM)
    swq = swq.transpose(0, 2, 1, 3).reshape(D, SWA_WIDTH)
    o += SWA_WIDTH
    kv = w[:, o:]
    used = SWA_WIDTH + kv.shape[1] + 2 * ML_HEADS
    pad = jnp.zeros((D, -used % (2 * LANES)), bf16)
    return w[:, :2 * ML_WIDTH], w[:, 2 * ML_WIDTH:4 * ML_WIDTH], jnp.concatenate([swq, kv, gates, pad], axis=1)


def _pair_rows(w_rows):
    n = w_rows.shape[1]
    w_rows = w_rows.reshape(SWA_KV_HEADS, SWA_GROUP, SWA_HEAD_DIM, n)
    return w_rows.transpose(1, 0, 2, 3).reshape(SWA_WIDTH, n)


def kernel(x, mem, rel_bias, w_in, ml_conv_w, ml_conv_b, ml_i_bias, ml_f_bias, ml_norm_g, swa_sinks, w_out, ln1_g, ln1_b, xa_wq, xa_wkv, xa_wo, ln2_g, ln2_b, ffn_w_up, ffn_conv_w, ffn_conv_b, ffn_w_down, ln3_g, ln3_b):
    depth = w_in.shape[0]
    row = lambda a: a.reshape(1, -1)
    bias = _swa_bias(rel_bias)
    for l in range(depth):
        wqk, wvo, wswg = _prep_inproj(w_in[l])
        qk, vo, gates, sw = _inproj(x, wqk, wvo, wswg, ml_conv_w[l], row(ml_conv_b[l]))
        gbias = jnp.concatenate([ml_i_bias[l], ml_f_bias[l]]).reshape(2 * ML_HEADS, 1)
        hml = _mlstm(qk, vo, gates, gbias, row(ml_norm_g[l]))
        hsw = _swa(sw, bias, swa_sinks[l])
        wo = w_out[l].astype(bf16)
        k, v = _memkv(mem, xa_wkv[l].astype(bf16))
        x = _mix(hml, hsw, x, wo[:ML_WIDTH], _pair_rows(wo[ML_WIDTH:]), row(ln1_g[l]), row(ln1_b[l]),
                 xa_wq[l].astype(bf16), k, v, xa_wo[l].astype(bf16), row(ln2_g[l]), row(ln2_b[l]))
        x = _ffn(x, ffn_w_up[l].astype(bf16), ffn_conv_w[l], row(ffn_conv_b[l]),
                 ffn_w_down[l].astype(bf16), row(ln3_g[l]), row(ln3_b[l]))
    return x
```

```python
import functools
import math

import numpy as np
import jax
import jax.numpy as jnp
from jax import lax
from jax.experimental import pallas as pl
from jax.experimental.pallas import tpu as pltpu

f32 = jnp.float32
bf16 = jnp.bfloat16

ML_HEADS = 4
ML_HEAD_DIM = 128
ML_WIDTH = ML_HEADS * ML_HEAD_DIM
ML_CONV = 4
SWA_HEADS = 8
SWA_KV_HEADS = 2
SWA_GROUP = SWA_HEADS // SWA_KV_HEADS
SWA_HEAD_DIM = 64
SWA_WIDTH = SWA_HEADS * SWA_HEAD_DIM
WINDOW = 128
BLOCK = 128
REL_BUCKETS = 32
REL_MAX_DIST = 128
XA_HEADS = 4
FFN_CONV = 3
DEPTH = 2
ALPHA = (2.0 * DEPTH) ** 0.25
EPS = 1e-5

LANES = 128
SUBLANES = 8
TM_PROJ = 1024
PROJ_ROWS = 128
ML_CHUNK = 512
TQ_SWA = 512
TM_MIX = 1024
MIX_STREAMS = 4
TM_FFN = 512
FF_CHUNK = 256
FF_ROWS = 64
VMEM_LIMIT = 56 * 1024 * 1024
NEG_BIG = -1e30


def _params(n_axes, flags=None):
    return pltpu.CompilerParams(
        dimension_semantics=("arbitrary",) * n_axes, vmem_limit_bytes=VMEM_LIMIT, flags=flags)


class _Layer:
    def __init__(self, stacked, l):
        self.stacked, self.l = stacked, l

    @property
    def shape(self):
        return self.stacked.shape[1:]


def _arr(op):
    return op.stacked if isinstance(op, _Layer) else op


def _const_spec(op):
    nd = len(op.shape)
    if isinstance(op, _Layer):
        l = op.l
        return pl.BlockSpec((None,) + tuple(op.shape), lambda *_: (l,) + (0,) * nd,
                            pipeline_mode=pl.Buffered(1))
    return pl.BlockSpec(op.shape, lambda *_: (0,) * nd, pipeline_mode=pl.Buffered(1))


def _layer_norm(z, g, b):
    mu = jnp.mean(z, axis=-1, keepdims=True)
    zc = z - mu
    var = jnp.mean(zc * zc, axis=-1, keepdims=True)
    return zc * lax.rsqrt(var + EPS) * g + b


def _inproj_kernel(x_ref, wqk_ref, wvo_ref, wswg_ref, cw_ref, cb_ref,
                   qk_ref, vo_ref, g_ref, sw_ref, xb_ref, p_ref):
    tm = x_ref.shape[1]
    n_slabs = p_ref.shape[0]
    halo = SUBLANES

    @pl.when(pl.program_id(1) == 0)
    def _():
        p_ref[:, 0:halo, :] = jnp.zeros((n_slabs, halo, LANES), f32)

    xb_ref[...] = x_ref[0].astype(bf16)
    acc = jnp.dot(xb_ref[...], wqk_ref[...], preferred_element_type=f32)
    for sl in range(n_slabs):
        p_ref[sl, halo:halo + tm, :] = acc[:, sl * LANES:(sl + 1) * LANES]

    vo_ref[0] = jnp.dot(xb_ref[...], wvo_ref[...], preferred_element_type=f32).astype(bf16)
    accs = jnp.dot(xb_ref[...], wswg_ref[...], preferred_element_type=f32)
    sw_w = sw_ref.shape[2]
    sw_ref[0] = accs[:, :sw_w].astype(bf16)
    g_ref[0] = accs[:, sw_w:sw_w + LANES].T[0:SUBLANES, :]

    for sl in range(n_slabs):
        lanes = slice(sl * LANES, (sl + 1) * LANES)
        for r0 in range(0, tm, PROJ_ROWS):
            y = cb_ref[:, lanes] + p_ref[sl, r0 + 5:r0 + 5 + PROJ_ROWS, :] * cw_ref[0:1, lanes]
            y = y + p_ref[sl, r0 + 6:r0 + 6 + PROJ_ROWS, :] * cw_ref[1:2, lanes]
            y = y + p_ref[sl, r0 + 7:r0 + 7 + PROJ_ROWS, :] * cw_ref[2:3, lanes]
            y = y + p_ref[sl, r0 + 8:r0 + 8 + PROJ_ROWS, :] * cw_ref[3:4, lanes]
            qk_ref[0, r0:r0 + PROJ_ROWS, lanes] = (y * jax.nn.sigmoid(y)).astype(bf16)
        p_ref[sl, 0:halo, :] = p_ref[sl, tm:tm + halo, :]


def _inproj(x, wqk, wvo, wswg, cw, cb):
    B, S, D = x.shape
    tm = min(TM_PROJ, S)
    grid = (B, S // tm)
    sw_w = SWA_WIDTH + 4 * SWA_HEAD_DIM
    tok = lambda w: pl.BlockSpec((1, tm, w), lambda b, s: (b, s, 0))
    return pl.pallas_call(
        _inproj_kernel,
        grid=grid,
        in_specs=[tok(D)] + [_const_spec(a) for a in (wqk, wvo, wswg, cw, cb)],
        out_specs=[tok(wqk.shape[1]), tok(wvo.shape[1]),
                   pl.BlockSpec((1, SUBLANES, tm), lambda b, s: (b, 0, s)),
                   tok(sw_w)],
        out_shape=[jax.ShapeDtypeStruct((B, S, wqk.shape[1]), bf16),
                   jax.ShapeDtypeStruct((B, S, wvo.shape[1]), bf16),
                   jax.ShapeDtypeStruct((B, SUBLANES, S), f32),
                   jax.ShapeDtypeStruct((B, S, sw_w), bf16)],
        scratch_shapes=[pltpu.VMEM((tm, D), bf16),
                        pltpu.VMEM((wqk.shape[1] // LANES, tm + SUBLANES, LANES), f32)],
        compiler_params=_params(2),
        name="inproj",
    )(x, *map(_arr, (wqk, wvo, wswg, cw, cb)))


C_LANE, CM_LANE, B_LANE = 0, SUBLANES, 2 * SUBLANES


def _scan_chunks(x, op, fill, chunk):
    pos = lax.broadcasted_iota(jnp.int32, x.shape, 1) % chunk
    sh = 1
    while sh < chunk:
        x = op(x, jnp.where(pos >= sh, pltpu.roll(x, sh, 1), fill))
        sh *= 2
    return x


def _mlgate_kernel(g_ref, gb_ref, rows_ref, cols_ref, *, chunk):
    g = g_ref[0] + gb_ref[...]
    ig = jnp.concatenate([g[0:4], g[0:4]], axis=0)
    fg = jnp.concatenate([g[4:8], g[4:8]], axis=0)
    lf = jnp.minimum(fg, 0.0) - jnp.log1p(jnp.exp(-jnp.abs(fg)))
    b = _scan_chunks(lf, jnp.add, 0.0, chunk)
    c = ig - b
    cm = _scan_chunks(c, jnp.maximum, -jnp.inf, chunk)
    rows_ref[0] = c
    stack = jnp.concatenate(
        [c, cm, b, jnp.zeros((LANES - 3 * SUBLANES, c.shape[1]), f32)], axis=0)
    cols_ref[0] = stack.T


def _mlgate(gates, gbias, chunk):
    B, _, S = gates.shape
    return pl.pallas_call(
        functools.partial(_mlgate_kernel, chunk=chunk),
        grid=(B,),
        in_specs=[pl.BlockSpec((1, SUBLANES, S), lambda b: (b, 0, 0)), _const_spec(gbias)],
        out_specs=[pl.BlockSpec((1, SUBLANES, S), lambda b: (b, 0, 0)),
                   pl.BlockSpec((1, S, LANES), lambda b: (b, 0, 0))],
        out_shape=[jax.ShapeDtypeStruct((B, SUBLANES, S), f32),
                   jax.ShapeDtypeStruct((B, S, LANES), f32)],
        compiler_params=_params(1),
        name="mlgate",
    )(gates, _arr(gbias))


def _mlstm_kernel(qk_ref, vo_ref, rows_ref, cols_ref, ng_ref, out_ref, c_ref, m_ref):
    L = qk_ref.shape[1]
    dh = ML_HEAD_DIM
    scale = dh ** -0.5

    @pl.when(pl.program_id(1) == 0)
    def _():
        c_ref[...] = jnp.zeros(c_ref.shape, f32)
        m_ref[...] = jnp.zeros(m_ref.shape, f32)

    row = lax.broadcasted_iota(jnp.int32, (L, L), 0)
    col = lax.broadcasted_iota(jnp.int32, (L, L), 1)
    causal = row >= col
    ones_blk = jnp.ones((L, dh), bf16)

    heads = range(ML_HEADS)
    hs = lambda h: slice(h * dh, (h + 1) * dh)
    ks = lambda h: slice(ML_WIDTH + h * dh, ML_WIDTH + (h + 1) * dh)
    m_all = m_ref[...]

    c_aug = [c_ref[h] for h in heads]
    v_aug = [jnp.concatenate([vo_ref[0, :, hs(h)], ones_blk], axis=1) for h in heads]
    qk_t = [lax.dot_general(qk_ref[0, :, hs(h)], qk_ref[0, :, ks(h)], (((1,), (1,)), ((), ())),
                            preferred_element_type=f32) for h in heads]
    q_c = [jnp.dot(qk_ref[0, :, hs(h)], c_aug[h].astype(bf16), preferred_element_type=f32)
           for h in heads]

    c_row, cm_b, big_m, m_prev, s_mat = [], [], [], [], []
    for h in heads:
        c_row.append(rows_ref[0, h:h + 1, :])
        cm_b.append(jnp.broadcast_to(cols_ref[0, :, CM_LANE + h:CM_LANE + h + 1], (L, dh)))
        m_prev.append(m_all[h:h + 1, 0:1])
        big_m.append(jnp.maximum(m_prev[h], cm_b[h]))
        cm_wide = jnp.concatenate([cm_b[h]] * (L // dh), axis=1)
        p = jnp.where(causal, jnp.exp((c_row[h] + math.log(scale)) - cm_wide), 0.0)
        s_mat.append((qk_t[h] * p).astype(bf16))
    s_v = [jnp.dot(s_mat[h], v_aug[h], preferred_element_type=f32) for h in heads]

    m_last, kw_t = [], []
    for h in heads:
        m_last.append(big_m[h][L - 1:L, 0:1])
        wk = jnp.exp(c_row[h] - m_last[h]) * scale
        kw_t.append((qk_ref[0, :, ks(h)].astype(f32).T * wk).astype(bf16))
    k_v = [jnp.dot(kw_t[h], v_aug[h], preferred_element_type=f32) for h in heads]

    m_new = []
    for h in heads:
        b_col = cols_ref[0, :, B_LANE + h:B_LANE + h + 1]
        b_b = jnp.broadcast_to(b_col, (L, dh))
        w_intra = jnp.exp(cm_b[h] - big_m[h])
        w_inter = jnp.exp(m_prev[h] - big_m[h])
        num = w_inter * q_c[h][:, :dh] + w_intra * s_v[h][:, :dh]
        den = w_inter * q_c[h][:, dh:] + w_intra * s_v[h][:, dh:]
        hh = num / jnp.maximum(jnp.abs(den), jnp.exp(-(b_b + big_m[h])))
        mu = jnp.mean(hh, axis=-1, keepdims=True)
        hc = hh - mu
        var = jnp.mean(hc * hc, axis=-1, keepdims=True)
        hn = hc * lax.rsqrt(var + EPS) * ng_ref[:, hs(h)]
        o = vo_ref[0, :, ks(h)]
        out_ref[0, :, hs(h)] = (jax.nn.sigmoid(o.astype(f32)) * hn).astype(bf16)
        c_ref[h] = jnp.exp(m_prev[h] - m_last[h]) * c_aug[h] + k_v[h]
        m_new.append(jnp.broadcast_to(b_col[L - 1:L, :] + m_last[h], (1, LANES)))
    m_ref[0:ML_HEADS, :] = jnp.concatenate(m_new, axis=0)


def _mlstm(qk, vo, gates, gbias, norm_g):
    B, S, _ = qk.shape
    L = min(ML_CHUNK, S)
    rows, cols = _mlgate(gates, gbias, L)
    grid = (B, S // L)
    return pl.pallas_call(
        _mlstm_kernel,
        grid=grid,
        in_specs=[pl.BlockSpec((1, L, 2 * ML_WIDTH), lambda b, s: (b, s, 0)),
                  pl.BlockSpec((1, L, 2 * ML_WIDTH), lambda b, s: (b, s, 0)),
                  pl.BlockSpec((1, SUBLANES, L), lambda b, s: (b, 0, s)),
                  pl.BlockSpec((1, L, LANES), lambda b, s: (b, s, 0)),
                  _const_spec(norm_g)],
        out_specs=pl.BlockSpec((1, L, ML_WIDTH), lambda b, s: (b, s, 0)),
        out_shape=jax.ShapeDtypeStruct((B, S, ML_WIDTH), bf16),
        scratch_shapes=[pltpu.VMEM((ML_HEADS, ML_HEAD_DIM, 2 * ML_HEAD_DIM), f32),
                        pltpu.VMEM((SUBLANES, LANES), f32)],
        compiler_params=_params(2),
        name="mlstm",
    )(qk, vo, rows, cols, _arr(norm_g))


def _t5_bucket_table():
    assert WINDOW == BLOCK
    r = np.arange(BLOCK)[:, None]
    c = np.arange(BLOCK)[None, :]
    n = np.where(c > r, BLOCK + r - c, r - c)
    max_exact = REL_BUCKETS // 2
    nf = np.maximum(n, 1).astype(np.float32)
    large = max_exact + (np.log(nf / np.float32(max_exact)) / np.float32(math.log(REL_MAX_DIST / max_exact))
                         * np.float32(REL_BUCKETS - max_exact)).astype(np.int32)
    large = np.minimum(large, REL_BUCKETS - 1)
    return np.where(n < max_exact, n, large).astype(np.int32)


def _bias_kernel(bucket_ref, rb_ref, out_ref):
    bucket = bucket_ref[...]
    for h in range(SWA_HEADS):
        acc = jnp.zeros(bucket.shape, f32)
        for i in range(REL_BUCKETS):
            acc = jnp.where(bucket == i, rb_ref[i, h], acc)
        out_ref[h] = acc


def _swa_bias(rel_bias):
    bucket = jnp.asarray(_t5_bucket_table())
    return pl.pallas_call(
        _bias_kernel,
        in_specs=[pl.BlockSpec(memory_space=pltpu.VMEM), pl.BlockSpec(memory_space=pltpu.SMEM)],
        out_specs=pl.BlockSpec(memory_space=pltpu.VMEM),
        out_shape=jax.ShapeDtypeStruct((SWA_HEADS, BLOCK, BLOCK), f32),
        name="swa_bias",
    )(bucket, rel_bias)


def _swa_kernel(sw_ref, prev_ref, bias_ref, sink_ref, out_ref, *, layer):
    tq = sw_ref.shape[1]
    scale = SWA_HEAD_DIM ** -0.5
    pair = 2 * SWA_HEAD_DIM
    kc = slice(SWA_WIDTH, SWA_WIDTH + pair)
    vc = slice(SWA_WIDTH + pair, SWA_WIDTH + 2 * pair)
    first_tile = pl.program_id(1) == 0
    lo = lax.broadcasted_iota(jnp.int32, (BLOCK, pair), 1) < SWA_HEAD_DIM
    use_prev = (lax.broadcasted_iota(jnp.int32, (BLOCK, BLOCK), 1)
                > lax.broadcasted_iota(jnp.int32, (BLOCK, BLOCK), 0))
    zero_p = jnp.zeros((BLOCK, BLOCK), bf16)

    for j in range(tq // BLOCK):
        rows = slice(j * BLOCK, (j + 1) * BLOCK)
        prows = slice((j - 1) * BLOCK, j * BLOCK)
        if j == 0:
            k_prev, v_prev = prev_ref[0, :, 0:pair], prev_ref[0, :, pair:2 * pair]
        else:
            k_prev, v_prev = sw_ref[0, prows, kc], sw_ref[0, prows, vc]
        kband = jnp.concatenate([k_prev, sw_ref[0, rows, kc]], axis=0)
        vband = jnp.concatenate([v_prev, sw_ref[0, rows, vc]], axis=0)
        qs, heads = [], []
        for p in range(SWA_GROUP):
            qp = sw_ref[0, rows, p * pair:(p + 1) * pair] * scale
            zero = jnp.zeros_like(qp)
            qs += [jnp.where(lo, qp, zero), jnp.where(lo, zero, qp)]
            heads += [p, p + SWA_GROUP]
        logits = lax.dot_general(jnp.concatenate(qs, axis=0), kband, (((1,), (1,)), ((), ())),
                                 preferred_element_type=f32)
        ps, dens = [], []
        for n, head in enumerate(heads):
            lg2 = logits[n * BLOCK:(n + 1) * BLOCK]
            lg = jnp.where(use_prev, lg2[:, :BLOCK], lg2[:, BLOCK:]) + bias_ref[head]
            if j == 0:
                lg = jnp.where(jnp.logical_and(first_tile, use_prev), NEG_BIG, lg)
            sink = sink_ref[layer, head]
            mx = jnp.maximum(jnp.max(lg, axis=-1, keepdims=True), sink)
            pr = jnp.exp(lg - mx)
            dens.append(jnp.sum(pr, axis=-1, keepdims=True) + jnp.exp(sink - mx))
            pb = pr.astype(bf16)
            ps.append(jnp.concatenate([jnp.where(use_prev, pb, zero_p),
                                       jnp.where(use_prev, zero_p, pb)], axis=1))
        pv = jnp.dot(jnp.concatenate(ps, axis=0), vband, preferred_element_type=f32)
        o = [pv[n * BLOCK:(n + 1) * BLOCK] / dens[n] for n in range(len(heads))]
        for p in range(SWA_GROUP):
            out_ref[0, rows, p * pair:(p + 1) * pair] = jnp.where(lo, o[2 * p], o[2 * p + 1]).astype(bf16)


def _swa(sw, bias, sinks, layer):
    B, S, W = sw.shape
    tq = min(TQ_SWA, S)
    nblk = tq // BLOCK
    kvw = W - SWA_WIDTH
    assert kvw == 4 * SWA_HEAD_DIM and SWA_WIDTH % kvw == 0, "k/v group = [k0|k1|v0|v1]"
    return pl.pallas_call(
        functools.partial(_swa_kernel, layer=layer),
        grid=(B, S // tq),
        in_specs=[pl.BlockSpec((1, tq, W), lambda b, s: (b, s, 0)),
                  pl.BlockSpec((1, BLOCK, kvw),
                               lambda b, s: (b, jnp.maximum(s * nblk - 1, 0), SWA_WIDTH // kvw)),
                  _const_spec(bias),
                  pl.BlockSpec(memory_space=pltpu.SMEM)],
        out_specs=pl.BlockSpec((1, tq, SWA_WIDTH), lambda b, s: (b, s, 0)),
        out_shape=jax.ShapeDtypeStruct((B, S, SWA_WIDTH), bf16),
        compiler_params=_params(2),
        name="swa",
    )(sw, sw, bias, sinks)


def _memkv_kernel(mem_ref, wkv_ref, k_ref, v_ref):
    d = k_ref.shape[2]
    kv = jnp.dot(mem_ref[0].astype(bf16), wkv_ref[...], preferred_element_type=f32)
    k_ref[0] = kv[:, :d].astype(bf16)
    v_ref[0] = kv[:, d:].astype(bf16)


def _memkv(mem, wkv):
    B, M, D = mem.shape
    blk = pl.BlockSpec((1, M, D), lambda b: (b, 0, 0))
    return pl.pallas_call(
        _memkv_kernel,
        grid=(B,),
        in_specs=[blk, _const_spec(wkv)],
        out_specs=[blk, blk],
        out_shape=[jax.ShapeDtypeStruct((B, M, D), bf16)] * 2,
        compiler_params=_params(1),
        name="memkv",
    )(mem, _arr(wkv))


def _mix_kernel(hml_ref, hsw_ref, x_ref, woml_ref, wosw_ref, g1_ref, b1_ref,
                wq_ref, k_ref, v_ref, wo_ref, g2_ref, b2_ref, out_ref):
    tm, d = x_ref.shape[1], x_ref.shape[2]
    dh = d // XA_HEADS
    rows = [slice(i * tm // MIX_STREAMS, (i + 1) * tm // MIX_STREAMS) for i in range(MIX_STREAMS)]
    streams = range(MIX_STREAMS)
    cs = lambda a: slice(a * dh, (a + 1) * dh)

    h = [jnp.dot(hml_ref[0, r, :], woml_ref[...], preferred_element_type=f32)
         + jnp.dot(hsw_ref[0, r, :], wosw_ref[...], preferred_element_type=f32) for r in rows]
    x1, q = [], []
    for i in streams:
        x1.append(_layer_norm(ALPHA * x_ref[0, rows[i], :] + h[i], g1_ref[...], b1_ref[...]))
        q.append(jnp.dot(x1[i].astype(bf16), wq_ref[...], preferred_element_type=f32).astype(bf16))

    items = [(a, i) for a in range(XA_HEADS) for i in streams]
    lag = 2
    prob, den, outs = {}, {}, {}

    def logits(a, i):
        lg = lax.dot_general(q[i][:, cs(a)], k_ref[0, :, cs(a)], (((1,), (1,)), ((), ())),
                             preferred_element_type=f32) * (dh ** -0.5)
        p = jnp.exp(lg - jnp.max(lg, axis=-1, keepdims=True))
        den[a, i] = jnp.sum(p, axis=-1, keepdims=True)
        prob[a, i] = p.astype(bf16)

    def values(a, i):
        o = jnp.dot(prob[a, i], v_ref[0, :, cs(a)], preferred_element_type=f32) / den[a, i]
        outs[a, i] = o.astype(bf16)

    for n, item in enumerate(items):
        logits(*item)
        if n >= lag:
            values(*items[n - lag])
    for item in items[-lag:]:
        values(*item)

    h2 = [jnp.dot(jnp.concatenate([outs[a, i] for a in range(XA_HEADS)], axis=1), wo_ref[...],
                  preferred_element_type=f32) for i in streams]
    for i in streams:
        out_ref[0, rows[i], :] = _layer_norm(ALPHA * x1[i] + h2[i], g2_ref[...], b2_ref[...])


def _mix(hml, hsw, x, woml, wosw, g1, b1, wq, k, v, wo, g2, b2):
    B, S, D = x.shape
    tm = min(TM_MIX, S)
    M = k.shape[1]
    tok = lambda w: pl.BlockSpec((1, tm, w), lambda b, s: (b, s, 0))
    memblk = pl.BlockSpec((1, M, D), lambda b, s: (b, 0, 0))
    consts = [woml, wosw, g1, b1, wq]
    return pl.pallas_call(
        _mix_kernel,
        grid=(B, S // tm),
        in_specs=[tok(hml.shape[2]), tok(hsw.shape[2]), tok(D)]
                 + [_const_spec(a) for a in consts]
                 + [memblk, memblk] + [_const_spec(a) for a in (wo, g2, b2)],
        out_specs=tok(D),
        out_shape=jax.ShapeDtypeStruct((B, S, D), f32),
        compiler_params=_params(2),
        name="mix",
    )(hml, hsw, x, *map(_arr, consts), k, v, *map(_arr, (wo, g2, b2)))


GELU_C0 = math.sqrt(2.0 / math.pi)
GELU_C1 = 0.044715 * GELU_C0


def _ffn_kernel(x_ref, wup_ref, cw_ref, cb_ref, wdn_ref, g_ref, b_ref, out_ref,
                carry_ref, xb_ref, u00_ref, u01_ref, u10_ref, u11_ref, h_ref):
    tm = x_ref.shape[1]
    d_ff = wdn_ref.shape[0]
    cwid = FF_CHUNK
    n_chunks = d_ff // cwid
    n_slabs = cwid // LANES
    halo = SUBLANES
    u_refs = ((u00_ref, u01_ref), (u10_ref, u11_ref))

    @pl.when(pl.program_id(1) == 0)
    def _():
        carry_ref[...] = jnp.zeros(carry_ref.shape, f32)

    xb_ref[...] = x_ref[0].astype(bf16)

    def up(ci):
        for u_ref, c0 in zip(u_refs[ci % 2], (ci * cwid, d_ff + ci * cwid)):
            u = jnp.dot(xb_ref[...], wup_ref[:, c0:c0 + cwid], preferred_element_type=f32)
            for sl in range(n_slabs):
                cols = slice(c0 + sl * LANES, c0 + (sl + 1) * LANES)
                u_ref[sl, 0:halo, :] = carry_ref[:, cols]
                u_ref[sl, halo:halo + tm, :] = u[:, sl * LANES:(sl + 1) * LANES]
                carry_ref[:, cols] = u_ref[sl, tm:tm + halo, :]

    def act(ci):
        gu_ref, vu_ref = u_refs[ci % 2]
        for sl in range(n_slabs):
            gc = slice(ci * cwid + sl * LANES, ci * cwid + (sl + 1) * LANES)
            vc = slice(d_ff + gc.start, d_ff + gc.stop)
            gb, gw = cb_ref[:, gc], [cw_ref[j:j + 1, gc] for j in range(FFN_CONV)]
            vb, vw = 0.5 * cb_ref[:, vc], [0.5 * cw_ref[j:j + 1, vc] for j in range(FFN_CONV)]
            for r0 in range(0, tm, FF_ROWS):
                g, v = gb, vb
                for j in range(FFN_CONV):
                    rows = slice(r0 + halo - (FFN_CONV - 1) + j, r0 + halo - (FFN_CONV - 1) + j + FF_ROWS)
                    g = g + gu_ref[sl, rows, :] * gw[j]
                    v = v + vu_ref[sl, rows, :] * vw[j]
                t = jnp.tanh(g * (GELU_C0 + GELU_C1 * (g * g)))
                h_ref[r0:r0 + FF_ROWS, gc] = ((g * v) * (1.0 + t)).astype(bf16)

    up(0)
    for ci in range(n_chunks - 1):
        up(ci + 1)
        act(ci)
    top, bot = slice(0, tm // 2), slice(tm // 2, tm)
    k1 = (n_chunks - 1) * cwid
    h_top = jnp.dot(h_ref[top, :k1], wdn_ref[:k1, :], preferred_element_type=f32)
    act(n_chunks - 1)
    h_top = h_top + jnp.dot(h_ref[top, k1:], wdn_ref[k1:, :], preferred_element_type=f32)
    h_bot = jnp.dot(h_ref[bot, :], wdn_ref[...], preferred_element_type=f32)
    out_ref[0, top, :] = _layer_norm(ALPHA * x_ref[0, top, :] + h_top, g_ref[...], b_ref[...])
    out_ref[0, bot, :] = _layer_norm(ALPHA * x_ref[0, bot, :] + h_bot, g_ref[...], b_ref[...])


def _ffn(x, wup, cw, cb, wdn, g, b):
    B, S, D = x.shape
    tm = min(TM_FFN, S)
    cwid = FF_CHUNK
    assert wdn.shape[0] % cwid == 0 and wup.shape[1] == 2 * wdn.shape[0]
    tok = pl.BlockSpec((1, tm, D), lambda b_, s: (b_, s, 0))
    return pl.pallas_call(
        _ffn_kernel,
        grid=(B, S // tm),
        in_specs=[tok] + [_const_spec(a) for a in (wup, cw, cb, wdn, g, b)],
        out_specs=tok,
        out_shape=jax.ShapeDtypeStruct((B, S, D), f32),
        scratch_shapes=[pltpu.VMEM((SUBLANES, wup.shape[1]), f32),
                        pltpu.VMEM((tm, D), bf16),
                        ]
                       + [pltpu.VMEM((cwid // LANES, tm + SUBLANES, LANES), f32)] * 4
                       + [pltpu.VMEM((tm, wdn.shape[0]), bf16)],
        compiler_params=_params(2),
        name="ffn",
    )(x, *map(_arr, (wup, cw, cb, wdn, g, b)))


def _prep_inproj(w_in):
    w = w_in.astype(bf16)
    depth, D = w.shape[:2]
    o = 4 * ML_WIDTH
    gates = w[..., o:o + 2 * ML_HEADS]
    o += 2 * ML_HEADS
    swq = w[..., o:o + SWA_WIDTH].reshape(depth, D, SWA_KV_HEADS, SWA_GROUP, SWA_HEAD_DIM)
    swq = swq.transpose(0, 1, 3, 2, 4).reshape(depth, D, SWA_WIDTH)
    o += SWA_WIDTH
    kv = w[..., o:]
    used = SWA_WIDTH + kv.shape[-1] + 2 * ML_HEADS
    pad = jnp.zeros((depth, D, -used % (2 * LANES)), bf16)
    return (w[..., :2 * ML_WIDTH], w[..., 2 * ML_WIDTH:4 * ML_WIDTH],
            jnp.concatenate([swq, kv, gates, pad], axis=-1))


def _pair_rows(w_rows):
    depth, _, n = w_rows.shape
    w_rows = w_rows.reshape(depth, SWA_KV_HEADS, SWA_GROUP, SWA_HEAD_DIM, n)
    return w_rows.transpose(0, 2, 1, 3, 4).reshape(depth, SWA_WIDTH, n)


def kernel(x, mem, rel_bias, w_in, ml_conv_w, ml_conv_b, ml_i_bias, ml_f_bias, ml_norm_g, swa_sinks, w_out, ln1_g, ln1_b, xa_wq, xa_wkv, xa_wo, ln2_g, ln2_b, ffn_w_up, ffn_conv_w, ffn_conv_b, ffn_w_down, ln3_g, ln3_b):
    depth = w_in.shape[0]
    rows = lambda a: a.reshape(depth, 1, -1)
    wqk, wvo, wswg = _prep_inproj(w_in)
    w_out_b = w_out.astype(bf16)
    wo_ml, wo_sw = w_out_b[:, :ML_WIDTH], _pair_rows(w_out_b[:, ML_WIDTH:])
    wq, wkv, wo = xa_wq.astype(bf16), xa_wkv.astype(bf16), xa_wo.astype(bf16)
    wup, wdn = ffn_w_up.astype(bf16), ffn_w_down.astype(bf16)
    gbias = jnp.concatenate([ml_i_bias, ml_f_bias], axis=1).reshape(depth, 2 * ML_HEADS, 1)
    small = [rows(a) for a in (ml_conv_b, ml_norm_g, ln1_g, ln1_b, ln2_g, ln2_b, ffn_conv_b, ln3_g, ln3_b)]
    bias = _swa_bias(rel_bias)
    for l in range(depth):
        at = lambda a: _Layer(a, l)
        conv_b, norm_g, g1, b1, g2, b2, ffn_cb, g3, b3 = map(at, small)
        qk, vo, gates, sw = _inproj(x, at(wqk), at(wvo), at(wswg), at(ml_conv_w), conv_b)
        hml = _mlstm(qk, vo, gates, at(gbias), norm_g)
        hsw = _swa(sw, bias, swa_sinks, l)
        k, v = _memkv(mem, at(wkv))
        x = _mix(hml, hsw, x, at(wo_ml), at(wo_sw), g1, b1, at(wq), k, v, at(wo), g2, b2)
        x = _ffn(x, at(wup), at(ffn_conv_w), ffn_cb, at(wdn), g3, b3)
    return x
```

```python
import functools
import math

import numpy as np
import jax
import jax.numpy as jnp
from jax import lax
from jax.experimental import pallas as pl
from jax.experimental.pallas import tpu as pltpu

f32 = jnp.float32
bf16 = jnp.bfloat16

ML_HEADS = 4
ML_HEAD_DIM = 128
ML_WIDTH = ML_HEADS * ML_HEAD_DIM
ML_CONV = 4
SWA_HEADS = 8
SWA_KV_HEADS = 2
SWA_GROUP = SWA_HEADS // SWA_KV_HEADS
SWA_HEAD_DIM = 64
SWA_WIDTH = SWA_HEADS * SWA_HEAD_DIM
WINDOW = 128
BLOCK = 128
REL_BUCKETS = 32
REL_MAX_DIST = 128
XA_HEADS = 4
FFN_CONV = 3
DEPTH = 2
ALPHA = (2.0 * DEPTH) ** 0.25
EPS = 1e-5

LANES = 128
SUBLANES = 8
TM_PROJ = 1024
PROJ_ROWS = 128
ML_CHUNK = 512
TQ_SWA = 512
TM_MIX = 1024
MIX_STREAMS = 4
TM_FFN = 512
FF_CHUNK = 256
FF_ROWS = 64
VMEM_LIMIT = 56 * 1024 * 1024
NEG_BIG = -1e30


def _params(n_axes, flags=None):
    return pltpu.CompilerParams(
        dimension_semantics=("arbitrary",) * n_axes, vmem_limit_bytes=VMEM_LIMIT, flags=flags)


class _Layer:
    def __init__(self, stacked, l):
        self.stacked, self.l = stacked, l

    @property
    def shape(self):
        return self.stacked.shape[1:]


def _arr(op):
    return op.stacked if isinstance(op, _Layer) else op


def _const_spec(op):
    nd = len(op.shape)
    if isinstance(op, _Layer):
        l = op.l
        return pl.BlockSpec((None,) + tuple(op.shape), lambda *_: (l,) + (0,) * nd,
                            pipeline_mode=pl.Buffered(1))
    return pl.BlockSpec(op.shape, lambda *_: (0,) * nd, pipeline_mode=pl.Buffered(1))


def _layer_norm(z, g, b):
    mu = jnp.mean(z, axis=-1, keepdims=True)
    zc = z - mu
    var = jnp.mean(zc * zc, axis=-1, keepdims=True)
    return zc * lax.rsqrt(var + EPS) * g + b


def _inproj_kernel(x_ref, wqk_ref, wvo_ref, wswg_ref, cw_ref, cb_ref,
                   qk_ref, vot_ref, g_ref, sw_ref, xb_ref, p_ref):
    tm = x_ref.shape[1]
    n_slabs = p_ref.shape[0]
    halo = SUBLANES

    @pl.when(pl.program_id(1) == 0)
    def _():
        p_ref[:, 0:halo, :] = jnp.zeros((n_slabs, halo, LANES), f32)

    xb_ref[...] = x_ref[0].astype(bf16)
    acc = jnp.dot(xb_ref[...], wqk_ref[...], preferred_element_type=f32)
    for sl in range(n_slabs):
        p_ref[sl, halo:halo + tm, :] = acc[:, sl * LANES:(sl + 1) * LANES]

    acc_vo = jnp.dot(xb_ref[...], wvo_ref[...], preferred_element_type=f32)
    for gi in range(acc_vo.shape[1] // LANES):
        lanes = slice(gi * LANES, (gi + 1) * LANES)
        vot_ref[0, lanes, :] = acc_vo[:, lanes].T.astype(bf16)
    accs = jnp.dot(xb_ref[...], wswg_ref[...], preferred_element_type=f32)
    sw_w = sw_ref.shape[2]
    sw_ref[0] = accs[:, :sw_w].astype(bf16)
    g_ref[0] = accs[:, sw_w:sw_w + LANES].T[0:SUBLANES, :]

    for sl in range(n_slabs):
        lanes = slice(sl * LANES, (sl + 1) * LANES)
        for r0 in range(0, tm, PROJ_ROWS):
            y = cb_ref[:, lanes] + p_ref[sl, r0 + 5:r0 + 5 + PROJ_ROWS, :] * cw_ref[0:1, lanes]
            y = y + p_ref[sl, r0 + 6:r0 + 6 + PROJ_ROWS, :] * cw_ref[1:2, lanes]
            y = y + p_ref[sl, r0 + 7:r0 + 7 + PROJ_ROWS, :] * cw_ref[2:3, lanes]
            y = y + p_ref[sl, r0 + 8:r0 + 8 + PROJ_ROWS, :] * cw_ref[3:4, lanes]
            qk_ref[0, r0:r0 + PROJ_ROWS, lanes] = (y * jax.nn.sigmoid(y)).astype(bf16)
        p_ref[sl, 0:halo, :] = p_ref[sl, tm:tm + halo, :]


def _inproj(x, wqk, wvo, wswg, cw, cb):
    B, S, D = x.shape
    tm = min(TM_PROJ, S)
    grid = (B, S // tm)
    sw_w = SWA_WIDTH + 4 * SWA_HEAD_DIM
    tok = lambda w: pl.BlockSpec((1, tm, w), lambda b, s: (b, s, 0))
    return pl.pallas_call(
        _inproj_kernel,
        grid=grid,
        in_specs=[tok(D)] + [_const_spec(a) for a in (wqk, wvo, wswg, cw, cb)],
        out_specs=[tok(wqk.shape[1]),
                   pl.BlockSpec((1, wvo.shape[1], tm), lambda b, s: (b, 0, s)),
                   pl.BlockSpec((1, SUBLANES, tm), lambda b, s: (b, 0, s)),
                   tok(sw_w)],
        out_shape=[jax.ShapeDtypeStruct((B, S, wqk.shape[1]), bf16),
                   jax.ShapeDtypeStruct((B, wvo.shape[1], S), bf16),
                   jax.ShapeDtypeStruct((B, SUBLANES, S), f32),
                   jax.ShapeDtypeStruct((B, S, sw_w), bf16)],
        scratch_shapes=[pltpu.VMEM((tm, D), bf16),
                        pltpu.VMEM((wqk.shape[1] // LANES, tm + SUBLANES, LANES), f32)],
        compiler_params=_params(2),
        name="inproj",
    )(x, *map(_arr, (wqk, wvo, wswg, cw, cb)))


C_ROW, CM_ROW, B_ROW = 0, SUBLANES, 2 * SUBLANES


def _scan_chunks(x, op, fill, chunk):
    pos = lax.broadcasted_iota(jnp.int32, x.shape, 1) % chunk
    sh = 1
    while sh < chunk:
        x = op(x, jnp.where(pos >= sh, pltpu.roll(x, sh, 1), fill))
        sh *= 2
    return x


def _mlgate_kernel(g_ref, gb_ref, rows_ref, cols_ref, *, chunk):
    g = g_ref[0] + gb_ref[...]
    ig = jnp.concatenate([g[0:4], g[0:4]], axis=0)
    fg = jnp.concatenate([g[4:8], g[4:8]], axis=0)
    lf = jnp.minimum(fg, 0.0) - jnp.log1p(jnp.exp(-jnp.abs(fg)))
    b = _scan_chunks(lf, jnp.add, 0.0, chunk)
    c = ig - b
    cm = _scan_chunks(c, jnp.maximum, -jnp.inf, chunk)
    rows_ref[0] = jnp.concatenate([c, cm, b], axis=0)
    stack = jnp.concatenate([c, jnp.zeros((LANES - SUBLANES, c.shape[1]), f32)], axis=0)
    cols_ref[0] = stack.T


def _mlgate(gates, gbias, chunk):
    B, _, S = gates.shape
    return pl.pallas_call(
        functools.partial(_mlgate_kernel, chunk=chunk),
        grid=(B,),
        in_specs=[pl.BlockSpec((1, SUBLANES, S), lambda b: (b, 0, 0)), _const_spec(gbias)],
        out_specs=[pl.BlockSpec((1, 3 * SUBLANES, S), lambda b: (b, 0, 0)),
                   pl.BlockSpec((1, S, LANES), lambda b: (b, 0, 0))],
        out_shape=[jax.ShapeDtypeStruct((B, 3 * SUBLANES, S), f32),
                   jax.ShapeDtypeStruct((B, S, LANES), f32)],
        compiler_params=_params(1),
        name="mlgate",
    )(gates, _arr(gbias))


def _mlstm_kernel(qk_ref, vot_ref, rows_ref, cols_ref, ngt_ref, out_ref, ct_ref, m_ref):
    L = qk_ref.shape[1]
    dh = ML_HEAD_DIM
    scale = dh ** -0.5
    reps = L // dh
    nt = (((1,), (1,)), ((), ()))

    @pl.when(pl.program_id(1) == 0)
    def _():
        ct_ref[...] = jnp.zeros(ct_ref.shape, f32)
        m_ref[...] = jnp.zeros(m_ref.shape, f32)

    half = L // 2
    src = lax.broadcasted_iota(jnp.int32, (L, half), 0)
    tgt = lax.broadcasted_iota(jnp.int32, (L, half), 1)
    causal = [(src <= tgt)[:half], src <= tgt + half]
    ones_t = jnp.ones((dh, L), bf16)

    heads = range(ML_HEADS)
    hs = lambda h: slice(h * dh, (h + 1) * dh)
    ks = lambda h: slice(ML_WIDTH + h * dh, ML_WIDTH + (h + 1) * dh)
    m_all = m_ref[...]

    ct = [ct_ref[h] for h in heads]
    tcols = [slice(0, half), slice(half, L)]
    srows = [slice(0, half), slice(0, L)]
    s_t = [[lax.dot_general(qk_ref[0, srows[j], ks(h)], qk_ref[0, tcols[j], hs(h)], nt,
                            preferred_element_type=f32) for j in range(2)]
           for h in heads]
    inter = [lax.dot_general(ct[h].astype(bf16), qk_ref[0, :, hs(h)], nt, preferred_element_type=f32)
             for h in heads]

    c_row, cm_row, v_aug, intra = [], [], [], []
    for h in heads:
        c_row.append(rows_ref[0, C_ROW + h:C_ROW + h + 1, :])
        cm_row.append(rows_ref[0, CM_ROW + h:CM_ROW + h + 1, :])
        c_b = jnp.broadcast_to(cols_ref[0, :, h:h + 1], (L, dh))
        c_wide = jnp.concatenate([c_b] * (half // dh), axis=1) + math.log(scale)
        v_aug.append(jnp.concatenate([vot_ref[0, hs(h), :], ones_t], axis=0))
        parts = []
        for j in range(2):
            p = jnp.where(causal[j], jnp.exp(c_wide[srows[j]] - cm_row[h][:, tcols[j]]), 0.0)
            s_mat = (s_t[h][j] * p).astype(bf16)
            parts.append(jnp.dot(v_aug[h][:, srows[j]], s_mat, preferred_element_type=f32))
        intra.append(jnp.concatenate(parts, axis=1))

    m_prev, big_m, m_last, vw = [], [], [], []
    for h in heads:
        m_prev.append(m_all[h:h + 1, 0:1])
        big_m.append(jnp.maximum(m_prev[h], cm_row[h]))
        m_last.append(big_m[h][:, L - 1:L])
        wk = jnp.exp(c_row[h] - m_last[h]) * scale
        vw.append((v_aug[h].astype(f32) * wk).astype(bf16))
    upd = [jnp.dot(vw[h], qk_ref[0, :, ks(h)], preferred_element_type=f32) for h in heads]

    m_new = []
    for h in heads:
        b_row = rows_ref[0, B_ROW + h:B_ROW + h + 1, :]
        w_inter = jnp.exp(m_prev[h] - big_m[h])
        w_intra = jnp.exp(cm_row[h] - big_m[h])
        nd = w_inter * inter[h] + w_intra * intra[h]
        den = nd[dh:dh + 1, :]
        inv = 1.0 / jnp.maximum(jnp.abs(den), jnp.exp(-(b_row + big_m[h])))
        hh = nd[:dh, :] * inv
        mu = jnp.mean(hh, axis=0, keepdims=True)
        hc = hh - mu
        var = jnp.mean(hc * hc, axis=0, keepdims=True)
        ng = jnp.concatenate([ngt_ref[hs(h), :]] * reps, axis=1)
        hn = hc * lax.rsqrt(var + EPS) * ng
        gate = jax.nn.sigmoid(vot_ref[0, ks(h), :].astype(f32))
        out_ref[0, :, hs(h)] = (gate * hn).T.astype(bf16)
        ct_ref[h] = jnp.exp(m_prev[h] - m_last[h]) * ct[h] + upd[h]
        m_new.append(jnp.broadcast_to(b_row[:, L - 1:L] + m_last[h], (1, LANES)))
    m_ref[0:ML_HEADS, :] = jnp.concatenate(m_new, axis=0)


def _mlstm(qk, vot, gates, gbias, norm_gt):
    B, S, _ = qk.shape
    L = min(ML_CHUNK, S)
    rows, cols = _mlgate(gates, gbias, L)
    grid = (B, S // L)
    return pl.pallas_call(
        _mlstm_kernel,
        grid=grid,
        in_specs=[pl.BlockSpec((1, L, 2 * ML_WIDTH), lambda b, s: (b, s, 0)),
                  pl.BlockSpec((1, 2 * ML_WIDTH, L), lambda b, s: (b, 0, s)),
                  pl.BlockSpec((1, 3 * SUBLANES, L), lambda b, s: (b, 0, s)),
                  pl.BlockSpec((1, L, LANES), lambda b, s: (b, s, 0)),
                  _const_spec(norm_gt)],
        out_specs=pl.BlockSpec((1, L, ML_WIDTH), lambda b, s: (b, s, 0)),
        out_shape=jax.ShapeDtypeStruct((B, S, ML_WIDTH), bf16),
        scratch_shapes=[pltpu.VMEM((ML_HEADS, 2 * ML_HEAD_DIM, ML_HEAD_DIM), f32),
                        pltpu.VMEM((SUBLANES, LANES), f32)],
        compiler_params=_params(2),
        name="mlstm",
    )(qk, vot, rows, cols, _arr(norm_gt))


def _t5_bucket_table():
    assert WINDOW == BLOCK
    r = np.arange(BLOCK)[:, None]
    c = np.arange(BLOCK)[None, :]
    n = np.where(c > r, BLOCK + r - c, r - c)
    max_exact = REL_BUCKETS // 2
    nf = np.maximum(n, 1).astype(np.float32)
    large = max_exact + (np.log(nf / np.float32(max_exact)) / np.float32(math.log(REL_MAX_DIST / max_exact))
                         * np.float32(REL_BUCKETS - max_exact)).astype(np.int32)
    large = np.minimum(large, REL_BUCKETS - 1)
    return np.where(n < max_exact, n, large).astype(np.int32)


def _bias_kernel(bucket_ref, rb_ref, out_ref):
    bucket = bucket_ref[...]
    for h in range(SWA_HEADS):
        acc = jnp.zeros(bucket.shape, f32)
        for i in range(REL_BUCKETS):
            acc = jnp.where(bucket == i, rb_ref[i, h], acc)
        out_ref[h] = acc


def _swa_bias(rel_bias):
    bucket = jnp.asarray(_t5_bucket_table())
    return pl.pallas_call(
        _bias_kernel,
        in_specs=[pl.BlockSpec(memory_space=pltpu.VMEM), pl.BlockSpec(memory_space=pltpu.SMEM)],
        out_specs=pl.BlockSpec(memory_space=pltpu.VMEM),
        out_shape=jax.ShapeDtypeStruct((SWA_HEADS, BLOCK, BLOCK), f32),
        name="swa_bias",
    )(bucket, rel_bias)


def _swa_kernel(sw_ref, prev_ref, bias_ref, sink_ref, out_ref, *, layer):
    tq = sw_ref.shape[1]
    scale = SWA_HEAD_DIM ** -0.5
    pair = 2 * SWA_HEAD_DIM
    kc = slice(SWA_WIDTH, SWA_WIDTH + pair)
    vc = slice(SWA_WIDTH + pair, SWA_WIDTH + 2 * pair)
    first_tile = pl.program_id(1) == 0
    lo = lax.broadcasted_iota(jnp.int32, (BLOCK, pair), 1) < SWA_HEAD_DIM
    use_prev = (lax.broadcasted_iota(jnp.int32, (BLOCK, BLOCK), 1)
                > lax.broadcasted_iota(jnp.int32, (BLOCK, BLOCK), 0))
    zero_p = jnp.zeros((BLOCK, BLOCK), bf16)

    for j in range(tq // BLOCK):
        rows = slice(j * BLOCK, (j + 1) * BLOCK)
        prows = slice((j - 1) * BLOCK, j * BLOCK)
        if j == 0:
            k_prev, v_prev = prev_ref[0, :, 0:pair], prev_ref[0, :, pair:2 * pair]
        else:
            k_prev, v_prev = sw_ref[0, prows, kc], sw_ref[0, prows, vc]
        kband = jnp.concatenate([k_prev, sw_ref[0, rows, kc]], axis=0)
        vband = jnp.concatenate([v_prev, sw_ref[0, rows, vc]], axis=0)
        qs, heads = [], []
        for p in range(SWA_GROUP):
            qp = sw_ref[0, rows, p * pair:(p + 1) * pair] * scale
            zero = jnp.zeros_like(qp)
            qs += [jnp.where(lo, qp, zero), jnp.where(lo, zero, qp)]
            heads += [p, p + SWA_GROUP]
        logits = lax.dot_general(jnp.concatenate(qs, axis=0), kband, (((1,), (1,)), ((), ())),
                                 preferred_element_type=f32)
        ps, dens = [], []
        for n, head in enumerate(heads):
            lg2 = logits[n * BLOCK:(n + 1) * BLOCK]
            lg = jnp.where(use_prev, lg2[:, :BLOCK], lg2[:, BLOCK:]) + bias_ref[head]
            if j == 0:
                lg = jnp.where(jnp.logical_and(first_tile, use_prev), NEG_BIG, lg)
            sink = sink_ref[layer, head]
            mx = jnp.maximum(jnp.max(lg, axis=-1, keepdims=True), sink)
            pr = jnp.exp(lg - mx)
            dens.append(jnp.sum(pr, axis=-1, keepdims=True) + jnp.exp(sink - mx))
            pb = pr.astype(bf16)
            ps.append(jnp.concatenate([jnp.where(use_prev, pb, zero_p),
                                       jnp.where(use_prev, zero_p, pb)], axis=1))
        pv = jnp.dot(jnp.concatenate(ps, axis=0), vband, preferred_element_type=f32)
        o = [pv[n * BLOCK:(n + 1) * BLOCK] / dens[n] for n in range(len(heads))]
        for p in range(SWA_GROUP):
            out_ref[0, rows, p * pair:(p + 1) * pair] = jnp.where(lo, o[2 * p], o[2 * p + 1]).astype(bf16)


def _swa(sw, bias, sinks, layer):
    B, S, W = sw.shape
    tq = min(TQ_SWA, S)
    nblk = tq // BLOCK
    kvw = W - SWA_WIDTH
    assert kvw == 4 * SWA_HEAD_DIM and SWA_WIDTH % kvw == 0, "k/v group = [k0|k1|v0|v1]"
    return pl.pallas_call(
        functools.partial(_swa_kernel, layer=layer),
        grid=(B, S // tq),
        in_specs=[pl.BlockSpec((1, tq, W), lambda b, s: (b, s, 0)),
                  pl.BlockSpec((1, BLOCK, kvw),
                               lambda b, s: (b, jnp.maximum(s * nblk - 1, 0), SWA_WIDTH // kvw)),
                  _const_spec(bias),
                  pl.BlockSpec(memory_space=pltpu.SMEM)],
        out_specs=pl.BlockSpec((1, tq, SWA_WIDTH), lambda b, s: (b, s, 0)),
        out_shape=jax.ShapeDtypeStruct((B, S, SWA_WIDTH), bf16),
        compiler_params=_params(2),
        name="swa",
    )(sw, sw, bias, sinks)


def _memkv_kernel(mem_ref, wkv_ref, k_ref, v_ref):
    d = k_ref.shape[2]
    kv = jnp.dot(mem_ref[0].astype(bf16), wkv_ref[...], preferred_element_type=f32)
    k_ref[0] = kv[:, :d].astype(bf16)
    v_ref[0] = kv[:, d:].astype(bf16)


def _memkv(mem, wkv):
    B, M, D = mem.shape
    blk = pl.BlockSpec((1, M, D), lambda b: (b, 0, 0))
    return pl.pallas_call(
        _memkv_kernel,
        grid=(B,),
        in_specs=[blk, _const_spec(wkv)],
        out_specs=[blk, blk],
        out_shape=[jax.ShapeDtypeStruct((B, M, D), bf16)] * 2,
        compiler_params=_params(1),
        name="memkv",
    )(mem, _arr(wkv))


def _mix_kernel(hml_ref, hsw_ref, x_ref, woml_ref, wosw_ref, g1_ref, b1_ref,
                wq_ref, k_ref, v_ref, wo_ref, g2_ref, b2_ref, out_ref):
    tm, d = x_ref.shape[1], x_ref.shape[2]
    dh = d // XA_HEADS
    rows = [slice(i * tm // MIX_STREAMS, (i + 1) * tm // MIX_STREAMS) for i in range(MIX_STREAMS)]
    streams = range(MIX_STREAMS)
    cs = lambda a: slice(a * dh, (a + 1) * dh)

    h = [jnp.dot(hml_ref[0, r, :], woml_ref[...], preferred_element_type=f32)
         + jnp.dot(hsw_ref[0, r, :], wosw_ref[...], preferred_element_type=f32) for r in rows]
    x1, q = [], []
    for i in streams:
        x1.append(_layer_norm(ALPHA * x_ref[0, rows[i], :] + h[i], g1_ref[...], b1_ref[...]))
        q.append(jnp.dot(x1[i].astype(bf16), wq_ref[...], preferred_element_type=f32).astype(bf16))

    items = [(a, i) for a in range(XA_HEADS) for i in streams]
    lag = 2
    prob, den, outs = {}, {}, {}

    def logits(a, i):
        lg = lax.dot_general(q[i][:, cs(a)], k_ref[0, :, cs(a)], (((1,), (1,)), ((), ())),
                             preferred_element_type=f32) * (dh ** -0.5)
        p = jnp.exp(lg - jnp.max(lg, axis=-1, keepdims=True))
        den[a, i] = jnp.sum(p, axis=-1, keepdims=True)
        prob[a, i] = p.astype(bf16)

    def values(a, i):
        o = jnp.dot(prob[a, i], v_ref[0, :, cs(a)], preferred_element_type=f32) / den[a, i]
        outs[a, i] = o.astype(bf16)

    for n, item in enumerate(items):
        logits(*item)
        if n >= lag:
            values(*items[n - lag])
    for item in items[-lag:]:
        values(*item)

    h2 = [jnp.dot(jnp.concatenate([outs[a, i] for a in range(XA_HEADS)], axis=1), wo_ref[...],
                  preferred_element_type=f32) for i in streams]
    for i in streams:
        out_ref[0, rows[i], :] = _layer_norm(ALPHA * x1[i] + h2[i], g2_ref[...], b2_ref[...])


def _mix(hml, hsw, x, woml, wosw, g1, b1, wq, k, v, wo, g2, b2):
    B, S, D = x.shape
    tm = min(TM_MIX, S)
    M = k.shape[1]
    tok = lambda w: pl.BlockSpec((1, tm, w), lambda b, s: (b, s, 0))
    memblk = pl.BlockSpec((1, M, D), lambda b, s: (b, 0, 0))
    consts = [woml, wosw, g1, b1, wq]
    return pl.pallas_call(
        _mix_kernel,
        grid=(B, S // tm),
        in_specs=[tok(hml.shape[2]), tok(hsw.shape[2]), tok(D)]
                 + [_const_spec(a) for a in consts]
                 + [memblk, memblk] + [_const_spec(a) for a in (wo, g2, b2)],
        out_specs=tok(D),
        out_shape=jax.ShapeDtypeStruct((B, S, D), f32),
        compiler_params=_params(2),
        name="mix",
    )(hml, hsw, x, *map(_arr, consts), k, v, *map(_arr, (wo, g2, b2)))


GELU_C0 = math.sqrt(2.0 / math.pi)
GELU_C1 = 0.044715 * GELU_C0


def _ffn_kernel(x_ref, wup_ref, cw_ref, cb_ref, wdn_ref, g_ref, b_ref, out_ref,
                carry_ref, xb_ref, u00_ref, u01_ref, u10_ref, u11_ref, h_ref):
    tm = x_ref.shape[1]
    d_ff = wdn_ref.shape[0]
    cwid = FF_CHUNK
    n_chunks = d_ff // cwid
    n_slabs = cwid // LANES
    halo = SUBLANES
    u_refs = ((u00_ref, u01_ref), (u10_ref, u11_ref))

    @pl.when(pl.program_id(1) == 0)
    def _():
        carry_ref[...] = jnp.zeros(carry_ref.shape, f32)

    xb_ref[...] = x_ref[0].astype(bf16)

    def up(ci):
        for u_ref, c0 in zip(u_refs[ci % 2], (ci * cwid, d_ff + ci * cwid)):
            u = jnp.dot(xb_ref[...], wup_ref[:, c0:c0 + cwid], preferred_element_type=f32)
            for sl in range(n_slabs):
                cols = slice(c0 + sl * LANES, c0 + (sl + 1) * LANES)
                u_ref[sl, 0:halo, :] = carry_ref[:, cols]
                u_ref[sl, halo:halo + tm, :] = u[:, sl * LANES:(sl + 1) * LANES]
                carry_ref[:, cols] = u_ref[sl, tm:tm + halo, :]

    def act(ci):
        gu_ref, vu_ref = u_refs[ci % 2]
        for sl in range(n_slabs):
            gc = slice(ci * cwid + sl * LANES, ci * cwid + (sl + 1) * LANES)
            vc = slice(d_ff + gc.start, d_ff + gc.stop)
            gb, gw = cb_ref[:, gc], [cw_ref[j:j + 1, gc] for j in range(FFN_CONV)]
            vb, vw = 0.5 * cb_ref[:, vc], [0.5 * cw_ref[j:j + 1, vc] for j in range(FFN_CONV)]
            for r0 in range(0, tm, FF_ROWS):
                g, v = gb, vb
                for j in range(FFN_CONV):
                    rows = slice(r0 + halo - (FFN_CONV - 1) + j, r0 + halo - (FFN_CONV - 1) + j + FF_ROWS)
                    g = g + gu_ref[sl, rows, :] * gw[j]
                    v = v + vu_ref[sl, rows, :] * vw[j]
                t = jnp.tanh(g * (GELU_C0 + GELU_C1 * (g * g)))
                h_ref[r0:r0 + FF_ROWS, gc] = ((g * v) * (1.0 + t)).astype(bf16)

    up(0)
    for ci in range(n_chunks - 1):
        up(ci + 1)
        act(ci)
    top, bot = slice(0, tm // 2), slice(tm // 2, tm)
    k1 = (n_chunks - 1) * cwid
    h_top = jnp.dot(h_ref[top, :k1], wdn_ref[:k1, :], preferred_element_type=f32)
    act(n_chunks - 1)
    h_top = h_top + jnp.dot(h_ref[top, k1:], wdn_ref[k1:, :], preferred_element_type=f32)
    h_bot = jnp.dot(h_ref[bot, :], wdn_ref[...], preferred_element_type=f32)
    out_ref[0, top, :] = _layer_norm(ALPHA * x_ref[0, top, :] + h_top, g_ref[...], b_ref[...])
    out_ref[0, bot, :] = _layer_norm(ALPHA * x_ref[0, bot, :] + h_bot, g_ref[...], b_ref[...])


def _ffn(x, wup, cw, cb, wdn, g, b):
    B, S, D = x.shape
    tm = min(TM_FFN, S)
    cwid = FF_CHUNK
    assert wdn.shape[0] % cwid == 0 and wup.shape[1] == 2 * wdn.shape[0]
    tok = pl.BlockSpec((1, tm, D), lambda b_, s: (b_, s, 0))
    return pl.pallas_call(
        _ffn_kernel,
        grid=(B, S // tm),
        in_specs=[tok] + [_const_spec(a) for a in (wup, cw, cb, wdn, g, b)],
        out_specs=tok,
        out_shape=jax.ShapeDtypeStruct((B, S, D), f32),
        scratch_shapes=[pltpu.VMEM((SUBLANES, wup.shape[1]), f32),
                        pltpu.VMEM((tm, D), bf16),
                        ]
                       + [pltpu.VMEM((cwid // LANES, tm + SUBLANES, LANES), f32)] * 4
                       + [pltpu.VMEM((tm, wdn.shape[0]), bf16)],
        compiler_params=_params(2),
        name="ffn",
    )(x, *map(_arr, (wup, cw, cb, wdn, g, b)))


def _prep_inproj(w_in):
    w = w_in.astype(bf16)
    depth, D = w.shape[:2]
    o = 4 * ML_WIDTH
    gates = w[..., o:o + 2 * ML_HEADS]
    o += 2 * ML_HEADS
    swq = w[..., o:o + SWA_WIDTH].reshape(depth, D, SWA_KV_HEADS, SWA_GROUP, SWA_HEAD_DIM)
    swq = swq.transpose(0, 1, 3, 2, 4).reshape(depth, D, SWA_WIDTH)
    o += SWA_WIDTH
    kv = w[..., o:]
    used = SWA_WIDTH + kv.shape[-1] + 2 * ML_HEADS
    pad = jnp.zeros((depth, D, -used % (2 * LANES)), bf16)
    return (w[..., :2 * ML_WIDTH], w[..., 2 * ML_WIDTH:4 * ML_WIDTH],
            jnp.concatenate([swq, kv, gates, pad], axis=-1))


def _pair_rows(w_rows):
    depth, _, n = w_rows.shape
    w_rows = w_rows.reshape(depth, SWA_KV_HEADS, SWA_GROUP, SWA_HEAD_DIM, n)
    return w_rows.transpose(0, 2, 1, 3, 4).reshape(depth, SWA_WIDTH, n)


def kernel(x, mem, rel_bias, w_in, ml_conv_w, ml_conv_b, ml_i_bias, ml_f_bias, ml_norm_g, swa_sinks, w_out, ln1_g, ln1_b, xa_wq, xa_wkv, xa_wo, ln2_g, ln2_b, ffn_w_up, ffn_conv_w, ffn_conv_b, ffn_w_down, ln3_g, ln3_b):
    depth = w_in.shape[0]
    rows = lambda a: a.reshape(depth, 1, -1)
    wqk, wvo, wswg = _prep_inproj(w_in)
    w_out_b = w_out.astype(bf16)
    wo_ml, wo_sw = w_out_b[:, :ML_WIDTH], _pair_rows(w_out_b[:, ML_WIDTH:])
    wq, wkv, wo = xa_wq.astype(bf16), xa_wkv.astype(bf16), xa_wo.astype(bf16)
    wup, wdn = ffn_w_up.astype(bf16), ffn_w_down.astype(bf16)
    gbias = jnp.concatenate([ml_i_bias, ml_f_bias], axis=1).reshape(depth, 2 * ML_HEADS, 1)
    small = [rows(a) for a in (ml_conv_b, ln1_g, ln1_b, ln2_g, ln2_b, ffn_conv_b, ln3_g, ln3_b)]
    norm_gt = jnp.broadcast_to(ml_norm_g[:, :, None], ml_norm_g.shape + (LANES,))
    bias = _swa_bias(rel_bias)
    for l in range(depth):
        at = lambda a: _Layer(a, l)
        conv_b, g1, b1, g2, b2, ffn_cb, g3, b3 = map(at, small)
        qk, vot, gates, sw = _inproj(x, at(wqk), at(wvo), at(wswg), at(ml_conv_w), conv_b)
        hml = _mlstm(qk, vot, gates, at(gbias), at(norm_gt))
        hsw = _swa(sw, bias, swa_sinks, l)
        k, v = _memkv(mem, at(wkv))
        x = _mix(hml, hsw, x, at(wo_ml), at(wo_sw), g1, b1, at(wq), k, v, at(wo), g2, b2)
        x = _ffn(x, at(wup), at(ffn_conv_w), ffn_cb, at(wdn), g3, b3)
    return x
```

```python
import functools
import math

import numpy as np
import jax
import jax.numpy as jnp
from jax import lax
from jax.experimental import pallas as pl
from jax.experimental.pallas import tpu as pltpu

f32 = jnp.float32
bf16 = jnp.bfloat16

ML_HEADS = 4
ML_HEAD_DIM = 128
ML_WIDTH = ML_HEADS * ML_HEAD_DIM
ML_CONV = 4
SWA_HEADS = 8
SWA_KV_HEADS = 2
SWA_GROUP = SWA_HEADS // SWA_KV_HEADS
SWA_HEAD_DIM = 64
SWA_WIDTH = SWA_HEADS * SWA_HEAD_DIM
WINDOW = 128
BLOCK = 128
REL_BUCKETS = 32
REL_MAX_DIST = 128
XA_HEADS = 4
FFN_CONV = 3
DEPTH = 2
ALPHA = (2.0 * DEPTH) ** 0.25
EPS = 1e-5

LANES = 128
SUBLANES = 8
TM_PROJ = 1024
PROJ_ROWS = 128
PROJ_COLS = 256
ML_CHUNK = 512
MLG_BATCH = 4
TQ_SWA = 512
TM_MIX = 1024
MIX_STREAMS = 4
TM_FFN = 512
FF_CHUNK = 256
FF_ROWS = 64
VMEM_LIMIT = 56 * 1024 * 1024
NEG_BIG = -1e30


def _params(n_axes, flags=None):
    return pltpu.CompilerParams(
        dimension_semantics=("arbitrary",) * n_axes, vmem_limit_bytes=VMEM_LIMIT, flags=flags)


class _Layer:
    def __init__(self, stacked, l, axis=None, size=None, index=0):
        self.stacked, self.l, self.axis, self.size, self.index = stacked, l, axis, size, index

    @property
    def shape(self):
        shape = list(self.stacked.shape[1:])
        if self.axis is not None:
            shape[self.axis] = self.size
        return tuple(shape)

    @property
    def block_index(self):
        idx = [0] * (self.stacked.ndim - 1)
        if self.axis is not None:
            idx[self.axis] = self.index
        return (self.l,) + tuple(idx)


def _arr(op):
    return op.stacked if isinstance(op, _Layer) else op


def _const_spec(op):
    if isinstance(op, _Layer):
        index = op.block_index
        return pl.BlockSpec((None,) + op.shape, lambda *_: index, pipeline_mode=pl.Buffered(1))
    nd = len(op.shape)
    return pl.BlockSpec(op.shape, lambda *_: (0,) * nd, pipeline_mode=pl.Buffered(1))


def _layer_norm(z, g, b):
    mu = jnp.mean(z, axis=-1, keepdims=True)
    zc = z - mu
    var = jnp.mean(zc * zc, axis=-1, keepdims=True)
    return zc * lax.rsqrt(var + EPS) * g + b


def _inproj_kernel(x_ref, wqk_ref, wvo_ref, wswg_ref, cw_ref, cb_ref,
                   qk_ref, vot_ref, g_ref, sw_ref, xb_ref, p_ref):
    tm = x_ref.shape[1]
    n_slabs = p_ref.shape[0]
    halo = SUBLANES

    @pl.when(pl.program_id(1) == 0)
    def _():
        p_ref[:, 0:halo, :] = jnp.zeros((n_slabs, halo, LANES), f32)

    xb_ref[...] = x_ref[0].astype(bf16)

    def project(w_ref, c0):
        return jnp.dot(xb_ref[...], w_ref[:, c0:c0 + PROJ_COLS], preferred_element_type=f32)

    def qk_piece(c0):
        acc = project(wqk_ref, c0)
        for li in range(PROJ_COLS // LANES):
            sl = c0 // LANES + li
            lanes = slice(sl * LANES, (sl + 1) * LANES)
            p_ref[sl, halo:halo + tm, :] = acc[:, li * LANES:(li + 1) * LANES]
            taps = [cw_ref[j:j + 1, lanes] for j in range(ML_CONV)]
            for r0 in range(0, tm, PROJ_ROWS):
                y = cb_ref[:, lanes]
                for j in range(ML_CONV):
                    first = r0 + halo - (ML_CONV - 1) + j
                    y = y + p_ref[sl, first:first + PROJ_ROWS, :] * taps[j]
                hy = 0.5 * y
                qk_ref[0, r0:r0 + PROJ_ROWS, lanes] = (hy + hy * jnp.tanh(hy)).astype(bf16)
            p_ref[sl, 0:halo, :] = p_ref[sl, tm:tm + halo, :]

    def vo_piece(c0):
        acc = project(wvo_ref, c0)
        for li in range(PROJ_COLS // LANES):
            lanes = slice(c0 + li * LANES, c0 + (li + 1) * LANES)
            vot_ref[0, lanes, :] = acc[:, li * LANES:(li + 1) * LANES].T.astype(bf16)

    def swg_piece(c0):
        sw_w = sw_ref.shape[2]
        acc = project(wswg_ref, c0)
        if c0 + PROJ_COLS <= sw_w:
            sw_ref[0, :, c0:c0 + PROJ_COLS] = acc.astype(bf16)
        else:
            g_ref[0] = acc[:, sw_w - c0:sw_w - c0 + LANES].T[0:SUBLANES, :]

    others = ([functools.partial(vo_piece, c0) for c0 in range(0, wvo_ref.shape[1], PROJ_COLS)]
              + [functools.partial(swg_piece, c0) for c0 in range(0, wswg_ref.shape[1], PROJ_COLS)])
    n_qk = wqk_ref.shape[1] // PROJ_COLS
    per_qk = -(-len(others) // n_qk)
    for i in range(n_qk):
        qk_piece(i * PROJ_COLS)
        for piece in others[i * per_qk:(i + 1) * per_qk]:
            piece()


def _inproj(x, wqk, wvo, wswg, cw, cb):
    B, S, D = x.shape
    tm = min(TM_PROJ, S)
    grid = (B, S // tm)
    sw_w = SWA_WIDTH + 4 * SWA_HEAD_DIM
    tok = lambda w: pl.BlockSpec((1, tm, w), lambda b, s: (b, s, 0))
    return pl.pallas_call(
        _inproj_kernel,
        grid=grid,
        in_specs=[tok(D)] + [_const_spec(a) for a in (wqk, wvo, wswg, cw, cb)],
        out_specs=[tok(wqk.shape[1]),
                   pl.BlockSpec((1, wvo.shape[1], tm), lambda b, s: (b, 0, s)),
                   pl.BlockSpec((1, SUBLANES, tm), lambda b, s: (b, 0, s)),
                   tok(sw_w)],
        out_shape=[jax.ShapeDtypeStruct((B, S, wqk.shape[1]), bf16),
                   jax.ShapeDtypeStruct((B, wvo.shape[1], S), bf16),
                   jax.ShapeDtypeStruct((B, SUBLANES, S), f32),
                   jax.ShapeDtypeStruct((B, S, sw_w), bf16)],
        scratch_shapes=[pltpu.VMEM((tm, D), bf16),
                        pltpu.VMEM((wqk.shape[1] // LANES, tm + SUBLANES, LANES), f32)],
        compiler_params=_params(2),
        name="inproj",
    )(x, *map(_arr, (wqk, wvo, wswg, cw, cb)))


C_ROW, CM_ROW, B_ROW = 0, SUBLANES, 2 * SUBLANES


def _scan_chunks(x, op, fill, chunk):
    pos = lax.broadcasted_iota(jnp.int32, x.shape, 1) % chunk
    sh = 1
    while sh < chunk:
        x = op(x, jnp.where(pos >= sh, pltpu.roll(x, sh, 1), fill))
        sh *= 2
    return x


def _mlgate_kernel(g_ref, gb_ref, rows_ref, cols_ref, *, chunk):
    for i in range(g_ref.shape[0]):
        g = g_ref[i] + gb_ref[...]
        ig = jnp.concatenate([g[0:4], g[0:4]], axis=0)
        fg = jnp.concatenate([g[4:8], g[4:8]], axis=0)
        lf = jnp.minimum(fg, 0.0) - jnp.log1p(jnp.exp(-jnp.abs(fg)))
        b = _scan_chunks(lf, jnp.add, 0.0, chunk)
        c = ig - b
        cm = _scan_chunks(c, jnp.maximum, -jnp.inf, chunk)
        rows_ref[i] = jnp.concatenate([c, cm, b], axis=0)
        stack = jnp.concatenate([c, jnp.zeros((LANES - SUBLANES, c.shape[1]), f32)], axis=0)
        cols_ref[i] = stack.T


def _mlgate(gates, gbias, chunk):
    B, _, S = gates.shape
    nb = math.gcd(B, MLG_BATCH)
    return pl.pallas_call(
        functools.partial(_mlgate_kernel, chunk=chunk),
        grid=(B // nb,),
        in_specs=[pl.BlockSpec((nb, SUBLANES, S), lambda b: (b, 0, 0)), _const_spec(gbias)],
        out_specs=[pl.BlockSpec((nb, 3 * SUBLANES, S), lambda b: (b, 0, 0)),
                   pl.BlockSpec((nb, S, LANES), lambda b: (b, 0, 0))],
        out_shape=[jax.ShapeDtypeStruct((B, 3 * SUBLANES, S), f32),
                   jax.ShapeDtypeStruct((B, S, LANES), f32)],
        compiler_params=_params(1),
        name="mlgate",
    )(gates, _arr(gbias))


def _mlstm_kernel(qk_ref, vot_ref, rows_ref, cols_ref, ngt_ref, out_ref, ct_ref, m_ref):
    L = qk_ref.shape[1]
    dh = ML_HEAD_DIM
    scale = dh ** -0.5
    reps = L // dh
    nt = (((1,), (1,)), ((), ()))

    @pl.when(pl.program_id(1) == 0)
    def _():
        ct_ref[...] = jnp.zeros(ct_ref.shape, f32)
        m_ref[...] = jnp.zeros(m_ref.shape, f32)

    half = L // 2
    src = lax.broadcasted_iota(jnp.int32, (L, half), 0)
    tgt = lax.broadcasted_iota(jnp.int32, (L, half), 1)
    causal = [(src <= tgt)[:half], src <= tgt + half]
    ones_t = jnp.ones((dh, L), bf16)

    heads = range(ML_HEADS)
    hs = lambda h: slice(h * dh, (h + 1) * dh)
    ks = lambda h: slice(ML_WIDTH + h * dh, ML_WIDTH + (h + 1) * dh)
    m_all = m_ref[...]

    ct = [ct_ref[h] for h in heads]
    tcols = [slice(0, half), slice(half, L)]
    srows = [slice(0, half), slice(0, L)]
    s_t = [[lax.dot_general(qk_ref[0, srows[j], ks(h)], qk_ref[0, tcols[j], hs(h)], nt,
                            preferred_element_type=f32) for j in range(2)]
           for h in heads]
    inter = [lax.dot_general(ct[h].astype(bf16), qk_ref[0, :, hs(h)], nt, preferred_element_type=f32)
             for h in heads]

    c_row, cm_row, v_aug, intra = [], [], [], []
    for h in heads:
        c_row.append(rows_ref[0, C_ROW + h:C_ROW + h + 1, :])
        cm_row.append(rows_ref[0, CM_ROW + h:CM_ROW + h + 1, :])
        c_b = jnp.broadcast_to(cols_ref[0, :, h:h + 1], (L, dh))
        c_wide = jnp.concatenate([c_b] * (half // dh), axis=1) + math.log(scale)
        v_aug.append(jnp.concatenate([vot_ref[0, hs(h), :], ones_t], axis=0))
        parts = []
        for j in range(2):
            p = jnp.where(causal[j], jnp.exp(c_wide[srows[j]] - cm_row[h][:, tcols[j]]), 0.0)
            s_mat = (s_t[h][j] * p).astype(bf16)
            parts.append(jnp.dot(v_aug[h][:, srows[j]], s_mat, preferred_element_type=f32))
        intra.append(jnp.concatenate(parts, axis=1))

    m_prev, big_m, m_last, vw = [], [], [], []
    for h in heads:
        m_prev.append(m_all[h:h + 1, 0:1])
        big_m.append(jnp.maximum(m_prev[h], cm_row[h]))
        m_last.append(big_m[h][:, L - 1:L])
        wk = jnp.exp(c_row[h] - m_last[h]) * scale
        vw.append((v_aug[h].astype(f32) * wk).astype(bf16))
    upd = [jnp.dot(vw[h], qk_ref[0, :, ks(h)], preferred_element_type=f32) for h in heads]

    m_new = []
    for h in heads:
        b_row = rows_ref[0, B_ROW + h:B_ROW + h + 1, :]
        w_inter = jnp.exp(m_prev[h] - big_m[h])
        w_intra = jnp.exp(cm_row[h] - big_m[h])
        nd = w_inter * inter[h] + w_intra * intra[h]
        den = nd[dh:dh + 1, :]
        inv = 1.0 / jnp.maximum(jnp.abs(den), jnp.exp(-(b_row + big_m[h])))
        hh = nd[:dh, :] * inv
        mu = jnp.mean(hh, axis=0, keepdims=True)
        hc = hh - mu
        var = jnp.mean(hc * hc, axis=0, keepdims=True)
        ng = jnp.concatenate([ngt_ref[hs(h), :]] * reps, axis=1)
        hn = hc * lax.rsqrt(var + EPS) * ng
        gate = jax.nn.sigmoid(vot_ref[0, ks(h), :].astype(f32))
        out_ref[0, :, hs(h)] = (gate * hn).T.astype(bf16)
        ct_ref[h] = jnp.exp(m_prev[h] - m_last[h]) * ct[h] + upd[h]
        m_new.append(jnp.broadcast_to(b_row[:, L - 1:L] + m_last[h], (1, LANES)))
    m_ref[0:ML_HEADS, :] = jnp.concatenate(m_new, axis=0)


def _mlstm(qk, vot, gates, gbias, norm_gt):
    B, S, _ = qk.shape
    L = min(ML_CHUNK, S)
    rows, cols = _mlgate(gates, gbias, L)
    grid = (B, S // L)
    return pl.pallas_call(
        _mlstm_kernel,
        grid=grid,
        in_specs=[pl.BlockSpec((1, L, 2 * ML_WIDTH), lambda b, s: (b, s, 0)),
                  pl.BlockSpec((1, 2 * ML_WIDTH, L), lambda b, s: (b, 0, s)),
                  pl.BlockSpec((1, 3 * SUBLANES, L), lambda b, s: (b, 0, s)),
                  pl.BlockSpec((1, L, LANES), lambda b, s: (b, s, 0)),
                  _const_spec(norm_gt)],
        out_specs=pl.BlockSpec((1, L, ML_WIDTH), lambda b, s: (b, s, 0)),
        out_shape=jax.ShapeDtypeStruct((B, S, ML_WIDTH), bf16),
        scratch_shapes=[pltpu.VMEM((ML_HEADS, 2 * ML_HEAD_DIM, ML_HEAD_DIM), f32),
                        pltpu.VMEM((SUBLANES, LANES), f32)],
        compiler_params=_params(2),
        name="mlstm",
    )(qk, vot, rows, cols, _arr(norm_gt))


def _t5_bucket_table():
    assert WINDOW == BLOCK
    r = np.arange(BLOCK)[:, None]
    c = np.arange(BLOCK)[None, :]
    n = np.where(c > r, BLOCK + r - c, r - c)
    max_exact = REL_BUCKETS // 2
    nf = np.maximum(n, 1).astype(np.float32)
    large = max_exact + (np.log(nf / np.float32(max_exact)) / np.float32(math.log(REL_MAX_DIST / max_exact))
                         * np.float32(REL_BUCKETS - max_exact)).astype(np.int32)
    large = np.minimum(large, REL_BUCKETS - 1)
    return np.where(n < max_exact, n, large).astype(np.int32)


def _bias_kernel(bucket_ref, rb_ref, out_ref):
    bucket = bucket_ref[...]
    for h in range(SWA_HEADS):
        acc = jnp.zeros(bucket.shape, f32)
        for i in range(REL_BUCKETS):
            acc = jnp.where(bucket == i, rb_ref[i, h], acc)
        out_ref[h] = acc


def _swa_bias(rel_bias):
    bucket = jnp.asarray(_t5_bucket_table())
    return pl.pallas_call(
        _bias_kernel,
        in_specs=[pl.BlockSpec(memory_space=pltpu.VMEM), pl.BlockSpec(memory_space=pltpu.SMEM)],
        out_specs=pl.BlockSpec(memory_space=pltpu.VMEM),
        out_shape=jax.ShapeDtypeStruct((SWA_HEADS, BLOCK, BLOCK), f32),
        name="swa_bias",
    )(bucket, rel_bias)


def _swa_kernel(sw_ref, prev_ref, bias_ref, sink_ref, out_ref, *, layer):
    tq = sw_ref.shape[1]
    scale = SWA_HEAD_DIM ** -0.5
    pair = 2 * SWA_HEAD_DIM
    kc = slice(SWA_WIDTH, SWA_WIDTH + pair)
    vc = slice(SWA_WIDTH + pair, SWA_WIDTH + 2 * pair)
    first_tile = pl.program_id(1) == 0
    lo = lax.broadcasted_iota(jnp.int32, (BLOCK, pair), 1) < SWA_HEAD_DIM
    use_prev = (lax.broadcasted_iota(jnp.int32, (BLOCK, BLOCK), 1)
                > lax.broadcasted_iota(jnp.int32, (BLOCK, BLOCK), 0))
    zero_p = jnp.zeros((BLOCK, BLOCK), bf16)

    for j in range(tq // BLOCK):
        rows = slice(j * BLOCK, (j + 1) * BLOCK)
        prows = slice((j - 1) * BLOCK, j * BLOCK)
        if j == 0:
            k_prev, v_prev = prev_ref[0, :, 0:pair], prev_ref[0, :, pair:2 * pair]
        else:
            k_prev, v_prev = sw_ref[0, prows, kc], sw_ref[0, prows, vc]
        kband = jnp.concatenate([k_prev, sw_ref[0, rows, kc]], axis=0)
        vband = jnp.concatenate([v_prev, sw_ref[0, rows, vc]], axis=0)
        qs, heads = [], []
        for p in range(SWA_GROUP):
            qp = sw_ref[0, rows, p * pair:(p + 1) * pair] * scale
            zero = jnp.zeros_like(qp)
            qs += [jnp.where(lo, qp, zero), jnp.where(lo, zero, qp)]
            heads += [p, p + SWA_GROUP]
        logits = lax.dot_general(jnp.concatenate(qs, axis=0), kband, (((1,), (1,)), ((), ())),
                                 preferred_element_type=f32)
        ps, dens = [], []
        for n, head in enumerate(heads):
            lg2 = logits[n * BLOCK:(n + 1) * BLOCK]
            lg = jnp.where(use_prev, lg2[:, :BLOCK], lg2[:, BLOCK:]) + bias_ref[head]
            if j == 0:
                lg = jnp.where(jnp.logical_and(first_tile, use_prev), NEG_BIG, lg)
            sink = sink_ref[layer, head]
            mx = jnp.maximum(jnp.max(lg, axis=-1, keepdims=True), sink)
            pr = jnp.exp(lg - mx)
            dens.append(jnp.sum(pr, axis=-1, keepdims=True) + jnp.exp(sink - mx))
            pb = pr.astype(bf16)
            ps.append(jnp.concatenate([jnp.where(use_prev, pb, zero_p),
                                       jnp.where(use_prev, zero_p, pb)], axis=1))
        pv = jnp.dot(jnp.concatenate(ps, axis=0), vband, preferred_element_type=f32)
        o = [pv[n * BLOCK:(n + 1) * BLOCK] / dens[n] for n in range(len(heads))]
        for p in range(SWA_GROUP):
            out_ref[0, rows, p * pair:(p + 1) * pair] = jnp.where(lo, o[2 * p], o[2 * p + 1]).astype(bf16)


def _swa(sw, bias, sinks, layer):
    B, S, W = sw.shape
    tq = min(TQ_SWA, S)
    nblk = tq // BLOCK
    kvw = W - SWA_WIDTH
    assert kvw == 4 * SWA_HEAD_DIM and SWA_WIDTH % kvw == 0, "k/v group = [k0|k1|v0|v1]"
    return pl.pallas_call(
        functools.partial(_swa_kernel, layer=layer),
        grid=(B, S // tq),
        in_specs=[pl.BlockSpec((1, tq, W), lambda b, s: (b, s, 0)),
                  pl.BlockSpec((1, BLOCK, kvw),
                               lambda b, s: (b, jnp.maximum(s * nblk - 1, 0), SWA_WIDTH // kvw)),
                  _const_spec(bias),
                  pl.BlockSpec(memory_space=pltpu.SMEM)],
        out_specs=pl.BlockSpec((1, tq, SWA_WIDTH), lambda b, s: (b, s, 0)),
        out_shape=jax.ShapeDtypeStruct((B, S, SWA_WIDTH), bf16),
        compiler_params=_params(2),
        name="swa",
    )(sw, sw, bias, sinks)


def _memkv_kernel(mem_ref, wkv_ref, k_ref, v_ref):
    d = k_ref.shape[2]
    kv = jnp.dot(mem_ref[0].astype(bf16), wkv_ref[...], preferred_element_type=f32)
    k_ref[0] = kv[:, :d].astype(bf16)
    v_ref[0] = kv[:, d:].astype(bf16)


def _memkv(mem, wkv):
    B, M, D = mem.shape
    blk = pl.BlockSpec((1, M, D), lambda b: (b, 0, 0))
    return pl.pallas_call(
        _memkv_kernel,
        grid=(B,),
        in_specs=[blk, _const_spec(wkv)],
        out_specs=[blk, blk],
        out_shape=[jax.ShapeDtypeStruct((B, M, D), bf16)] * 2,
        compiler_params=_params(1),
        name="memkv",
    )(mem, _arr(wkv))


def _mix_kernel(hml_ref, hsw_ref, x_ref, woml_ref, wosw_ref, g1_ref, b1_ref,
                wq_ref, k_ref, v_ref, wo_ref, g2_ref, b2_ref, out_ref):
    tm, d = x_ref.shape[1], x_ref.shape[2]
    dh = d // XA_HEADS
    rows = [slice(i * tm // MIX_STREAMS, (i + 1) * tm // MIX_STREAMS) for i in range(MIX_STREAMS)]
    streams = range(MIX_STREAMS)
    cs = lambda a: slice(a * dh, (a + 1) * dh)

    h = [jnp.dot(hml_ref[0, r, :], woml_ref[...], preferred_element_type=f32)
         + jnp.dot(hsw_ref[0, r, :], wosw_ref[...], preferred_element_type=f32) for r in rows]
    x1, q = [], []
    for i in streams:
        x1.append(_layer_norm(ALPHA * x_ref[0, rows[i], :] + h[i], g1_ref[...], b1_ref[...]))
        q.append(jnp.dot(x1[i].astype(bf16), wq_ref[...], preferred_element_type=f32).astype(bf16))

    items = [(a, i) for a in range(XA_HEADS) for i in streams]
    lag = 2
    prob, den, outs = {}, {}, {}

    def logits(a, i):
        lg = lax.dot_general(q[i][:, cs(a)], k_ref[0, :, cs(a)], (((1,), (1,)), ((), ())),
                             preferred_element_type=f32) * (dh ** -0.5)
        p = jnp.exp(lg - jnp.max(lg, axis=-1, keepdims=True))
        den[a, i] = jnp.sum(p, axis=-1, keepdims=True)
        prob[a, i] = p.astype(bf16)

    def values(a, i):
        o = jnp.dot(prob[a, i], v_ref[0, :, cs(a)], preferred_element_type=f32) / den[a, i]
        outs[a, i] = o.astype(bf16)

    for n, item in enumerate(items):
        logits(*item)
        if n >= lag:
            values(*items[n - lag])
    for item in items[-lag:]:
        values(*item)

    h2 = [jnp.dot(jnp.concatenate([outs[a, i] for a in range(XA_HEADS)], axis=1), wo_ref[...],
                  preferred_element_type=f32) for i in streams]
    for i in streams:
        out_ref[0, rows[i], :] = _layer_norm(ALPHA * x1[i] + h2[i], g2_ref[...], b2_ref[...])


def _mix(hml, hsw, x, woml, wosw, g1, b1, wq, k, v, wo, g2, b2):
    B, S, D = x.shape
    tm = min(TM_MIX, S)
    M = k.shape[1]
    tok = lambda w: pl.BlockSpec((1, tm, w), lambda b, s: (b, s, 0))
    memblk = pl.BlockSpec((1, M, D), lambda b, s: (b, 0, 0))
    consts = [woml, wosw, g1, b1, wq]
    return pl.pallas_call(
        _mix_kernel,
        grid=(B, S // tm),
        in_specs=[tok(hml.shape[2]), tok(hsw.shape[2]), tok(D)]
                 + [_const_spec(a) for a in consts]
                 + [memblk, memblk] + [_const_spec(a) for a in (wo, g2, b2)],
        out_specs=tok(D),
        out_shape=jax.ShapeDtypeStruct((B, S, D), f32),
        compiler_params=_params(2),
        name="mix",
    )(hml, hsw, x, *map(_arr, consts), k, v, *map(_arr, (wo, g2, b2)))


GELU_C0 = math.sqrt(2.0 / math.pi)
GELU_C1 = 0.044715 * GELU_C0


def _ffn_kernel(x_ref, wup_ref, cw_ref, cb_ref, wdn_ref, g_ref, b_ref, out_ref,
                carry_ref, xb_ref, u00_ref, u01_ref, u10_ref, u11_ref, h_ref):
    tm = x_ref.shape[1]
    d_ff = wdn_ref.shape[0]
    cwid = FF_CHUNK
    n_chunks = d_ff // cwid
    n_slabs = cwid // LANES
    halo = SUBLANES
    u_refs = ((u00_ref, u01_ref), (u10_ref, u11_ref))

    @pl.when(pl.program_id(1) == 0)
    def _():
        carry_ref[...] = jnp.zeros(carry_ref.shape, f32)

    xb_ref[...] = x_ref[0].astype(bf16)

    def up(ci):
        for u_ref, c0 in zip(u_refs[ci % 2], (ci * cwid, d_ff + ci * cwid)):
            u = jnp.dot(xb_ref[...], wup_ref[:, c0:c0 + cwid], preferred_element_type=f32)
            for sl in range(n_slabs):
                cols = slice(c0 + sl * LANES, c0 + (sl + 1) * LANES)
                u_ref[sl, 0:halo, :] = carry_ref[:, cols]
                u_ref[sl, halo:halo + tm, :] = u[:, sl * LANES:(sl + 1) * LANES]
                carry_ref[:, cols] = u_ref[sl, tm:tm + halo, :]

    def act(ci):
        gu_ref, vu_ref = u_refs[ci % 2]
        for sl in range(n_slabs):
            gc = slice(ci * cwid + sl * LANES, ci * cwid + (sl + 1) * LANES)
            vc = slice(d_ff + gc.start, d_ff + gc.stop)
            gb, gw = cb_ref[:, gc], [cw_ref[j:j + 1, gc] for j in range(FFN_CONV)]
            vb, vw = 0.5 * cb_ref[:, vc], [0.5 * cw_ref[j:j + 1, vc] for j in range(FFN_CONV)]
            for r0 in range(0, tm, FF_ROWS):
                g, v = gb, vb
                for j in range(FFN_CONV):
                    rows = slice(r0 + halo - (FFN_CONV - 1) + j, r0 + halo - (FFN_CONV - 1) + j + FF_ROWS)
                    g = g + gu_ref[sl, rows, :] * gw[j]
                    v = v + vu_ref[sl, rows, :] * vw[j]
                t = jnp.tanh(g * (GELU_C0 + GELU_C1 * (g * g)))
                h_ref[r0:r0 + FF_ROWS, gc] = ((g * v) * (1.0 + t)).astype(bf16)

    up(0)
    for ci in range(n_chunks - 1):
        up(ci + 1)
        act(ci)
    top, bot = slice(0, tm // 2), slice(tm // 2, tm)
    k1 = (n_chunks - 1) * cwid
    h_top = jnp.dot(h_ref[top, :k1], wdn_ref[:k1, :], preferred_element_type=f32)
    act(n_chunks - 1)
    h_top = h_top + jnp.dot(h_ref[top, k1:], wdn_ref[k1:, :], preferred_element_type=f32)
    h_bot = jnp.dot(h_ref[bot, :], wdn_ref[...], preferred_element_type=f32)
    out_ref[0, top, :] = _layer_norm(ALPHA * x_ref[0, top, :] + h_top, g_ref[...], b_ref[...])
    out_ref[0, bot, :] = _layer_norm(ALPHA * x_ref[0, bot, :] + h_bot, g_ref[...], b_ref[...])


def _ffn(x, wup, cw, cb, wdn, g, b):
    B, S, D = x.shape
    tm = min(TM_FFN, S)
    cwid = FF_CHUNK
    assert wdn.shape[0] % cwid == 0 and wup.shape[1] == 2 * wdn.shape[0]
    tok = pl.BlockSpec((1, tm, D), lambda b_, s: (b_, s, 0))
    return pl.pallas_call(
        _ffn_kernel,
        grid=(B, S // tm),
        in_specs=[tok] + [_const_spec(a) for a in (wup, cw, cb, wdn, g, b)],
        out_specs=tok,
        out_shape=jax.ShapeDtypeStruct((B, S, D), f32),
        scratch_shapes=[pltpu.VMEM((SUBLANES, wup.shape[1]), f32),
                        pltpu.VMEM((tm, D), bf16),
                        ]
                       + [pltpu.VMEM((cwid // LANES, tm + SUBLANES, LANES), f32)] * 4
                       + [pltpu.VMEM((tm, wdn.shape[0]), bf16)],
        compiler_params=_params(2),
        name="ffn",
    )(x, *map(_arr, (wup, cw, cb, wdn, g, b)))


PAIR_ORDER = [h for p in range(SWA_GROUP) for h in (p, p + SWA_GROUP)]


def _swa_gate_columns(w):
    depth, D = w.shape[:2]
    g0 = 4 * ML_WIDTH
    q0 = g0 + 2 * ML_HEADS
    kv0 = q0 + SWA_WIDTH
    heads = [w[..., q0 + h * SWA_HEAD_DIM:q0 + (h + 1) * SWA_HEAD_DIM] for h in PAIR_ORDER]
    used = SWA_WIDTH + (w.shape[-1] - kv0) + 2 * ML_HEADS
    pad = jnp.zeros((depth, D, -used % PROJ_COLS), bf16)
    return jnp.concatenate(heads + [w[..., kv0:], w[..., g0:q0], pad], axis=-1)


def _pair_rows(w_out_b):
    r0 = ML_WIDTH
    return jnp.concatenate(
        [w_out_b[:, r0 + h * SWA_HEAD_DIM:r0 + (h + 1) * SWA_HEAD_DIM] for h in PAIR_ORDER], axis=1)


def kernel(x, mem, rel_bias, w_in, ml_conv_w, ml_conv_b, ml_i_bias, ml_f_bias, ml_norm_g, swa_sinks, w_out, ln1_g, ln1_b, xa_wq, xa_wkv, xa_wo, ln2_g, ln2_b, ffn_w_up, ffn_conv_w, ffn_conv_b, ffn_w_down, ln3_g, ln3_b):
    depth = w_in.shape[0]
    assert depth == DEPTH, "ALPHA is the DeepNorm constant of a DEPTH-layer trunk"
    rows = lambda a: a.reshape(depth, 1, -1)
    w_in_b, w_out_b = w_in.astype(bf16), w_out.astype(bf16)
    wswg, wo_sw = _swa_gate_columns(w_in_b), _pair_rows(w_out_b)
    wq, wkv, wo = xa_wq.astype(bf16), xa_wkv.astype(bf16), xa_wo.astype(bf16)
    wup, wdn = ffn_w_up.astype(bf16), ffn_w_down.astype(bf16)
    gbias = jnp.concatenate([ml_i_bias, ml_f_bias], axis=1).reshape(depth, 2 * ML_HEADS, 1)
    small = [rows(a) for a in (ml_conv_b, ln1_g, ln1_b, ln2_g, ln2_b, ffn_conv_b, ln3_g, ln3_b)]
    norm_gt = jnp.broadcast_to(ml_norm_g[:, :, None], ml_norm_g.shape + (LANES,))
    bias = _swa_bias(rel_bias)
    for l in range(depth):
        at = lambda a: _Layer(a, l)
        conv_b, g1, b1, g2, b2, ffn_cb, g3, b3 = map(at, small)
        wqk, wvo = (_Layer(w_in_b, l, axis=1, size=2 * ML_WIDTH, index=i) for i in (0, 1))
        wo_ml = _Layer(w_out_b, l, axis=0, size=ML_WIDTH, index=0)
        qk, vot, gates, sw = _inproj(x, wqk, wvo, at(wswg), at(ml_conv_w), conv_b)
        hml = _mlstm(qk, vot, gates, at(gbias), at(norm_gt))
        hsw = _swa(sw, bias, swa_sinks, l)
        k, v = _memkv(mem, at(wkv))
        x = _mix(hml, hsw, x, wo_ml, at(wo_sw), g1, b1, at(wq), k, v, at(wo), g2, b2)
        x = _ffn(x, at(wup), at(ffn_conv_w), ffn_cb, at(wdn), g3, b3)
    return x
```

```python
import functools
import math

import numpy as np
import jax
import jax.numpy as jnp
from jax import lax
from jax.experimental import pallas as pl
from jax.experimental.pallas import tpu as pltpu

f32 = jnp.float32
bf16 = jnp.bfloat16

ML_HEADS = 4
ML_HEAD_DIM = 128
ML_WIDTH = ML_HEADS * ML_HEAD_DIM
ML_CONV = 4
SWA_HEADS = 8
SWA_KV_HEADS = 2
SWA_GROUP = SWA_HEADS // SWA_KV_HEADS
SWA_HEAD_DIM = 64
SWA_WIDTH = SWA_HEADS * SWA_HEAD_DIM
WINDOW = 128
BLOCK = 128
REL_BUCKETS = 32
REL_MAX_DIST = 128
XA_HEADS = 4
FFN_CONV = 3
DEPTH = 2
ALPHA = (2.0 * DEPTH) ** 0.25
EPS = 1e-5

LANES = 128
SUBLANES = 8
TM_PROJ = 1024
PROJ_ROWS = 128
PROJ_COLS = 256
ML_CHUNK = 512
ML_STEP_CHUNKS = 2
MLG_BATCH = 4
TQ_SWA = 2048
TM_MIX = 1024
MIX_STREAMS = 4
TM_FFN = 512
FF_CHUNK = 256
FF_ROWS = 64
VMEM_LIMIT = 56 * 1024 * 1024
NEG_BIG = -1e30


def _params(n_axes, flags=None):
    return pltpu.CompilerParams(
        dimension_semantics=("arbitrary",) * n_axes, vmem_limit_bytes=VMEM_LIMIT, flags=flags)


class _Layer:
    def __init__(self, stacked, l, axis=None, size=None, index=0):
        self.stacked, self.l, self.axis, self.size, self.index = stacked, l, axis, size, index

    @property
    def shape(self):
        shape = list(self.stacked.shape[1:])
        if self.axis is not None:
            shape[self.axis] = self.size
        return tuple(shape)

    @property
    def block_index(self):
        idx = [0] * (self.stacked.ndim - 1)
        if self.axis is not None:
            idx[self.axis] = self.index
        return (self.l,) + tuple(idx)


def _arr(op):
    return op.stacked if isinstance(op, _Layer) else op


def _const_spec(op):
    if isinstance(op, _Layer):
        index = op.block_index
        return pl.BlockSpec((None,) + op.shape, lambda *_: index, pipeline_mode=pl.Buffered(1))
    nd = len(op.shape)
    return pl.BlockSpec(op.shape, lambda *_: (0,) * nd, pipeline_mode=pl.Buffered(1))


def _layer_norm(z, g, b):
    mu = jnp.mean(z, axis=-1, keepdims=True)
    zc = z - mu
    var = jnp.mean(zc * zc, axis=-1, keepdims=True)
    return zc * lax.rsqrt(var + EPS) * g + b


def _inproj_kernel(x_ref, wqk_ref, wvo_ref, wswg_ref, cw_ref, cb_ref,
                   qk_ref, vot_ref, g_ref, sw_ref, xb_ref, p_ref):
    tm = x_ref.shape[1]
    n_slabs = p_ref.shape[0]
    halo = SUBLANES

    @pl.when(pl.program_id(1) == 0)
    def _():
        p_ref[:, 0:halo, :] = jnp.zeros((n_slabs, halo, LANES), f32)

    xb_ref[...] = x_ref[0].astype(bf16)

    def project(w_ref, c0):
        return jnp.dot(xb_ref[...], w_ref[:, c0:c0 + PROJ_COLS], preferred_element_type=f32)

    def qk_piece(c0):
        acc = project(wqk_ref, c0)
        for li in range(PROJ_COLS // LANES):
            sl = c0 // LANES + li
            lanes = slice(sl * LANES, (sl + 1) * LANES)
            p_ref[sl, halo:halo + tm, :] = acc[:, li * LANES:(li + 1) * LANES]
            taps = [cw_ref[j:j + 1, lanes] for j in range(ML_CONV)]
            for r0 in range(0, tm, PROJ_ROWS):
                y = cb_ref[:, lanes]
                for j in range(ML_CONV):
                    first = r0 + halo - (ML_CONV - 1) + j
                    y = y + p_ref[sl, first:first + PROJ_ROWS, :] * taps[j]
                hy = 0.5 * y
                qk_ref[0, r0:r0 + PROJ_ROWS, lanes] = (hy + hy * jnp.tanh(hy)).astype(bf16)
            p_ref[sl, 0:halo, :] = p_ref[sl, tm:tm + halo, :]

    def vo_piece(c0):
        acc = project(wvo_ref, c0)
        for li in range(PROJ_COLS // LANES):
            lanes = slice(c0 + li * LANES, c0 + (li + 1) * LANES)
            vot_ref[0, lanes, :] = acc[:, li * LANES:(li + 1) * LANES].T.astype(bf16)

    def swg_piece(c0):
        sw_w = sw_ref.shape[2]
        acc = project(wswg_ref, c0)
        if c0 + PROJ_COLS <= sw_w:
            sw_ref[0, :, c0:c0 + PROJ_COLS] = acc.astype(bf16)
        else:
            g_ref[0] = acc[:, sw_w - c0:sw_w - c0 + LANES].T[0:SUBLANES, :]

    others = ([functools.partial(vo_piece, c0) for c0 in range(0, wvo_ref.shape[1], PROJ_COLS)]
              + [functools.partial(swg_piece, c0) for c0 in range(0, wswg_ref.shape[1], PROJ_COLS)])
    n_qk = wqk_ref.shape[1] // PROJ_COLS
    per_qk = -(-len(others) // n_qk)
    for i in range(n_qk):
        qk_piece(i * PROJ_COLS)
        for piece in others[i * per_qk:(i + 1) * per_qk]:
            piece()


def _inproj(x, wqk, wvo, wswg, cw, cb):
    B, S, D = x.shape
    tm = min(TM_PROJ, S)
    grid = (B, S // tm)
    sw_w = SWA_WIDTH + 4 * SWA_HEAD_DIM
    tok = lambda w: pl.BlockSpec((1, tm, w), lambda b, s: (b, s, 0))
    return pl.pallas_call(
        _inproj_kernel,
        grid=grid,
        in_specs=[tok(D)] + [_const_spec(a) for a in (wqk, wvo, wswg, cw, cb)],
        out_specs=[tok(wqk.shape[1]),
                   pl.BlockSpec((1, wvo.shape[1], tm), lambda b, s: (b, 0, s)),
                   pl.BlockSpec((1, SUBLANES, tm), lambda b, s: (b, 0, s)),
                   tok(sw_w)],
        out_shape=[jax.ShapeDtypeStruct((B, S, wqk.shape[1]), bf16),
                   jax.ShapeDtypeStruct((B, wvo.shape[1], S), bf16),
                   jax.ShapeDtypeStruct((B, SUBLANES, S), f32),
                   jax.ShapeDtypeStruct((B, S, sw_w), bf16)],
        scratch_shapes=[pltpu.VMEM((tm, D), bf16),
                        pltpu.VMEM((wqk.shape[1] // LANES, tm + SUBLANES, LANES), f32)],
        compiler_params=_params(2),
        name="inproj",
    )(x, *map(_arr, (wqk, wvo, wswg, cw, cb)))


C_ROW, CM_ROW, B_ROW = 0, SUBLANES, 2 * SUBLANES


def _scan_chunks(x, op, fill, chunk):
    pos = lax.broadcasted_iota(jnp.int32, x.shape, 1) % chunk
    sh = 1
    while sh < chunk:
        x = op(x, jnp.where(pos >= sh, pltpu.roll(x, sh, 1), fill))
        sh *= 2
    return x


def _mlgate_kernel(g_ref, gb_ref, rows_ref, cols_ref, *, chunk):
    for i in range(g_ref.shape[0]):
        g = g_ref[i] + gb_ref[...]
        ig = jnp.concatenate([g[0:4], g[0:4]], axis=0)
        fg = jnp.concatenate([g[4:8], g[4:8]], axis=0)
        lf = jnp.minimum(fg, 0.0) - jnp.log1p(jnp.exp(-jnp.abs(fg)))
        b = _scan_chunks(lf, jnp.add, 0.0, chunk)
        c = ig - b
        cm = _scan_chunks(c, jnp.maximum, -jnp.inf, chunk)
        rows_ref[i] = jnp.concatenate([c, cm, b], axis=0)
        stack = jnp.concatenate([c, jnp.zeros((LANES - SUBLANES, c.shape[1]), f32)], axis=0)
        cols_ref[i] = stack.T


def _mlgate(gates, gbias, chunk):
    B, _, S = gates.shape
    nb = math.gcd(B, MLG_BATCH)
    return pl.pallas_call(
        functools.partial(_mlgate_kernel, chunk=chunk),
        grid=(B // nb,),
        in_specs=[pl.BlockSpec((nb, SUBLANES, S), lambda b: (b, 0, 0)), _const_spec(gbias)],
        out_specs=[pl.BlockSpec((nb, 3 * SUBLANES, S), lambda b: (b, 0, 0)),
                   pl.BlockSpec((nb, S, LANES), lambda b: (b, 0, 0))],
        out_shape=[jax.ShapeDtypeStruct((B, 3 * SUBLANES, S), f32),
                   jax.ShapeDtypeStruct((B, S, LANES), f32)],
        compiler_params=_params(1),
        name="mlgate",
    )(gates, _arr(gbias))


def _mlstm_kernel(qk_ref, vot_ref, rows_ref, cols_ref, ngt_ref, out_ref, ct_ref, m_ref):
    L = ML_CHUNK if qk_ref.shape[1] % ML_CHUNK == 0 else qk_ref.shape[1]

    @pl.when(pl.program_id(1) == 0)
    def _():
        ct_ref[...] = jnp.zeros(ct_ref.shape, f32)
        m_ref[...] = jnp.zeros(m_ref.shape, f32)

    heads = range(ML_HEADS)
    m_all = m_ref[...]
    state = ([ct_ref[h] for h in heads], [m_all[h:h + 1, 0:1] for h in heads])
    for c in range(qk_ref.shape[1] // L):
        state = _mlstm_chunk(slice(c * L, (c + 1) * L), state, qk_ref, vot_ref, rows_ref, cols_ref,
                             ngt_ref, out_ref)
    for h in heads:
        ct_ref[h] = state[0][h]
    m_ref[0:ML_HEADS, :] = jnp.concatenate(
        [jnp.broadcast_to(m, (1, LANES)) for m in state[1]], axis=0)


def _mlstm_chunk(ts, state, qk_ref, vot_ref, rows_ref, cols_ref, ngt_ref, out_ref):
    ct, m_prev = state
    L = ts.stop - ts.start
    dh = ML_HEAD_DIM
    scale = dh ** -0.5
    reps = L // dh
    nt = (((1,), (1,)), ((), ()))
    half = L // 2
    src = lax.broadcasted_iota(jnp.int32, (L, half), 0)
    tgt = lax.broadcasted_iota(jnp.int32, (L, half), 1)
    causal = [(src <= tgt)[:half], src <= tgt + half]
    ones_t = jnp.ones((dh, L), bf16)
    heads = range(ML_HEADS)
    hs = lambda h: slice(h * dh, (h + 1) * dh)
    ks = lambda h: slice(ML_WIDTH + h * dh, ML_WIDTH + (h + 1) * dh)
    q = [qk_ref[0, ts, hs(h)] for h in heads]
    k = [qk_ref[0, ts, ks(h)] for h in heads]

    tcols = [slice(0, half), slice(half, L)]
    srows = [slice(0, half), slice(0, L)]
    s_t = [[lax.dot_general(k[h][srows[j]], q[h][tcols[j]], nt,
                            preferred_element_type=f32) for j in range(2)]
           for h in heads]
    inter = [lax.dot_general(ct[h].astype(bf16), q[h], nt, preferred_element_type=f32)
             for h in heads]

    c_row, cm_row, v_aug, intra = [], [], [], []
    for h in heads:
        c_row.append(rows_ref[0, C_ROW + h:C_ROW + h + 1, ts])
        cm_row.append(rows_ref[0, CM_ROW + h:CM_ROW + h + 1, ts])
        c_b = jnp.broadcast_to(cols_ref[0, ts, h:h + 1], (L, dh))
        c_wide = jnp.concatenate([c_b] * (half // dh), axis=1) + math.log(scale)
        v_aug.append(jnp.concatenate([vot_ref[0, hs(h), ts], ones_t], axis=0))
        parts = []
        for j in range(2):
            p = jnp.where(causal[j], jnp.exp(c_wide[srows[j]] - cm_row[h][:, tcols[j]]), 0.0)
            s_mat = (s_t[h][j] * p).astype(bf16)
            parts.append(jnp.dot(v_aug[h][:, srows[j]], s_mat, preferred_element_type=f32))
        intra.append(jnp.concatenate(parts, axis=1))

    big_m, m_last, vw = [], [], []
    for h in heads:
        big_m.append(jnp.maximum(m_prev[h], cm_row[h]))
        m_last.append(big_m[h][:, L - 1:L])
        wk = jnp.exp(c_row[h] - m_last[h]) * scale
        vw.append((v_aug[h].astype(f32) * wk).astype(bf16))
    upd = [jnp.dot(vw[h], k[h], preferred_element_type=f32) for h in heads]

    ct_new, m_new = [], []
    for h in heads:
        b_row = rows_ref[0, B_ROW + h:B_ROW + h + 1, ts]
        w_inter = jnp.exp(m_prev[h] - big_m[h])
        w_intra = jnp.exp(cm_row[h] - big_m[h])
        nd = w_inter * inter[h] + w_intra * intra[h]
        den = nd[dh:dh + 1, :]
        inv = 1.0 / jnp.maximum(jnp.abs(den), jnp.exp(-(b_row + big_m[h])))
        hh = nd[:dh, :] * inv
        mu = jnp.mean(hh, axis=0, keepdims=True)
        hc = hh - mu
        var = jnp.mean(hc * hc, axis=0, keepdims=True)
        ng = jnp.concatenate([ngt_ref[hs(h), :]] * reps, axis=1)
        hn = hc * lax.rsqrt(var + EPS) * ng
        gate = jax.nn.sigmoid(vot_ref[0, ks(h), ts].astype(f32))
        out_ref[0, ts, hs(h)] = (gate * hn).T.astype(bf16)
        ct_new.append(jnp.exp(m_prev[h] - m_last[h]) * ct[h] + upd[h])
        m_new.append(b_row[:, L - 1:L] + m_last[h])
    return ct_new, m_new


def _mlstm(qk, vot, gates, gbias, norm_gt):
    B, S, _ = qk.shape
    chunk = min(ML_CHUNK, S)
    rows, cols = _mlgate(gates, gbias, chunk)
    L = min(ML_STEP_CHUNKS * chunk, S)
    grid = (B, S // L)
    return pl.pallas_call(
        _mlstm_kernel,
        grid=grid,
        in_specs=[pl.BlockSpec((1, L, 2 * ML_WIDTH), lambda b, s: (b, s, 0)),
                  pl.BlockSpec((1, 2 * ML_WIDTH, L), lambda b, s: (b, 0, s)),
                  pl.BlockSpec((1, 3 * SUBLANES, L), lambda b, s: (b, 0, s)),
                  pl.BlockSpec((1, L, LANES), lambda b, s: (b, s, 0)),
                  _const_spec(norm_gt)],
        out_specs=pl.BlockSpec((1, L, ML_WIDTH), lambda b, s: (b, s, 0)),
        out_shape=jax.ShapeDtypeStruct((B, S, ML_WIDTH), bf16),
        scratch_shapes=[pltpu.VMEM((ML_HEADS, 2 * ML_HEAD_DIM, ML_HEAD_DIM), f32),
                        pltpu.VMEM((SUBLANES, LANES), f32)],
        compiler_params=_params(2),
        name="mlstm",
    )(qk, vot, rows, cols, _arr(norm_gt))


def _t5_bucket_table():
    assert WINDOW == BLOCK
    r = np.arange(BLOCK)[:, None]
    c = np.arange(BLOCK)[None, :]
    n = np.where(c > r, BLOCK + r - c, r - c)
    max_exact = REL_BUCKETS // 2
    nf = np.maximum(n, 1).astype(np.float32)
    large = max_exact + (np.log(nf / np.float32(max_exact)) / np.float32(math.log(REL_MAX_DIST / max_exact))
                         * np.float32(REL_BUCKETS - max_exact)).astype(np.int32)
    large = np.minimum(large, REL_BUCKETS - 1)
    return np.where(n < max_exact, n, large).astype(np.int32)


def _bias_kernel(bucket_ref, rb_ref, out_ref):
    bucket = bucket_ref[...]
    for h in range(SWA_HEADS):
        acc = jnp.zeros(bucket.shape, f32)
        for i in range(REL_BUCKETS):
            acc = jnp.where(bucket == i, rb_ref[i, h], acc)
        out_ref[h] = acc


def _swa_bias(rel_bias):
    bucket = jnp.asarray(_t5_bucket_table())
    return pl.pallas_call(
        _bias_kernel,
        in_specs=[pl.BlockSpec(memory_space=pltpu.VMEM), pl.BlockSpec(memory_space=pltpu.SMEM)],
        out_specs=pl.BlockSpec(memory_space=pltpu.VMEM),
        out_shape=jax.ShapeDtypeStruct((SWA_HEADS, BLOCK, BLOCK), f32),
        name="swa_bias",
    )(bucket, rel_bias)


def _swa_kernel(sw_ref, prev_ref, bias_ref, sink_ref, out_ref, *, layer):
    tq = sw_ref.shape[1]
    scale = SWA_HEAD_DIM ** -0.5
    pair = 2 * SWA_HEAD_DIM
    kc = slice(SWA_WIDTH, SWA_WIDTH + pair)
    vc = slice(SWA_WIDTH + pair, SWA_WIDTH + 2 * pair)
    first_tile = pl.program_id(1) == 0
    lo = lax.broadcasted_iota(jnp.int32, (BLOCK, pair), 1) < SWA_HEAD_DIM
    use_prev = (lax.broadcasted_iota(jnp.int32, (BLOCK, BLOCK), 1)
                > lax.broadcasted_iota(jnp.int32, (BLOCK, BLOCK), 0))
    zero_p = jnp.zeros((BLOCK, BLOCK), bf16)

    for j in range(tq // BLOCK):
        rows = slice(j * BLOCK, (j + 1) * BLOCK)
        prows = slice((j - 1) * BLOCK, j * BLOCK)
        if j == 0:
            k_prev, v_prev = prev_ref[0, :, 0:pair], prev_ref[0, :, pair:2 * pair]
        else:
            k_prev, v_prev = sw_ref[0, prows, kc], sw_ref[0, prows, vc]
        kband = jnp.concatenate([k_prev, sw_ref[0, rows, kc]], axis=0)
        vband = jnp.concatenate([v_prev, sw_ref[0, rows, vc]], axis=0)
        qs, heads = [], []
        for p in range(SWA_GROUP):
            qp = sw_ref[0, rows, p * pair:(p + 1) * pair] * scale
            zero = jnp.zeros_like(qp)
            qs += [jnp.where(lo, qp, zero), jnp.where(lo, zero, qp)]
            heads += [p, p + SWA_GROUP]
        logits = lax.dot_general(jnp.concatenate(qs, axis=0), kband, (((1,), (1,)), ((), ())),
                                 preferred_element_type=f32)
        ps, dens = [], []
        for n, head in enumerate(heads):
            lg2 = logits[n * BLOCK:(n + 1) * BLOCK]
            lg = jnp.where(use_prev, lg2[:, :BLOCK], lg2[:, BLOCK:]) + bias_ref[head]
            if j == 0:
                lg = jnp.where(jnp.logical_and(first_tile, use_prev), NEG_BIG, lg)
            sink = sink_ref[layer, head]
            mx = jnp.maximum(jnp.max(lg, axis=-1, keepdims=True), sink)
            pr = jnp.exp(lg - mx)
            dens.append(jnp.sum(pr, axis=-1, keepdims=True) + jnp.exp(sink - mx))
            pb = pr.astype(bf16)
            ps.append(jnp.concatenate([jnp.where(use_prev, pb, zero_p),
                                       jnp.where(use_prev, zero_p, pb)], axis=1))
        pv = jnp.dot(jnp.concatenate(ps, axis=0), vband, preferred_element_type=f32)
        o = [pv[n * BLOCK:(n + 1) * BLOCK] / dens[n] for n in range(len(heads))]
        for p in range(SWA_GROUP):
            out_ref[0, rows, p * pair:(p + 1) * pair] = jnp.where(lo, o[2 * p], o[2 * p + 1]).astype(bf16)


def _swa(sw, bias, sinks, layer):
    B, S, W = sw.shape
    tq = min(TQ_SWA, S)
    nblk = tq // BLOCK
    kvw = W - SWA_WIDTH
    assert kvw == 4 * SWA_HEAD_DIM and SWA_WIDTH % kvw == 0, "k/v group = [k0|k1|v0|v1]"
    return pl.pallas_call(
        functools.partial(_swa_kernel, layer=layer),
        grid=(B, S // tq),
        in_specs=[pl.BlockSpec((1, tq, W), lambda b, s: (b, s, 0)),
                  pl.BlockSpec((1, BLOCK, kvw),
                               lambda b, s: (b, jnp.maximum(s * nblk - 1, 0), SWA_WIDTH // kvw)),
                  _const_spec(bias),
                  pl.BlockSpec(memory_space=pltpu.SMEM)],
        out_specs=pl.BlockSpec((1, tq, SWA_WIDTH), lambda b, s: (b, s, 0)),
        out_shape=jax.ShapeDtypeStruct((B, S, SWA_WIDTH), bf16),
        compiler_params=_params(2),
        name="swa",
    )(sw, sw, bias, sinks)


def _memkv_kernel(mem_ref, wkv_ref, k_ref, v_ref):
    d = k_ref.shape[2]
    kv = jnp.dot(mem_ref[0].astype(bf16), wkv_ref[...], preferred_element_type=f32)
    k_ref[0] = kv[:, :d].astype(bf16)
    v_ref[0] = kv[:, d:].astype(bf16)


def _memkv(mem, wkv):
    B, M, D = mem.shape
    blk = pl.BlockSpec((1, M, D), lambda b: (b, 0, 0))
    return pl.pallas_call(
        _memkv_kernel,
        grid=(B,),
        in_specs=[blk, _const_spec(wkv)],
        out_specs=[blk, blk],
        out_shape=[jax.ShapeDtypeStruct((B, M, D), bf16)] * 2,
        compiler_params=_params(1),
        name="memkv",
    )(mem, _arr(wkv))


def _mix_kernel(hml_ref, hsw_ref, x_ref, woml_ref, wosw_ref, g1_ref, b1_ref,
                wq_ref, k_ref, v_ref, wo_ref, g2_ref, b2_ref, out_ref):
    tm, d = x_ref.shape[1], x_ref.shape[2]
    dh = d // XA_HEADS
    rows = [slice(i * tm // MIX_STREAMS, (i + 1) * tm // MIX_STREAMS) for i in range(MIX_STREAMS)]
    streams = range(MIX_STREAMS)
    cs = lambda a: slice(a * dh, (a + 1) * dh)

    h = [jnp.dot(hml_ref[0, r, :], woml_ref[...], preferred_element_type=f32)
         + jnp.dot(hsw_ref[0, r, :], wosw_ref[...], preferred_element_type=f32) for r in rows]
    x1, q = [], []
    for i in streams:
        x1.append(_layer_norm(ALPHA * x_ref[0, rows[i], :] + h[i], g1_ref[...], b1_ref[...]))
        q.append(jnp.dot(x1[i].astype(bf16), wq_ref[...], preferred_element_type=f32).astype(bf16))

    items = [(a, i) for a in range(XA_HEADS) for i in streams]
    lag = 2
    prob, den, outs = {}, {}, {}

    def logits(a, i):
        lg = lax.dot_general(q[i][:, cs(a)], k_ref[0, :, cs(a)], (((1,), (1,)), ((), ())),
                             preferred_element_type=f32) * (dh ** -0.5)
        p = jnp.exp(lg - jnp.max(lg, axis=-1, keepdims=True))
        den[a, i] = jnp.sum(p, axis=-1, keepdims=True)
        prob[a, i] = p.astype(bf16)

    def values(a, i):
        o = jnp.dot(prob[a, i], v_ref[0, :, cs(a)], preferred_element_type=f32) / den[a, i]
        outs[a, i] = o.astype(bf16)

    for n, item in enumerate(items):
        logits(*item)
        if n >= lag:
            values(*items[n - lag])
    for item in items[-lag:]:
        values(*item)

    h2 = [jnp.dot(jnp.concatenate([outs[a, i] for a in range(XA_HEADS)], axis=1), wo_ref[...],
                  preferred_element_type=f32) for i in streams]
    for i in streams:
        out_ref[0, rows[i], :] = _layer_norm(ALPHA * x1[i] + h2[i], g2_ref[...], b2_ref[...])


def _mix(hml, hsw, x, woml, wosw, g1, b1, wq, k, v, wo, g2, b2):
    B, S, D = x.shape
    tm = min(TM_MIX, S)
    M = k.shape[1]
    tok = lambda w: pl.BlockSpec((1, tm, w), lambda b, s: (b, s, 0))
    memblk = pl.BlockSpec((1, M, D), lambda b, s: (b, 0, 0))
    consts = [woml, wosw, g1, b1, wq]
    return pl.pallas_call(
        _mix_kernel,
        grid=(B, S // tm),
        in_specs=[tok(hml.shape[2]), tok(hsw.shape[2]), tok(D)]
                 + [_const_spec(a) for a in consts]
                 + [memblk, memblk] + [_const_spec(a) for a in (wo, g2, b2)],
        out_specs=tok(D),
        out_shape=jax.ShapeDtypeStruct((B, S, D), f32),
        compiler_params=_params(2),
        name="mix",
    )(hml, hsw, x, *map(_arr, consts), k, v, *map(_arr, (wo, g2, b2)))


GELU_C0 = math.sqrt(2.0 / math.pi)
GELU_C1 = 0.044715 * GELU_C0


def _ffn_kernel(x_ref, wup_ref, cw_ref, cb_ref, wdn_ref, g_ref, b_ref, out_ref,
                carry_ref, xb_ref, u00_ref, u01_ref, u10_ref, u11_ref, h_ref):
    tm = x_ref.shape[1]
    d_ff = wdn_ref.shape[0]
    cwid = FF_CHUNK
    n_chunks = d_ff // cwid
    n_slabs = cwid // LANES
    halo = SUBLANES
    u_refs = ((u00_ref, u01_ref), (u10_ref, u11_ref))

    @pl.when(pl.program_id(1) == 0)
    def _():
        carry_ref[...] = jnp.zeros(carry_ref.shape, f32)

    xb_ref[...] = x_ref[0].astype(bf16)

    def up(ci):
        for u_ref, c0 in zip(u_refs[ci % 2], (ci * cwid, d_ff + ci * cwid)):
            u = jnp.dot(xb_ref[...], wup_ref[:, c0:c0 + cwid], preferred_element_type=f32)
            for sl in range(n_slabs):
                cols = slice(c0 + sl * LANES, c0 + (sl + 1) * LANES)
                u_ref[sl, 0:halo, :] = carry_ref[:, cols]
                u_ref[sl, halo:halo + tm, :] = u[:, sl * LANES:(sl + 1) * LANES]
                carry_ref[:, cols] = u_ref[sl, tm:tm + halo, :]

    def act(ci):
        gu_ref, vu_ref = u_refs[ci % 2]
        for sl in range(n_slabs):
            gc = slice(ci * cwid + sl * LANES, ci * cwid + (sl + 1) * LANES)
            vc = slice(d_ff + gc.start, d_ff + gc.stop)
            gb, gw = cb_ref[:, gc], [cw_ref[j:j + 1, gc] for j in range(FFN_CONV)]
            vb, vw = 0.5 * cb_ref[:, vc], [0.5 * cw_ref[j:j + 1, vc] for j in range(FFN_CONV)]
            for r0 in range(0, tm, FF_ROWS):
                g, v = gb, vb
                for j in range(FFN_CONV):
                    rows = slice(r0 + halo - (FFN_CONV - 1) + j, r0 + halo - (FFN_CONV - 1) + j + FF_ROWS)
                    g = g + gu_ref[sl, rows, :] * gw[j]
                    v = v + vu_ref[sl, rows, :] * vw[j]
                t = jnp.tanh(g * (GELU_C0 + GELU_C1 * (g * g)))
                h_ref[r0:r0 + FF_ROWS, gc] = ((g * v) * (1.0 + t)).astype(bf16)

    up(0)
    for ci in range(n_chunks - 1):
        up(ci + 1)
        act(ci)
    top, bot = slice(0, tm // 2), slice(tm // 2, tm)
    k1 = (n_chunks - 1) * cwid
    h_top = jnp.dot(h_ref[top, :k1], wdn_ref[:k1, :], preferred_element_type=f32)
    act(n_chunks - 1)
    h_top = h_top + jnp.dot(h_ref[top, k1:], wdn_ref[k1:, :], preferred_element_type=f32)
    h_bot = jnp.dot(h_ref[bot, :], wdn_ref[...], preferred_element_type=f32)
    out_ref[0, top, :] = _layer_norm(ALPHA * x_ref[0, top, :] + h_top, g_ref[...], b_ref[...])
    out_ref[0, bot, :] = _layer_norm(ALPHA * x_ref[0, bot, :] + h_bot, g_ref[...], b_ref[...])


def _ffn(x, wup, cw, cb, wdn, g, b):
    B, S, D = x.shape
    tm = min(TM_FFN, S)
    cwid = FF_CHUNK
    assert wdn.shape[0] % cwid == 0 and wup.shape[1] == 2 * wdn.shape[0]
    tok = pl.BlockSpec((1, tm, D), lambda b_, s: (b_, s, 0))
    return pl.pallas_call(
        _ffn_kernel,
        grid=(B, S // tm),
        in_specs=[tok] + [_const_spec(a) for a in (wup, cw, cb, wdn, g, b)],
        out_specs=tok,
        out_shape=jax.ShapeDtypeStruct((B, S, D), f32),
        scratch_shapes=[pltpu.VMEM((SUBLANES, wup.shape[1]), f32),
                        pltpu.VMEM((tm, D), bf16),
                        ]
                       + [pltpu.VMEM((cwid // LANES, tm + SUBLANES, LANES), f32)] * 4
                       + [pltpu.VMEM((tm, wdn.shape[0]), bf16)],
        compiler_params=_params(2),
        name="ffn",
    )(x, *map(_arr, (wup, cw, cb, wdn, g, b)))


PAIR_ORDER = [h for p in range(SWA_GROUP) for h in (p, p + SWA_GROUP)]


def _swa_gate_columns(w):
    depth, D = w.shape[:2]
    g0 = 4 * ML_WIDTH
    q0 = g0 + 2 * ML_HEADS
    kv0 = q0 + SWA_WIDTH
    heads = [w[..., q0 + h * SWA_HEAD_DIM:q0 + (h + 1) * SWA_HEAD_DIM] for h in PAIR_ORDER]
    used = SWA_WIDTH + (w.shape[-1] - kv0) + 2 * ML_HEADS
    pad = jnp.zeros((depth, D, -used % PROJ_COLS), bf16)
    return jnp.concatenate(heads + [w[..., kv0:], w[..., g0:q0], pad], axis=-1)


def _pair_rows(w_out_b):
    r0 = ML_WIDTH
    return jnp.concatenate(
        [w_out_b[:, r0 + h * SWA_HEAD_DIM:r0 + (h + 1) * SWA_HEAD_DIM] for h in PAIR_ORDER], axis=1)


def kernel(x, mem, rel_bias, w_in, ml_conv_w, ml_conv_b, ml_i_bias, ml_f_bias, ml_norm_g, swa_sinks, w_out, ln1_g, ln1_b, xa_wq, xa_wkv, xa_wo, ln2_g, ln2_b, ffn_w_up, ffn_conv_w, ffn_conv_b, ffn_w_down, ln3_g, ln3_b):
    depth = w_in.shape[0]
    assert depth == DEPTH, "ALPHA is the DeepNorm constant of a DEPTH-layer trunk"
    rows = lambda a: a.reshape(depth, 1, -1)
    w_in_b, w_out_b = w_in.astype(bf16), w_out.astype(bf16)
    wswg, wo_sw = _swa_gate_columns(w_in_b), _pair_rows(w_out_b)
    wq, wkv, wo = xa_wq.astype(bf16), xa_wkv.astype(bf16), xa_wo.astype(bf16)
    wup, wdn = ffn_w_up.astype(bf16), ffn_w_down.astype(bf16)
    gbias = jnp.concatenate([ml_i_bias, ml_f_bias], axis=1).reshape(depth, 2 * ML_HEADS, 1)
    small = [rows(a) for a in (ml_conv_b, ln1_g, ln1_b, ln2_g, ln2_b, ffn_conv_b, ln3_g, ln3_b)]
    norm_gt = jnp.broadcast_to(ml_norm_g[:, :, None], ml_norm_g.shape + (LANES,))
    bias = _swa_bias(rel_bias)
    for l in range(depth):
        at = lambda a: _Layer(a, l)
        conv_b, g1, b1, g2, b2, ffn_cb, g3, b3 = map(at, small)
        wqk, wvo = (_Layer(w_in_b, l, axis=1, size=2 * ML_WIDTH, index=i) for i in (0, 1))
        wo_ml = _Layer(w_out_b, l, axis=0, size=ML_WIDTH, index=0)
        qk, vot, gates, sw = _inproj(x, wqk, wvo, at(wswg), at(ml_conv_w), conv_b)
        hml = _mlstm(qk, vot, gates, at(gbias), at(norm_gt))
        hsw = _swa(sw, bias, swa_sinks, l)
        k, v = _memkv(mem, at(wkv))
        x = _mix(hml, hsw, x, wo_ml, at(wo_sw), g1, b1, at(wq), k, v, at(wo), g2, b2)
        x = _ffn(x, at(wup), at(ffn_conv_w), ffn_cb, at(wdn), g3, b3)
    return x
```

```python
import functools
import math

import numpy as np
import jax
import jax.numpy as jnp
from jax import lax
from jax.experimental import pallas as pl
from jax.experimental.pallas import tpu as pltpu

f32 = jnp.float32
bf16 = jnp.bfloat16

ML_HEADS = 4
ML_HEAD_DIM = 128
ML_WIDTH = ML_HEADS * ML_HEAD_DIM
ML_CONV = 4
SWA_HEADS = 8
SWA_KV_HEADS = 2
SWA_GROUP = SWA_HEADS // SWA_KV_HEADS
SWA_HEAD_DIM = 64
SWA_WIDTH = SWA_HEADS * SWA_HEAD_DIM
WINDOW = 128
BLOCK = 128
REL_BUCKETS = 32
REL_MAX_DIST = 128
XA_HEADS = 4
FFN_CONV = 3
DEPTH = 2
ALPHA = (2.0 * DEPTH) ** 0.25
EPS = 1e-5

LANES = 128
SUBLANES = 8
TM_PROJ = 1024
PROJ_ROWS = 128
PROJ_COLS = 256
ML_CHUNK = 512
ML_STEP_CHUNKS = 2
TQ_SWA = 4096
TM_MIX = 1024
MIX_STREAMS = 4
TM_FFN = 512
FF_CHUNK = 256
FF_ROWS = 64
VMEM_LIMIT = 56 * 1024 * 1024
NEG_BIG = -1e30


def _params(n_axes, flags=None):
    return pltpu.CompilerParams(
        dimension_semantics=("arbitrary",) * n_axes, vmem_limit_bytes=VMEM_LIMIT, flags=flags)


class _Layer:
    def __init__(self, stacked, l, axis=None, size=None, index=0):
        self.stacked, self.l, self.axis, self.size, self.index = stacked, l, axis, size, index

    @property
    def shape(self):
        shape = list(self.stacked.shape[1:])
        if self.axis is not None:
            shape[self.axis] = self.size
        return tuple(shape)

    @property
    def block_index(self):
        idx = [0] * (self.stacked.ndim - 1)
        if self.axis is not None:
            idx[self.axis] = self.index
        return (self.l,) + tuple(idx)


def _arr(op):
    return op.stacked if isinstance(op, _Layer) else op


def _const_spec(op):
    if isinstance(op, _Layer):
        index = op.block_index
        return pl.BlockSpec((None,) + op.shape, lambda *_: index, pipeline_mode=pl.Buffered(1))
    nd = len(op.shape)
    return pl.BlockSpec(op.shape, lambda *_: (0,) * nd, pipeline_mode=pl.Buffered(1))


def _layer_norm(z, g, b):
    mu = jnp.mean(z, axis=-1, keepdims=True)
    zc = z - mu
    var = jnp.mean(zc * zc, axis=-1, keepdims=True)
    return zc * lax.rsqrt(var + EPS) * g + b


def _inproj_kernel(x_ref, wqk_ref, wvo_ref, wswg_ref, cw_ref, cb_ref, gb_ref,
                   qk_ref, vot_ref, rows_ref, cols_ref, sw_ref, xb_ref, p_ref, *, chunk):
    tm = x_ref.shape[1]
    n_slabs = p_ref.shape[0]
    halo = SUBLANES

    @pl.when(pl.program_id(1) == 0)
    def _():
        p_ref[:, 0:halo, :] = jnp.zeros((n_slabs, halo, LANES), f32)

    xb_ref[...] = x_ref[0].astype(bf16)

    def project(w_ref, c0):
        return jnp.dot(xb_ref[...], w_ref[:, c0:c0 + PROJ_COLS], preferred_element_type=f32)

    def qk_piece(c0):
        acc = project(wqk_ref, c0)
        for li in range(PROJ_COLS // LANES):
            sl = c0 // LANES + li
            lanes = slice(sl * LANES, (sl + 1) * LANES)
            p_ref[sl, halo:halo + tm, :] = acc[:, li * LANES:(li + 1) * LANES]
            taps = [cw_ref[j:j + 1, lanes] for j in range(ML_CONV)]
            for r0 in range(0, tm, PROJ_ROWS):
                y = cb_ref[:, lanes]
                for j in range(ML_CONV):
                    first = r0 + halo - (ML_CONV - 1) + j
                    y = y + p_ref[sl, first:first + PROJ_ROWS, :] * taps[j]
                hy = 0.5 * y
                qk_ref[0, r0:r0 + PROJ_ROWS, lanes] = (hy + hy * jnp.tanh(hy)).astype(bf16)
            p_ref[sl, 0:halo, :] = p_ref[sl, tm:tm + halo, :]

    def vo_piece(c0):
        acc = project(wvo_ref, c0)
        for li in range(PROJ_COLS // LANES):
            lanes = slice(c0 + li * LANES, c0 + (li + 1) * LANES)
            vot_ref[0, lanes, :] = acc[:, li * LANES:(li + 1) * LANES].T.astype(bf16)

    def swg_piece(c0):
        sw_w = sw_ref.shape[2]
        acc = project(wswg_ref, c0)
        if c0 + PROJ_COLS <= sw_w:
            sw_ref[0, :, c0:c0 + PROJ_COLS] = acc.astype(bf16)
        else:
            gates = acc[:, sw_w - c0:sw_w - c0 + LANES].T[0:SUBLANES, :]
            _gate_stats(gates + gb_ref[...], rows_ref, cols_ref, chunk)

    swg = [functools.partial(swg_piece, c0) for c0 in range(0, wswg_ref.shape[1], PROJ_COLS)]
    swg.pop()()
    others = [functools.partial(vo_piece, c0) for c0 in range(0, wvo_ref.shape[1], PROJ_COLS)] + swg
    n_qk = wqk_ref.shape[1] // PROJ_COLS
    per_qk = -(-len(others) // n_qk)
    for i in range(n_qk):
        qk_piece(i * PROJ_COLS)
        for piece in others[i * per_qk:(i + 1) * per_qk]:
            piece()


def _inproj(x, wqk, wvo, wswg, cw, cb, gbias):
    B, S, D = x.shape
    tm = min(TM_PROJ, S)
    chunk = min(ML_CHUNK, S)
    assert tm % chunk == 0
    grid = (B, S // tm)
    sw_w = SWA_WIDTH + 4 * SWA_HEAD_DIM
    tok = lambda w: pl.BlockSpec((1, tm, w), lambda b, s: (b, s, 0))
    return pl.pallas_call(
        functools.partial(_inproj_kernel, chunk=chunk),
        grid=grid,
        in_specs=[tok(D)] + [_const_spec(a) for a in (wqk, wvo, wswg, cw, cb, gbias)],
        out_specs=[tok(wqk.shape[1]),
                   pl.BlockSpec((1, wvo.shape[1], tm), lambda b, s: (b, 0, s)),
                   pl.BlockSpec((1, 3 * SUBLANES, tm), lambda b, s: (b, 0, s)),
                   tok(LANES),
                   tok(sw_w)],
        out_shape=[jax.ShapeDtypeStruct((B, S, wqk.shape[1]), bf16),
                   jax.ShapeDtypeStruct((B, wvo.shape[1], S), bf16),
                   jax.ShapeDtypeStruct((B, 3 * SUBLANES, S), f32),
                   jax.ShapeDtypeStruct((B, S, LANES), f32),
                   jax.ShapeDtypeStruct((B, S, sw_w), bf16)],
        scratch_shapes=[pltpu.VMEM((tm, D), bf16),
                        pltpu.VMEM((wqk.shape[1] // LANES, tm + SUBLANES, LANES), f32)],
        compiler_params=_params(2),
        name="inproj",
    )(x, *map(_arr, (wqk, wvo, wswg, cw, cb, gbias)))


C_ROW, CM_ROW, B_ROW = 0, SUBLANES, 2 * SUBLANES


def _scan_chunks(x, op, fill, chunk):
    pos = lax.broadcasted_iota(jnp.int32, x.shape, 1) % chunk
    sh = 1
    while sh < chunk:
        x = op(x, jnp.where(pos >= sh, pltpu.roll(x, sh, 1), fill))
        sh *= 2
    return x


def _gate_stats(g, rows_ref, cols_ref, chunk):
    ig = jnp.concatenate([g[0:4], g[0:4]], axis=0)
    fg = jnp.concatenate([g[4:8], g[4:8]], axis=0)
    lf = jnp.minimum(fg, 0.0) - jnp.log1p(jnp.exp(-jnp.abs(fg)))
    b = _scan_chunks(lf, jnp.add, 0.0, chunk)
    c = ig - b
    cm = _scan_chunks(c, jnp.maximum, -jnp.inf, chunk)
    rows_ref[0] = jnp.concatenate([c, cm, b], axis=0)
    stack = jnp.concatenate([c, jnp.zeros((LANES - SUBLANES, c.shape[1]), f32)], axis=0)
    cols_ref[0] = stack.T


def _mlstm_kernel(qk_ref, vot_ref, rows_ref, cols_ref, ngt_ref, out_ref, ct_ref, m_ref):
    L = ML_CHUNK if qk_ref.shape[1] % ML_CHUNK == 0 else qk_ref.shape[1]

    @pl.when(pl.program_id(1) == 0)
    def _():
        ct_ref[...] = jnp.zeros(ct_ref.shape, f32)
        m_ref[...] = jnp.zeros(m_ref.shape, f32)

    heads = range(ML_HEADS)
    m_all = m_ref[...]
    state = ([ct_ref[h] for h in heads], [m_all[h:h + 1, 0:1] for h in heads])
    for c in range(qk_ref.shape[1] // L):
        state = _mlstm_chunk(slice(c * L, (c + 1) * L), state, qk_ref, vot_ref, rows_ref, cols_ref,
                             ngt_ref, out_ref)
    for h in heads:
        ct_ref[h] = state[0][h]
    m_ref[0:ML_HEADS, :] = jnp.concatenate(
        [jnp.broadcast_to(m, (1, LANES)) for m in state[1]], axis=0)


def _mlstm_chunk(ts, state, qk_ref, vot_ref, rows_ref, cols_ref, ngt_ref, out_ref):
    ct, m_prev = state
    L = ts.stop - ts.start
    dh = ML_HEAD_DIM
    scale = dh ** -0.5
    reps = L // dh
    nt = (((1,), (1,)), ((), ()))
    half = L // 2
    src = lax.broadcasted_iota(jnp.int32, (L, half), 0)
    tgt = lax.broadcasted_iota(jnp.int32, (L, half), 1)
    causal = [(src <= tgt)[:half], src <= tgt + half]
    ones_t = jnp.ones((dh, L), bf16)
    heads = range(ML_HEADS)
    hs = lambda h: slice(h * dh, (h + 1) * dh)
    ks = lambda h: slice(ML_WIDTH + h * dh, ML_WIDTH + (h + 1) * dh)
    q = [qk_ref[0, ts, hs(h)] for h in heads]
    k = [qk_ref[0, ts, ks(h)] for h in heads]

    tcols = [slice(0, half), slice(half, L)]
    srows = [slice(0, half), slice(0, L)]
    s_t = [[lax.dot_general(k[h][srows[j]], q[h][tcols[j]], nt,
                            preferred_element_type=f32) for j in range(2)]
           for h in heads]
    inter = [lax.dot_general(ct[h].astype(bf16), q[h], nt, preferred_element_type=f32)
             for h in heads]

    c_row, cm_row, v_aug, intra = [], [], [], []
    for h in heads:
        c_row.append(rows_ref[0, C_ROW + h:C_ROW + h + 1, ts])
        cm_row.append(rows_ref[0, CM_ROW + h:CM_ROW + h + 1, ts])
        c_b = jnp.broadcast_to(cols_ref[0, ts, h:h + 1], (L, dh))
        c_wide = jnp.concatenate([c_b] * (half // dh), axis=1) + math.log(scale)
        v_aug.append(jnp.concatenate([vot_ref[0, hs(h), ts], ones_t], axis=0))
        parts = []
        for j in range(2):
            p = jnp.where(causal[j], jnp.exp(c_wide[srows[j]] - cm_row[h][:, tcols[j]]), 0.0)
            s_mat = (s_t[h][j] * p).astype(bf16)
            parts.append(jnp.dot(v_aug[h][:, srows[j]], s_mat, preferred_element_type=f32))
        intra.append(jnp.concatenate(parts, axis=1))

    big_m, m_last, vw = [], [], []
    for h in heads:
        big_m.append(jnp.maximum(m_prev[h], cm_row[h]))
        m_last.append(big_m[h][:, L - 1:L])
        wk = jnp.exp(c_row[h] - m_last[h]) * scale
        vw.append((v_aug[h].astype(f32) * wk).astype(bf16))
    upd = [jnp.dot(vw[h], k[h], preferred_element_type=f32) for h in heads]

    ct_new, m_new = [], []
    for h in heads:
        b_row = rows_ref[0, B_ROW + h:B_ROW + h + 1, ts]
        w_inter = jnp.exp(m_prev[h] - big_m[h])
        w_intra = jnp.exp(cm_row[h] - big_m[h])
        nd = w_inter * inter[h] + w_intra * intra[h]
        den = nd[dh:dh + 1, :]
        inv = 1.0 / jnp.maximum(jnp.abs(den), jnp.exp(-(b_row + big_m[h])))
        hh = nd[:dh, :] * inv
        mu = jnp.mean(hh, axis=0, keepdims=True)
        hc = hh - mu
        var = jnp.mean(hc * hc, axis=0, keepdims=True)
        ng = jnp.concatenate([ngt_ref[hs(h), :]] * reps, axis=1)
        hn = hc * lax.rsqrt(var + EPS) * ng
        gate = jax.nn.sigmoid(vot_ref[0, ks(h), ts].astype(f32))
        out_ref[0, ts, hs(h)] = (gate * hn).T.astype(bf16)
        ct_new.append(jnp.exp(m_prev[h] - m_last[h]) * ct[h] + upd[h])
        m_new.append(b_row[:, L - 1:L] + m_last[h])
    return ct_new, m_new


def _mlstm(qk, vot, rows, cols, norm_gt):
    B, S, _ = qk.shape
    chunk = min(ML_CHUNK, S)
    L = min(ML_STEP_CHUNKS * chunk, S)
    grid = (B, S // L)
    return pl.pallas_call(
        _mlstm_kernel,
        grid=grid,
        in_specs=[pl.BlockSpec((1, L, 2 * ML_WIDTH), lambda b, s: (b, s, 0)),
                  pl.BlockSpec((1, 2 * ML_WIDTH, L), lambda b, s: (b, 0, s)),
                  pl.BlockSpec((1, 3 * SUBLANES, L), lambda b, s: (b, 0, s)),
                  pl.BlockSpec((1, L, LANES), lambda b, s: (b, s, 0)),
                  _const_spec(norm_gt)],
        out_specs=pl.BlockSpec((1, L, ML_WIDTH), lambda b, s: (b, s, 0)),
        out_shape=jax.ShapeDtypeStruct((B, S, ML_WIDTH), bf16),
        scratch_shapes=[pltpu.VMEM((ML_HEADS, 2 * ML_HEAD_DIM, ML_HEAD_DIM), f32),
                        pltpu.VMEM((SUBLANES, LANES), f32)],
        compiler_params=_params(2),
        name="mlstm",
    )(qk, vot, rows, cols, _arr(norm_gt))


def _t5_bucket_table():
    assert WINDOW == BLOCK
    r = np.arange(BLOCK)[:, None]
    c = np.arange(BLOCK)[None, :]
    n = np.where(c > r, BLOCK + r - c, r - c)
    max_exact = REL_BUCKETS // 2
    nf = np.maximum(n, 1).astype(np.float32)
    large = max_exact + (np.log(nf / np.float32(max_exact)) / np.float32(math.log(REL_MAX_DIST / max_exact))
                         * np.float32(REL_BUCKETS - max_exact)).astype(np.int32)
    large = np.minimum(large, REL_BUCKETS - 1)
    return np.where(n < max_exact, n, large).astype(np.int32)


def _bias_kernel(bucket_ref, rb_ref, out_ref):
    bucket = bucket_ref[...]
    for h in range(SWA_HEADS):
        acc = jnp.zeros(bucket.shape, f32)
        for i in range(REL_BUCKETS):
            acc = jnp.where(bucket == i, rb_ref[i, h], acc)
        out_ref[h] = acc


def _swa_bias(rel_bias):
    bucket = jnp.asarray(_t5_bucket_table())
    return pl.pallas_call(
        _bias_kernel,
        in_specs=[pl.BlockSpec(memory_space=pltpu.VMEM), pl.BlockSpec(memory_space=pltpu.SMEM)],
        out_specs=pl.BlockSpec(memory_space=pltpu.VMEM),
        out_shape=jax.ShapeDtypeStruct((SWA_HEADS, BLOCK, BLOCK), f32),
        name="swa_bias",
    )(bucket, rel_bias)


def _swa_kernel(sw_ref, prev_ref, bias_ref, sink_ref, out_ref, *, layer):
    tq = sw_ref.shape[1]
    scale = SWA_HEAD_DIM ** -0.5
    pair = 2 * SWA_HEAD_DIM
    kc = slice(SWA_WIDTH, SWA_WIDTH + pair)
    vc = slice(SWA_WIDTH + pair, SWA_WIDTH + 2 * pair)
    first_tile = pl.program_id(1) == 0
    lo = lax.broadcasted_iota(jnp.int32, (BLOCK, pair), 1) < SWA_HEAD_DIM
    use_prev = (lax.broadcasted_iota(jnp.int32, (BLOCK, BLOCK), 1)
                > lax.broadcasted_iota(jnp.int32, (BLOCK, BLOCK), 0))
    zero_p = jnp.zeros((BLOCK, BLOCK), bf16)

    for j in range(tq // BLOCK):
        rows = slice(j * BLOCK, (j + 1) * BLOCK)
        prows = slice((j - 1) * BLOCK, j * BLOCK)
        if j == 0:
            k_prev, v_prev = prev_ref[0, :, 0:pair], prev_ref[0, :, pair:2 * pair]
        else:
            k_prev, v_prev = sw_ref[0, prows, kc], sw_ref[0, prows, vc]
        kband = jnp.concatenate([k_prev, sw_ref[0, rows, kc]], axis=0)
        vband = jnp.concatenate([v_prev, sw_ref[0, rows, vc]], axis=0)
        qs, heads = [], []
        for p in range(SWA_GROUP):
            qp = sw_ref[0, rows, p * pair:(p + 1) * pair] * scale
            zero = jnp.zeros_like(qp)
            qs += [jnp.where(lo, qp, zero), jnp.where(lo, zero, qp)]
            heads += [p, p + SWA_GROUP]
        logits = lax.dot_general(jnp.concatenate(qs, axis=0), kband, (((1,), (1,)), ((), ())),
                                 preferred_element_type=f32)
        ps, dens = [], []
        for n, head in enumerate(heads):
            lg2 = logits[n * BLOCK:(n + 1) * BLOCK]
            lg = jnp.where(use_prev, lg2[:, :BLOCK], lg2[:, BLOCK:]) + bias_ref[head]
            if j == 0:
                lg = jnp.where(jnp.logical_and(first_tile, use_prev), NEG_BIG, lg)
            sink = sink_ref[layer, head]
            mx = jnp.maximum(jnp.max(lg, axis=-1, keepdims=True), sink)
            pr = jnp.exp(lg - mx)
            dens.append(jnp.sum(pr, axis=-1, keepdims=True) + jnp.exp(sink - mx))
            pb = pr.astype(bf16)
            ps.append(jnp.concatenate([jnp.where(use_prev, pb, zero_p),
                                       jnp.where(use_prev, zero_p, pb)], axis=1))
        pv = jnp.dot(jnp.concatenate(ps, axis=0), vband, preferred_element_type=f32)
        o = [pv[n * BLOCK:(n + 1) * BLOCK] / dens[n] for n in range(len(heads))]
        for p in range(SWA_GROUP):
            out_ref[0, rows, p * pair:(p + 1) * pair] = jnp.where(lo, o[2 * p], o[2 * p + 1]).astype(bf16)


def _swa(sw, bias, sinks, layer):
    B, S, W = sw.shape
    tq = min(TQ_SWA, S)
    nblk = tq // BLOCK
    kvw = W - SWA_WIDTH
    assert kvw == 4 * SWA_HEAD_DIM and SWA_WIDTH % kvw == 0, "k/v group = [k0|k1|v0|v1]"
    return pl.pallas_call(
        functools.partial(_swa_kernel, layer=layer),
        grid=(B, S // tq),
        in_specs=[pl.BlockSpec((1, tq, W), lambda b, s: (b, s, 0)),
                  pl.BlockSpec((1, BLOCK, kvw),
                               lambda b, s: (b, jnp.maximum(s * nblk - 1, 0), SWA_WIDTH // kvw)),
                  _const_spec(bias),
                  pl.BlockSpec(memory_space=pltpu.SMEM)],
        out_specs=pl.BlockSpec((1, tq, SWA_WIDTH), lambda b, s: (b, s, 0)),
        out_shape=jax.ShapeDtypeStruct((B, S, SWA_WIDTH), bf16),
        compiler_params=_params(2),
        name="swa",
    )(sw, sw, bias, sinks)


def _memkv_kernel(mem_ref, wkv_ref, k_ref, v_ref):
    d = k_ref.shape[2]
    kv = jnp.dot(mem_ref[0].astype(bf16), wkv_ref[...], preferred_element_type=f32)
    k_ref[0] = kv[:, :d].astype(bf16)
    v_ref[0] = kv[:, d:].astype(bf16)


def _memkv(mem, wkv):
    B, M, D = mem.shape
    blk = pl.BlockSpec((1, M, D), lambda b: (b, 0, 0))
    return pl.pallas_call(
        _memkv_kernel,
        grid=(B,),
        in_specs=[blk, _const_spec(wkv)],
        out_specs=[blk, blk],
        out_shape=[jax.ShapeDtypeStruct((B, M, D), bf16)] * 2,
        compiler_params=_params(1),
        name="memkv",
    )(mem, _arr(wkv))


def _mix_kernel(hml_ref, hsw_ref, x_ref, woml_ref, wosw_ref, g1_ref, b1_ref,
                wq_ref, k_ref, v_ref, wo_ref, g2_ref, b2_ref, out_ref):
    tm, d = x_ref.shape[1], x_ref.shape[2]
    dh = d // XA_HEADS
    rows = [slice(i * tm // MIX_STREAMS, (i + 1) * tm // MIX_STREAMS) for i in range(MIX_STREAMS)]
    streams = range(MIX_STREAMS)
    cs = lambda a: slice(a * dh, (a + 1) * dh)

    h = [jnp.dot(hml_ref[0, r, :], woml_ref[...], preferred_element_type=f32)
         + jnp.dot(hsw_ref[0, r, :], wosw_ref[...], preferred_element_type=f32) for r in rows]
    x1, q = [], []
    for i in streams:
        x1.append(_layer_norm(ALPHA * x_ref[0, rows[i], :] + h[i], g1_ref[...], b1_ref[...]))
        q.append(jnp.dot(x1[i].astype(bf16), wq_ref[...], preferred_element_type=f32).astype(bf16))

    items = [(a, i) for a in range(XA_HEADS) for i in streams]
    lag = 2
    prob, den, outs = {}, {}, {}

    def logits(a, i):
        lg = lax.dot_general(q[i][:, cs(a)], k_ref[0, :, cs(a)], (((1,), (1,)), ((), ())),
                             preferred_element_type=f32) * (dh ** -0.5)
        p = jnp.exp(lg - jnp.max(lg, axis=-1, keepdims=True))
        den[a, i] = jnp.sum(p, axis=-1, keepdims=True)
        prob[a, i] = p.astype(bf16)

    def values(a, i):
        o = jnp.dot(prob[a, i], v_ref[0, :, cs(a)], preferred_element_type=f32) / den[a, i]
        outs[a, i] = o.astype(bf16)

    for n, item in enumerate(items):
        logits(*item)
        if n >= lag:
            values(*items[n - lag])
    for item in items[-lag:]:
        values(*item)

    h2 = [jnp.dot(jnp.concatenate([outs[a, i] for a in range(XA_HEADS)], axis=1), wo_ref[...],
                  preferred_element_type=f32) for i in streams]
    for i in streams:
        out_ref[0, rows[i], :] = _layer_norm(ALPHA * x1[i] + h2[i], g2_ref[...], b2_ref[...])


def _mix(hml, hsw, x, woml, wosw, g1, b1, wq, k, v, wo, g2, b2):
    B, S, D = x.shape
    tm = min(TM_MIX, S)
    M = k.shape[1]
    tok = lambda w: pl.BlockSpec((1, tm, w), lambda b, s: (b, s, 0))
    memblk = pl.BlockSpec((1, M, D), lambda b, s: (b, 0, 0))
    consts = [woml, wosw, g1, b1, wq]
    return pl.pallas_call(
        _mix_kernel,
        grid=(B, S // tm),
        in_specs=[tok(hml.shape[2]), tok(hsw.shape[2]), tok(D)]
                 + [_const_spec(a) for a in consts]
                 + [memblk, memblk] + [_const_spec(a) for a in (wo, g2, b2)],
        out_specs=tok(D),
        out_shape=jax.ShapeDtypeStruct((B, S, D), f32),
        compiler_params=_params(2),
        name="mix",
    )(hml, hsw, x, *map(_arr, consts), k, v, *map(_arr, (wo, g2, b2)))


GELU_C0 = math.sqrt(2.0 / math.pi)
GELU_C1 = 0.044715 * GELU_C0


def _ffn_kernel(x_ref, wup_ref, cw_ref, cb_ref, wdn_ref, g_ref, b_ref, out_ref,
                carry_ref, xb_ref, u00_ref, u01_ref, u10_ref, u11_ref, h_ref):
    tm = x_ref.shape[1]
    d_ff = wdn_ref.shape[0]
    cwid = FF_CHUNK
    n_chunks = d_ff // cwid
    n_slabs = cwid // LANES
    halo = SUBLANES
    u_refs = ((u00_ref, u01_ref), (u10_ref, u11_ref))

    @pl.when(pl.program_id(1) == 0)
    def _():
        carry_ref[...] = jnp.zeros(carry_ref.shape, f32)

    xb_ref[...] = x_ref[0].astype(bf16)

    def up(ci):
        for u_ref, c0 in zip(u_refs[ci % 2], (ci * cwid, d_ff + ci * cwid)):
            u = jnp.dot(xb_ref[...], wup_ref[:, c0:c0 + cwid], preferred_element_type=f32)
            for sl in range(n_slabs):
                cols = slice(c0 + sl * LANES, c0 + (sl + 1) * LANES)
                u_ref[sl, 0:halo, :] = carry_ref[:, cols]
                u_ref[sl, halo:halo + tm, :] = u[:, sl * LANES:(sl + 1) * LANES]
                carry_ref[:, cols] = u_ref[sl, tm:tm + halo, :]

    def act(ci):
        gu_ref, vu_ref = u_refs[ci % 2]
        for sl in range(n_slabs):
            gc = slice(ci * cwid + sl * LANES, ci * cwid + (sl + 1) * LANES)
            vc = slice(d_ff + gc.start, d_ff + gc.stop)
            gb, gw = cb_ref[:, gc], [cw_ref[j:j + 1, gc] for j in range(FFN_CONV)]
            vb, vw = 0.5 * cb_ref[:, vc], [0.5 * cw_ref[j:j + 1, vc] for j in range(FFN_CONV)]
            for r0 in range(0, tm, FF_ROWS):
                g, v = gb, vb
                for j in range(FFN_CONV):
                    rows = slice(r0 + halo - (FFN_CONV - 1) + j, r0 + halo - (FFN_CONV - 1) + j + FF_ROWS)
                    g = g + gu_ref[sl, rows, :] * gw[j]
                    v = v + vu_ref[sl, rows, :] * vw[j]
                t = jnp.tanh(g * (GELU_C0 + GELU_C1 * (g * g)))
                h_ref[r0:r0 + FF_ROWS, gc] = ((g * v) * (1.0 + t)).astype(bf16)

    up(0)
    for ci in range(n_chunks - 1):
        up(ci + 1)
        act(ci)
    top, bot = slice(0, tm // 2), slice(tm // 2, tm)
    k1 = (n_chunks - 1) * cwid
    h_top = jnp.dot(h_ref[top, :k1], wdn_ref[:k1, :], preferred_element_type=f32)
    act(n_chunks - 1)
    h_top = h_top + jnp.dot(h_ref[top, k1:], wdn_ref[k1:, :], preferred_element_type=f32)
    h_bot = jnp.dot(h_ref[bot, :], wdn_ref[...], preferred_element_type=f32)
    out_ref[0, top, :] = _layer_norm(ALPHA * x_ref[0, top, :] + h_top, g_ref[...], b_ref[...])
    out_ref[0, bot, :] = _layer_norm(ALPHA * x_ref[0, bot, :] + h_bot, g_ref[...], b_ref[...])


def _ffn(x, wup, cw, cb, wdn, g, b):
    B, S, D = x.shape
    tm = min(TM_FFN, S)
    cwid = FF_CHUNK
    assert wdn.shape[0] % cwid == 0 and wup.shape[1] == 2 * wdn.shape[0]
    tok = pl.BlockSpec((1, tm, D), lambda b_, s: (b_, s, 0))
    return pl.pallas_call(
        _ffn_kernel,
        grid=(B, S // tm),
        in_specs=[tok] + [_const_spec(a) for a in (wup, cw, cb, wdn, g, b)],
        out_specs=tok,
        out_shape=jax.ShapeDtypeStruct((B, S, D), f32),
        scratch_shapes=[pltpu.VMEM((SUBLANES, wup.shape[1]), f32),
                        pltpu.VMEM((tm, D), bf16),
                        ]
                       + [pltpu.VMEM((cwid // LANES, tm + SUBLANES, LANES), f32)] * 4
                       + [pltpu.VMEM((tm, wdn.shape[0]), bf16)],
        compiler_params=_params(2),
        name="ffn",
    )(x, *map(_arr, (wup, cw, cb, wdn, g, b)))


PAIR_ORDER = [h for p in range(SWA_GROUP) for h in (p, p + SWA_GROUP)]


def _swa_gate_columns(w):
    depth, D = w.shape[:2]
    g0 = 4 * ML_WIDTH
    q0 = g0 + 2 * ML_HEADS
    kv0 = q0 + SWA_WIDTH
    heads = [w[..., q0 + h * SWA_HEAD_DIM:q0 + (h + 1) * SWA_HEAD_DIM] for h in PAIR_ORDER]
    used = SWA_WIDTH + (w.shape[-1] - kv0) + 2 * ML_HEADS
    pad = jnp.zeros((depth, D, -used % PROJ_COLS), bf16)
    return jnp.concatenate(heads + [w[..., kv0:], w[..., g0:q0], pad], axis=-1)


def _pair_rows(w_out_b):
    r0 = ML_WIDTH
    return jnp.concatenate(
        [w_out_b[:, r0 + h * SWA_HEAD_DIM:r0 + (h + 1) * SWA_HEAD_DIM] for h in PAIR_ORDER], axis=1)


def kernel(x, mem, rel_bias, w_in, ml_conv_w, ml_conv_b, ml_i_bias, ml_f_bias, ml_norm_g, swa_sinks, w_out, ln1_g, ln1_b, xa_wq, xa_wkv, xa_wo, ln2_g, ln2_b, ffn_w_up, ffn_conv_w, ffn_conv_b, ffn_w_down, ln3_g, ln3_b):
    depth = w_in.shape[0]
    assert depth == DEPTH, "ALPHA is the DeepNorm constant of a DEPTH-layer trunk"
    rows = lambda a: a.reshape(depth, 1, -1)
    w_in_b, w_out_b = w_in.astype(bf16), w_out.astype(bf16)
    wswg, wo_sw = _swa_gate_columns(w_in_b), _pair_rows(w_out_b)
    wq, wkv, wo = xa_wq.astype(bf16), xa_wkv.astype(bf16), xa_wo.astype(bf16)
    wup, wdn = ffn_w_up.astype(bf16), ffn_w_down.astype(bf16)
    gbias = jnp.concatenate([ml_i_bias, ml_f_bias], axis=1).reshape(depth, 2 * ML_HEADS, 1)
    small = [rows(a) for a in (ml_conv_b, ln1_g, ln1_b, ln2_g, ln2_b, ffn_conv_b, ln3_g, ln3_b)]
    norm_gt = jnp.broadcast_to(ml_norm_g[:, :, None], ml_norm_g.shape + (LANES,))
    bias = _swa_bias(rel_bias)
    for l in range(depth):
        at = lambda a: _Layer(a, l)
        conv_b, g1, b1, g2, b2, ffn_cb, g3, b3 = map(at, small)
        wqk, wvo = (_Layer(w_in_b, l, axis=1, size=2 * ML_WIDTH, index=i) for i in (0, 1))
        wo_ml = _Layer(w_out_b, l, axis=0, size=ML_WIDTH, index=0)
        qk, vot, rows, cols, sw = _inproj(x, wqk, wvo, at(wswg), at(ml_conv_w), conv_b, at(gbias))
        hml = _mlstm(qk, vot, rows, cols, at(norm_gt))
        hsw = _swa(sw, bias, swa_sinks, l)
        k, v = _memkv(mem, at(wkv))
        x = _mix(hml, hsw, x, wo_ml, at(wo_sw), g1, b1, at(wq), k, v, at(wo), g2, b2)
        x = _ffn(x, at(wup), at(ffn_conv_w), ffn_cb, at(wdn), g3, b3)
    return x
```

```python
import functools
import math

import numpy as np
import jax
import jax.numpy as jnp
from jax import lax
from jax.experimental import pallas as pl
from jax.experimental.pallas import tpu as pltpu

f32 = jnp.float32
bf16 = jnp.bfloat16

ML_HEADS = 4
ML_HEAD_DIM = 128
ML_WIDTH = ML_HEADS * ML_HEAD_DIM
ML_CONV = 4
SWA_HEADS = 8
SWA_KV_HEADS = 2
SWA_GROUP = SWA_HEADS // SWA_KV_HEADS
SWA_HEAD_DIM = 64
SWA_WIDTH = SWA_HEADS * SWA_HEAD_DIM
WINDOW = 128
BLOCK = 128
REL_BUCKETS = 32
REL_MAX_DIST = 128
XA_HEADS = 4
FFN_CONV = 3
DEPTH = 2
ALPHA = (2.0 * DEPTH) ** 0.25
EPS = 1e-5

LANES = 128
SUBLANES = 8
TM_PROJ = 1024
PROJ_ROWS = 128
PROJ_COLS = 256
ML_CHUNK = 512
ML_STEP_CHUNKS = 2
TQ_SWA = 4096
TM_MEMKV = 1024
TM_MIX = 1024
MIX_STREAMS = 4
TM_FFN = 512
FF_CHUNK = 256
FF_ROWS = 64
VMEM_LIMIT = 56 * 1024 * 1024
NEG_BIG = -1e30


def _params(n_axes, flags=None):
    return pltpu.CompilerParams(
        dimension_semantics=("arbitrary",) * n_axes, vmem_limit_bytes=VMEM_LIMIT, flags=flags)


class _Layer:
    def __init__(self, stacked, l, axis=None, size=None, index=0):
        self.stacked, self.l, self.axis, self.size, self.index = stacked, l, axis, size, index

    @property
    def shape(self):
        shape = list(self.stacked.shape[1:])
        if self.axis is not None:
            shape[self.axis] = self.size
        return tuple(shape)

    @property
    def block_index(self):
        idx = [0] * (self.stacked.ndim - 1)
        if self.axis is not None:
            idx[self.axis] = self.index
        return (self.l,) + tuple(idx)


def _arr(op):
    return op.stacked if isinstance(op, _Layer) else op


def _const_spec(op):
    if isinstance(op, _Layer):
        index = op.block_index
        return pl.BlockSpec((None,) + op.shape, lambda *_: index, pipeline_mode=pl.Buffered(1))
    nd = len(op.shape)
    return pl.BlockSpec(op.shape, lambda *_: (0,) * nd, pipeline_mode=pl.Buffered(1))


def _layer_norm(z, g, b):
    mu = jnp.mean(z, axis=-1, keepdims=True)
    zc = z - mu
    var = jnp.mean(zc * zc, axis=-1, keepdims=True)
    return zc * lax.rsqrt(var + EPS) * g + b


def _inproj_kernel(x_ref, wqk_ref, wvo_ref, wswg_ref, cw_ref, cb_ref, gb_ref,
                   qk_ref, vot_ref, rows_ref, cols_ref, sw_ref, xb_ref, p_ref, *, chunk):
    tm = x_ref.shape[1]
    n_slabs = p_ref.shape[0]
    halo = SUBLANES

    @pl.when(pl.program_id(1) == 0)
    def _():
        p_ref[:, 0:halo, :] = jnp.zeros((n_slabs, halo, LANES), f32)

    xb_ref[...] = x_ref[0].astype(bf16)

    def project(w_ref, c0):
        return jnp.dot(xb_ref[...], w_ref[:, c0:c0 + PROJ_COLS], preferred_element_type=f32)

    def qk_piece(c0):
        acc = project(wqk_ref, c0)
        for li in range(PROJ_COLS // LANES):
            sl = c0 // LANES + li
            lanes = slice(sl * LANES, (sl + 1) * LANES)
            p_ref[sl, halo:halo + tm, :] = acc[:, li * LANES:(li + 1) * LANES]
            taps = [cw_ref[j:j + 1, lanes] for j in range(ML_CONV)]
            for r0 in range(0, tm, PROJ_ROWS):
                y = cb_ref[:, lanes]
                for j in range(ML_CONV):
                    first = r0 + halo - (ML_CONV - 1) + j
                    y = y + p_ref[sl, first:first + PROJ_ROWS, :] * taps[j]
                hy = 0.5 * y
                qk_ref[0, r0:r0 + PROJ_ROWS, lanes] = (hy + hy * jnp.tanh(hy)).astype(bf16)
            p_ref[sl, 0:halo, :] = p_ref[sl, tm:tm + halo, :]

    def vo_piece(c0):
        acc = project(wvo_ref, c0)
        for li in range(PROJ_COLS // LANES):
            lanes = slice(c0 + li * LANES, c0 + (li + 1) * LANES)
            vot_ref[0, lanes, :] = acc[:, li * LANES:(li + 1) * LANES].T.astype(bf16)

    def swg_piece(c0):
        sw_w = sw_ref.shape[2]
        acc = project(wswg_ref, c0)
        if c0 + PROJ_COLS <= sw_w:
            sw_ref[0, :, c0:c0 + PROJ_COLS] = acc.astype(bf16)
        else:
            gates = acc[:, sw_w - c0:sw_w - c0 + LANES].T[0:SUBLANES, :]
            _gate_stats(gates + gb_ref[...], rows_ref, cols_ref, chunk)

    swg = [functools.partial(swg_piece, c0) for c0 in range(0, wswg_ref.shape[1], PROJ_COLS)]
    swg.pop()()
    others = [functools.partial(vo_piece, c0) for c0 in range(0, wvo_ref.shape[1], PROJ_COLS)] + swg
    n_qk = wqk_ref.shape[1] // PROJ_COLS
    per_qk = -(-len(others) // n_qk)
    for i in range(n_qk):
        qk_piece(i * PROJ_COLS)
        for piece in others[i * per_qk:(i + 1) * per_qk]:
            piece()


def _inproj(x, wqk, wvo, wswg, cw, cb, gbias):
    B, S, D = x.shape
    tm = min(TM_PROJ, S)
    chunk = min(ML_CHUNK, S)
    assert tm % chunk == 0
    grid = (B, S // tm)
    sw_w = SWA_WIDTH + 4 * SWA_HEAD_DIM
    tok = lambda w: pl.BlockSpec((1, tm, w), lambda b, s: (b, s, 0))
    return pl.pallas_call(
        functools.partial(_inproj_kernel, chunk=chunk),
        grid=grid,
        in_specs=[tok(D)] + [_const_spec(a) for a in (wqk, wvo, wswg, cw, cb, gbias)],
        out_specs=[tok(wqk.shape[1]),
                   pl.BlockSpec((1, wvo.shape[1], tm), lambda b, s: (b, 0, s)),
                   pl.BlockSpec((1, 3 * SUBLANES, tm), lambda b, s: (b, 0, s)),
                   tok(LANES),
                   tok(sw_w)],
        out_shape=[jax.ShapeDtypeStruct((B, S, wqk.shape[1]), bf16),
                   jax.ShapeDtypeStruct((B, wvo.shape[1], S), bf16),
                   jax.ShapeDtypeStruct((B, 3 * SUBLANES, S), f32),
                   jax.ShapeDtypeStruct((B, S, LANES), f32),
                   jax.ShapeDtypeStruct((B, S, sw_w), bf16)],
        scratch_shapes=[pltpu.VMEM((tm, D), bf16),
                        pltpu.VMEM((wqk.shape[1] // LANES, tm + SUBLANES, LANES), f32)],
        compiler_params=_params(2),
        name="inproj",
    )(x, *map(_arr, (wqk, wvo, wswg, cw, cb, gbias)))


C_ROW, CM_ROW, B_ROW = 0, SUBLANES, 2 * SUBLANES


def _scan_chunks(x, op, fill, chunk):
    pos = lax.broadcasted_iota(jnp.int32, x.shape, 1) % chunk
    sh = 1
    while sh < chunk:
        x = op(x, jnp.where(pos >= sh, pltpu.roll(x, sh, 1), fill))
        sh *= 2
    return x


def _gate_stats(g, rows_ref, cols_ref, chunk):
    ig = jnp.concatenate([g[0:4], g[0:4]], axis=0)
    fg = jnp.concatenate([g[4:8], g[4:8]], axis=0)
    lf = jnp.minimum(fg, 0.0) - jnp.log1p(jnp.exp(-jnp.abs(fg)))
    b = _scan_chunks(lf, jnp.add, 0.0, chunk)
    c = ig - b
    cm = _scan_chunks(c, jnp.maximum, -jnp.inf, chunk)
    rows_ref[0] = jnp.concatenate([c, cm, b], axis=0)
    stack = jnp.concatenate([c, jnp.zeros((LANES - SUBLANES, c.shape[1]), f32)], axis=0)
    cols_ref[0] = stack.T


def _mlstm_kernel(qk_ref, vot_ref, rows_ref, cols_ref, ngt_ref, out_ref, ct_ref, m_ref):
    L = ML_CHUNK if qk_ref.shape[1] % ML_CHUNK == 0 else qk_ref.shape[1]

    @pl.when(pl.program_id(1) == 0)
    def _():
        ct_ref[...] = jnp.zeros(ct_ref.shape, f32)
        m_ref[...] = jnp.zeros(m_ref.shape, f32)

    heads = range(ML_HEADS)
    m_all = m_ref[...]
    state = ([ct_ref[h] for h in heads], [m_all[h:h + 1, 0:1] for h in heads])
    for c in range(qk_ref.shape[1] // L):
        state = _mlstm_chunk(slice(c * L, (c + 1) * L), state, qk_ref, vot_ref, rows_ref, cols_ref,
                             ngt_ref, out_ref)
    for h in heads:
        ct_ref[h] = state[0][h]
    m_ref[0:ML_HEADS, :] = jnp.concatenate(
        [jnp.broadcast_to(m, (1, LANES)) for m in state[1]], axis=0)


def _mlstm_chunk(ts, state, qk_ref, vot_ref, rows_ref, cols_ref, ngt_ref, out_ref):
    ct, m_prev = state
    L = ts.stop - ts.start
    dh = ML_HEAD_DIM
    scale = dh ** -0.5
    reps = L // dh
    nt = (((1,), (1,)), ((), ()))
    half = L // 2
    src = lax.broadcasted_iota(jnp.int32, (L, half), 0)
    tgt = lax.broadcasted_iota(jnp.int32, (L, half), 1)
    causal = [(src <= tgt)[:half], src <= tgt + half]
    ones_t = jnp.ones((dh, L), bf16)
    heads = range(ML_HEADS)
    hs = lambda h: slice(h * dh, (h + 1) * dh)
    ks = lambda h: slice(ML_WIDTH + h * dh, ML_WIDTH + (h + 1) * dh)
    q = [qk_ref[0, ts, hs(h)] for h in heads]
    k = [qk_ref[0, ts, ks(h)] for h in heads]

    tcols = [slice(0, half), slice(half, L)]
    srows = [slice(0, half), slice(0, L)]
    s_t = [[lax.dot_general(k[h][srows[j]], q[h][tcols[j]], nt,
                            preferred_element_type=f32) for j in range(2)]
           for h in heads]
    inter = [lax.dot_general(ct[h].astype(bf16), q[h], nt, preferred_element_type=f32)
             for h in heads]

    c_row, cm_row, v_aug, intra = [], [], [], []
    for h in heads:
        c_row.append(rows_ref[0, C_ROW + h:C_ROW + h + 1, ts])
        cm_row.append(rows_ref[0, CM_ROW + h:CM_ROW + h + 1, ts])
        c_b = jnp.broadcast_to(cols_ref[0, ts, h:h + 1], (L, dh))
        c_wide = jnp.concatenate([c_b] * (half // dh), axis=1) + math.log(scale)
        v_aug.append(jnp.concatenate([vot_ref[0, hs(h), ts], ones_t], axis=0))
        parts = []
        for j in range(2):
            p = jnp.where(causal[j], jnp.exp(c_wide[srows[j]] - cm_row[h][:, tcols[j]]), 0.0)
            s_mat = (s_t[h][j] * p).astype(bf16)
            parts.append(jnp.dot(v_aug[h][:, srows[j]], s_mat, preferred_element_type=f32))
        intra.append(jnp.concatenate(parts, axis=1))

    big_m, m_last, vw = [], [], []
    for h in heads:
        big_m.append(jnp.maximum(m_prev[h], cm_row[h]))
        m_last.append(big_m[h][:, L - 1:L])
        wk = jnp.exp(c_row[h] - m_last[h]) * scale
        vw.append((v_aug[h].astype(f32) * wk).astype(bf16))
    upd = [jnp.dot(vw[h], k[h], preferred_element_type=f32) for h in heads]

    ct_new, m_new = [], []
    for h in heads:
        b_row = rows_ref[0, B_ROW + h:B_ROW + h + 1, ts]
        w_inter = jnp.exp(m_prev[h] - big_m[h])
        w_intra = jnp.exp(cm_row[h] - big_m[h])
        nd = w_inter * inter[h] + w_intra * intra[h]
        den = nd[dh:dh + 1, :]
        inv = 1.0 / jnp.maximum(jnp.abs(den), jnp.exp(-(b_row + big_m[h])))
        hh = nd[:dh, :] * inv
        mu = jnp.mean(hh, axis=0, keepdims=True)
        hc = hh - mu
        var = jnp.mean(hc * hc, axis=0, keepdims=True)
        ng = jnp.concatenate([ngt_ref[hs(h), :]] * reps, axis=1)
        hn = hc * lax.rsqrt(var + EPS) * ng
        gate = jax.nn.sigmoid(vot_ref[0, ks(h), ts].astype(f32))
        out_ref[0, ts, hs(h)] = (gate * hn).T.astype(bf16)
        ct_new.append(jnp.exp(m_prev[h] - m_last[h]) * ct[h] + upd[h])
        m_new.append(b_row[:, L - 1:L] + m_last[h])
    return ct_new, m_new


def _mlstm(qk, vot, rows, cols, norm_gt):
    B, S, _ = qk.shape
    chunk = min(ML_CHUNK, S)
    L = min(ML_STEP_CHUNKS * chunk, S)
    grid = (B, S // L)
    return pl.pallas_call(
        _mlstm_kernel,
        grid=grid,
        in_specs=[pl.BlockSpec((1, L, 2 * ML_WIDTH), lambda b, s: (b, s, 0)),
                  pl.BlockSpec((1, 2 * ML_WIDTH, L), lambda b, s: (b, 0, s)),
                  pl.BlockSpec((1, 3 * SUBLANES, L), lambda b, s: (b, 0, s)),
                  pl.BlockSpec((1, L, LANES), lambda b, s: (b, s, 0)),
                  _const_spec(norm_gt)],
        out_specs=pl.BlockSpec((1, L, ML_WIDTH), lambda b, s: (b, s, 0)),
        out_shape=jax.ShapeDtypeStruct((B, S, ML_WIDTH), bf16),
        scratch_shapes=[pltpu.VMEM((ML_HEADS, 2 * ML_HEAD_DIM, ML_HEAD_DIM), f32),
                        pltpu.VMEM((SUBLANES, LANES), f32)],
        compiler_params=_params(2),
        name="mlstm",
    )(qk, vot, rows, cols, _arr(norm_gt))


def _t5_bucket_table():
    assert WINDOW == BLOCK
    r = np.arange(BLOCK)[:, None]
    c = np.arange(BLOCK)[None, :]
    n = np.where(c > r, BLOCK + r - c, r - c)
    max_exact = REL_BUCKETS // 2
    nf = np.maximum(n, 1).astype(np.float32)
    large = max_exact + (np.log(nf / np.float32(max_exact)) / np.float32(math.log(REL_MAX_DIST / max_exact))
                         * np.float32(REL_BUCKETS - max_exact)).astype(np.int32)
    large = np.minimum(large, REL_BUCKETS - 1)
    return np.where(n < max_exact, n, large).astype(np.int32)


def _bias_kernel(bucket_ref, rb_ref, out_ref):
    bucket = bucket_ref[...]
    for h in range(SWA_HEADS):
        acc = jnp.zeros(bucket.shape, f32)
        for i in range(REL_BUCKETS):
            acc = jnp.where(bucket == i, rb_ref[i, h], acc)
        out_ref[h] = acc


def _swa_bias(rel_bias):
    bucket = jnp.asarray(_t5_bucket_table())
    return pl.pallas_call(
        _bias_kernel,
        in_specs=[pl.BlockSpec(memory_space=pltpu.VMEM), pl.BlockSpec(memory_space=pltpu.SMEM)],
        out_specs=pl.BlockSpec(memory_space=pltpu.VMEM),
        out_shape=jax.ShapeDtypeStruct((SWA_HEADS, BLOCK, BLOCK), f32),
        name="swa_bias",
    )(bucket, rel_bias)


def _swa_kernel(sw_ref, prev_ref, bias_ref, sink_ref, out_ref, *, layer):
    tq = sw_ref.shape[1]
    scale = SWA_HEAD_DIM ** -0.5
    pair = 2 * SWA_HEAD_DIM
    kc = slice(SWA_WIDTH, SWA_WIDTH + pair)
    vc = slice(SWA_WIDTH + pair, SWA_WIDTH + 2 * pair)
    first_tile = pl.program_id(1) == 0
    lo = lax.broadcasted_iota(jnp.int32, (BLOCK, pair), 1) < SWA_HEAD_DIM
    use_prev = (lax.broadcasted_iota(jnp.int32, (BLOCK, BLOCK), 1)
                > lax.broadcasted_iota(jnp.int32, (BLOCK, BLOCK), 0))
    zero_p = jnp.zeros((BLOCK, BLOCK), bf16)

    for j in range(tq // BLOCK):
        rows = slice(j * BLOCK, (j + 1) * BLOCK)
        prows = slice((j - 1) * BLOCK, j * BLOCK)
        if j == 0:
            k_prev, v_prev = prev_ref[0, :, 0:pair], prev_ref[0, :, pair:2 * pair]
        else:
            k_prev, v_prev = sw_ref[0, prows, kc], sw_ref[0, prows, vc]
        kband = jnp.concatenate([k_prev, sw_ref[0, rows, kc]], axis=0)
        vband = jnp.concatenate([v_prev, sw_ref[0, rows, vc]], axis=0)
        qs, heads = [], []
        for p in range(SWA_GROUP):
            qp = sw_ref[0, rows, p * pair:(p + 1) * pair] * scale
            zero = jnp.zeros_like(qp)
            qs += [jnp.where(lo, qp, zero), jnp.where(lo, zero, qp)]
            heads += [p, p + SWA_GROUP]
        logits = lax.dot_general(jnp.concatenate(qs, axis=0), kband, (((1,), (1,)), ((), ())),
                                 preferred_element_type=f32)
        ps, dens = [], []
        for n, head in enumerate(heads):
            lg2 = logits[n * BLOCK:(n + 1) * BLOCK]
            lg = jnp.where(use_prev, lg2[:, :BLOCK], lg2[:, BLOCK:]) + bias_ref[head]
            if j == 0:
                lg = jnp.where(jnp.logical_and(first_tile, use_prev), NEG_BIG, lg)
            sink = sink_ref[layer, head]
            mx = jnp.maximum(jnp.max(lg, axis=-1, keepdims=True), sink)
            pr = jnp.exp(lg - mx)
            dens.append(jnp.sum(pr, axis=-1, keepdims=True) + jnp.exp(sink - mx))
            pb = pr.astype(bf16)
            ps.append(jnp.concatenate([jnp.where(use_prev, pb, zero_p),
                                       jnp.where(use_prev, zero_p, pb)], axis=1))
        pv = jnp.dot(jnp.concatenate(ps, axis=0), vband, preferred_element_type=f32)
        o = [pv[n * BLOCK:(n + 1) * BLOCK] / dens[n] for n in range(len(heads))]
        for p in range(SWA_GROUP):
            out_ref[0, rows, p * pair:(p + 1) * pair] = jnp.where(lo, o[2 * p], o[2 * p + 1]).astype(bf16)


def _swa(sw, bias, sinks, layer):
    B, S, W = sw.shape
    tq = min(TQ_SWA, S)
    nblk = tq // BLOCK
    kvw = W - SWA_WIDTH
    assert kvw == 4 * SWA_HEAD_DIM and SWA_WIDTH % kvw == 0, "k/v group = [k0|k1|v0|v1]"
    return pl.pallas_call(
        functools.partial(_swa_kernel, layer=layer),
        grid=(B, S // tq),
        in_specs=[pl.BlockSpec((1, tq, W), lambda b, s: (b, s, 0)),
                  pl.BlockSpec((1, BLOCK, kvw),
                               lambda b, s: (b, jnp.maximum(s * nblk - 1, 0), SWA_WIDTH // kvw)),
                  _const_spec(bias),
                  pl.BlockSpec(memory_space=pltpu.SMEM)],
        out_specs=pl.BlockSpec((1, tq, SWA_WIDTH), lambda b, s: (b, s, 0)),
        out_shape=jax.ShapeDtypeStruct((B, S, SWA_WIDTH), bf16),
        compiler_params=_params(2),
        name="swa",
    )(sw, sw, bias, sinks)


def _memkv_kernel(mem_ref, wkv_ref, k_ref, v_ref):
    d = k_ref.shape[1]
    mb = mem_ref[...].astype(bf16)
    for c0 in range(0, 2 * d, PROJ_COLS):
        piece = jnp.dot(mb, wkv_ref[:, c0:c0 + PROJ_COLS], preferred_element_type=f32).astype(bf16)
        if c0 < d:
            k_ref[:, c0:c0 + PROJ_COLS] = piece
        else:
            v_ref[:, c0 - d:c0 - d + PROJ_COLS] = piece


def _memkv(mem, wkv):
    B, M, D = mem.shape
    rows = B * M
    tm = math.gcd(rows, TM_MEMKV)
    blk = pl.BlockSpec((tm, D), lambda i: (i, 0))
    k, v = pl.pallas_call(
        _memkv_kernel,
        grid=(rows // tm,),
        in_specs=[blk, _const_spec(wkv)],
        out_specs=[blk, blk],
        out_shape=[jax.ShapeDtypeStruct((rows, D), bf16)] * 2,
        compiler_params=_params(1),
        name="memkv",
    )(mem.reshape(rows, D), _arr(wkv))
    return k.reshape(B, M, D), v.reshape(B, M, D)


def _mix_kernel(hml_ref, hsw_ref, x_ref, woml_ref, wosw_ref, g1_ref, b1_ref,
                wq_ref, k_ref, v_ref, wo_ref, g2_ref, b2_ref, out_ref):
    tm, d = x_ref.shape[1], x_ref.shape[2]
    dh = d // XA_HEADS
    rows = [slice(i * tm // MIX_STREAMS, (i + 1) * tm // MIX_STREAMS) for i in range(MIX_STREAMS)]
    streams = range(MIX_STREAMS)
    cs = lambda a: slice(a * dh, (a + 1) * dh)

    h = [jnp.dot(hml_ref[0, r, :], woml_ref[...], preferred_element_type=f32)
         + jnp.dot(hsw_ref[0, r, :], wosw_ref[...], preferred_element_type=f32) for r in rows]
    x1, q = [], []
    for i in streams:
        x1.append(_layer_norm(ALPHA * x_ref[0, rows[i], :] + h[i], g1_ref[...], b1_ref[...]))
        q.append(jnp.dot(x1[i].astype(bf16), wq_ref[...], preferred_element_type=f32).astype(bf16))

    items = [(a, i) for a in range(XA_HEADS) for i in streams]
    lag = MIX_STREAMS
    prob, den, outs = {}, {}, {}

    def logits(a, i):
        lg = lax.dot_general(q[i][:, cs(a)], k_ref[0, :, cs(a)], (((1,), (1,)), ((), ())),
                             preferred_element_type=f32) * (dh ** -0.5)
        p = jnp.exp(lg - jnp.max(lg, axis=-1, keepdims=True))
        den[a, i] = jnp.sum(p, axis=-1, keepdims=True)
        prob[a, i] = p.astype(bf16)

    def values(a, i):
        o = jnp.dot(prob[a, i], v_ref[0, :, cs(a)], preferred_element_type=f32) / den[a, i]
        outs[a, i] = o.astype(bf16)

    for n, item in enumerate(items):
        logits(*item)
        if n >= lag:
            values(*items[n - lag])
    for item in items[-lag:]:
        values(*item)

    h2 = [jnp.dot(jnp.concatenate([outs[a, i] for a in range(XA_HEADS)], axis=1), wo_ref[...],
                  preferred_element_type=f32) for i in streams]
    for i in streams:
        out_ref[0, rows[i], :] = _layer_norm(ALPHA * x1[i] + h2[i], g2_ref[...], b2_ref[...])


def _mix(hml, hsw, x, woml, wosw, g1, b1, wq, k, v, wo, g2, b2):
    B, S, D = x.shape
    tm = min(TM_MIX, S)
    M = k.shape[1]
    tok = lambda w: pl.BlockSpec((1, tm, w), lambda b, s: (b, s, 0))
    memblk = pl.BlockSpec((1, M, D), lambda b, s: (b, 0, 0))
    consts = [woml, wosw, g1, b1, wq]
    return pl.pallas_call(
        _mix_kernel,
        grid=(B, S // tm),
        in_specs=[tok(hml.shape[2]), tok(hsw.shape[2]), tok(D)]
                 + [_const_spec(a) for a in consts]
                 + [memblk, memblk] + [_const_spec(a) for a in (wo, g2, b2)],
        out_specs=tok(D),
        out_shape=jax.ShapeDtypeStruct((B, S, D), f32),
        compiler_params=_params(2),
        name="mix",
    )(hml, hsw, x, *map(_arr, consts), k, v, *map(_arr, (wo, g2, b2)))


GELU_C0 = math.sqrt(2.0 / math.pi)
GELU_C1 = 0.044715 * GELU_C0


def _ffn_kernel(x_ref, wup_ref, cw_ref, cb_ref, wdn_ref, g_ref, b_ref, out_ref,
                carry_ref, xb_ref, u00_ref, u01_ref, u10_ref, u11_ref, h_ref):
    tm = x_ref.shape[1]
    d_ff = wdn_ref.shape[0]
    cwid = FF_CHUNK
    n_chunks = d_ff // cwid
    n_slabs = cwid // LANES
    halo = SUBLANES
    u_refs = ((u00_ref, u01_ref), (u10_ref, u11_ref))

    @pl.when(pl.program_id(1) == 0)
    def _():
        carry_ref[...] = jnp.zeros(carry_ref.shape, f32)

    xb_ref[...] = x_ref[0].astype(bf16)

    def up(ci):
        for u_ref, c0 in zip(u_refs[ci % 2], (ci * cwid, d_ff + ci * cwid)):
            u = jnp.dot(xb_ref[...], wup_ref[:, c0:c0 + cwid], preferred_element_type=f32)
            for sl in range(n_slabs):
                cols = slice(c0 + sl * LANES, c0 + (sl + 1) * LANES)
                u_ref[sl, 0:halo, :] = carry_ref[:, cols]
                u_ref[sl, halo:halo + tm, :] = u[:, sl * LANES:(sl + 1) * LANES]
                carry_ref[:, cols] = u_ref[sl, tm:tm + halo, :]

    def act(ci):
        gu_ref, vu_ref = u_refs[ci % 2]
        for sl in range(n_slabs):
            gc = slice(ci * cwid + sl * LANES, ci * cwid + (sl + 1) * LANES)
            vc = slice(d_ff + gc.start, d_ff + gc.stop)
            gb, gw = cb_ref[:, gc], [cw_ref[j:j + 1, gc] for j in range(FFN_CONV)]
            vb, vw = 0.5 * cb_ref[:, vc], [0.5 * cw_ref[j:j + 1, vc] for j in range(FFN_CONV)]
            for r0 in range(0, tm, FF_ROWS):
                g, v = gb, vb
                for j in range(FFN_CONV):
                    rows = slice(r0 + halo - (FFN_CONV - 1) + j, r0 + halo - (FFN_CONV - 1) + j + FF_ROWS)
                    g = g + gu_ref[sl, rows, :] * gw[j]
                    v = v + vu_ref[sl, rows, :] * vw[j]
                t = jnp.tanh(g * (GELU_C0 + GELU_C1 * (g * g)))
                h_ref[r0:r0 + FF_ROWS, gc] = ((g * v) * (1.0 + t)).astype(bf16)

    up(0)
    for ci in range(n_chunks - 1):
        up(ci + 1)
        act(ci)
    top, bot = slice(0, tm // 2), slice(tm // 2, tm)
    k1 = (n_chunks - 1) * cwid
    h_top = jnp.dot(h_ref[top, :k1], wdn_ref[:k1, :], preferred_element_type=f32)
    act(n_chunks - 1)
    h_top = h_top + jnp.dot(h_ref[top, k1:], wdn_ref[k1:, :], preferred_element_type=f32)
    h_bot = jnp.dot(h_ref[bot, :], wdn_ref[...], preferred_element_type=f32)
    out_ref[0, top, :] = _layer_norm(ALPHA * x_ref[0, top, :] + h_top, g_ref[...], b_ref[...])
    out_ref[0, bot, :] = _layer_norm(ALPHA * x_ref[0, bot, :] + h_bot, g_ref[...], b_ref[...])


def _ffn(x, wup, cw, cb, wdn, g, b):
    B, S, D = x.shape
    tm = min(TM_FFN, S)
    cwid = FF_CHUNK
    assert wdn.shape[0] % cwid == 0 and wup.shape[1] == 2 * wdn.shape[0]
    tok = pl.BlockSpec((1, tm, D), lambda b_, s: (b_, s, 0))
    return pl.pallas_call(
        _ffn_kernel,
        grid=(B, S // tm),
        in_specs=[tok] + [_const_spec(a) for a in (wup, cw, cb, wdn, g, b)],
        out_specs=tok,
        out_shape=jax.ShapeDtypeStruct((B, S, D), f32),
        scratch_shapes=[pltpu.VMEM((SUBLANES, wup.shape[1]), f32),
                        pltpu.VMEM((tm, D), bf16),
                        ]
                       + [pltpu.VMEM((cwid // LANES, tm + SUBLANES, LANES), f32)] * 4
                       + [pltpu.VMEM((tm, wdn.shape[0]), bf16)],
        compiler_params=_params(2),
        name="ffn",
    )(x, *map(_arr, (wup, cw, cb, wdn, g, b)))


PAIR_ORDER = [h for p in range(SWA_GROUP) for h in (p, p + SWA_GROUP)]


def _swa_gate_columns(w):
    depth, D = w.shape[:2]
    g0 = 4 * ML_WIDTH
    q0 = g0 + 2 * ML_HEADS
    kv0 = q0 + SWA_WIDTH
    heads = [w[..., q0 + h * SWA_HEAD_DIM:q0 + (h + 1) * SWA_HEAD_DIM] for h in PAIR_ORDER]
    used = SWA_WIDTH + (w.shape[-1] - kv0) + 2 * ML_HEADS
    pad = jnp.zeros((depth, D, -used % PROJ_COLS), bf16)
    return jnp.concatenate(heads + [w[..., kv0:], w[..., g0:q0], pad], axis=-1)


def _pair_rows(w_out_b):
    r0 = ML_WIDTH
    return jnp.concatenate(
        [w_out_b[:, r0 + h * SWA_HEAD_DIM:r0 + (h + 1) * SWA_HEAD_DIM] for h in PAIR_ORDER], axis=1)


def kernel(x, mem, rel_bias, w_in, ml_conv_w, ml_conv_b, ml_i_bias, ml_f_bias, ml_norm_g, swa_sinks, w_out, ln1_g, ln1_b, xa_wq, xa_wkv, xa_wo, ln2_g, ln2_b, ffn_w_up, ffn_conv_w, ffn_conv_b, ffn_w_down, ln3_g, ln3_b):
    depth = w_in.shape[0]
    assert depth == DEPTH, "ALPHA is the DeepNorm constant of a DEPTH-layer trunk"
    rows = lambda a: a.reshape(depth, 1, -1)
    w_in_b, w_out_b = w_in.astype(bf16), w_out.astype(bf16)
    wswg, wo_sw = _swa_gate_columns(w_in_b), _pair_rows(w_out_b)
    wq, wkv, wo = xa_wq.astype(bf16), xa_wkv.astype(bf16), xa_wo.astype(bf16)
    wup, wdn = ffn_w_up.astype(bf16), ffn_w_down.astype(bf16)
    gbias = jnp.concatenate([ml_i_bias, ml_f_bias], axis=1).reshape(depth, 2 * ML_HEADS, 1)
    small = [rows(a) for a in (ml_conv_b, ln1_g, ln1_b, ln2_g, ln2_b, ffn_conv_b, ln3_g, ln3_b)]
    norm_gt = jnp.broadcast_to(ml_norm_g[:, :, None], ml_norm_g.shape + (LANES,))
    bias = _swa_bias(rel_bias)
    for l in range(depth):
        at = lambda a: _Layer(a, l)
        conv_b, g1, b1, g2, b2, ffn_cb, g3, b3 = map(at, small)
        wqk, wvo = (_Layer(w_in_b, l, axis=1, size=2 * ML_WIDTH, index=i) for i in (0, 1))
        wo_ml = _Layer(w_out_b, l, axis=0, size=ML_WIDTH, index=0)
        qk, vot, rows, cols, sw = _inproj(x, wqk, wvo, at(wswg), at(ml_conv_w), conv_b, at(gbias))
        hml = _mlstm(qk, vot, rows, cols, at(norm_gt))
        hsw = _swa(sw, bias, swa_sinks, l)
        k, v = _memkv(mem, at(wkv))
        x = _mix(hml, hsw, x, wo_ml, at(wo_sw), g1, b1, at(wq), k, v, at(wo), g2, b2)
        x = _ffn(x, at(wup), at(ffn_conv_w), ffn_cb, at(wdn), g3, b3)
    return x
```

```python
import functools
import math

import numpy as np
import jax
import jax.numpy as jnp
from jax import lax
from jax.experimental import pallas as pl
from jax.experimental.pallas import tpu as pltpu

f32 = jnp.float32
bf16 = jnp.bfloat16

ML_HEADS = 4
ML_HEAD_DIM = 128
ML_WIDTH = ML_HEADS * ML_HEAD_DIM
ML_CONV = 4
SWA_HEADS = 8
SWA_KV_HEADS = 2
SWA_GROUP = SWA_HEADS // SWA_KV_HEADS
SWA_HEAD_DIM = 64
SWA_WIDTH = SWA_HEADS * SWA_HEAD_DIM
WINDOW = 128
BLOCK = 128
REL_BUCKETS = 32
REL_MAX_DIST = 128
XA_HEADS = 4
FFN_CONV = 3
DEPTH = 2
ALPHA = (2.0 * DEPTH) ** 0.25
EPS = 1e-5

LANES = 128
SUBLANES = 8
TM_PROJ = 1024
PROJ_ROWS = 128
PROJ_COLS = 256
ML_CHUNK = 256
ML_STEP_CHUNKS = 4
TQ_SWA = 4096
TM_MEMKV = 1024
TM_MIX = 1024
MIX_STREAMS = 4
TM_FFN = 512
FF_CHUNK = 256
FF_ROWS = 64
VMEM_LIMIT = 56 * 1024 * 1024
NEG_BIG = -1e30


def _params(n_axes, flags=None):
    return pltpu.CompilerParams(
        dimension_semantics=("arbitrary",) * n_axes, vmem_limit_bytes=VMEM_LIMIT, flags=flags)


class _Layer:
    def __init__(self, stacked, l, axis=None, size=None, index=0):
        self.stacked, self.l, self.axis, self.size, self.index = stacked, l, axis, size, index

    @property
    def shape(self):
        shape = list(self.stacked.shape[1:])
        if self.axis is not None:
            shape[self.axis] = self.size
        return tuple(shape)

    @property
    def block_index(self):
        idx = [0] * (self.stacked.ndim - 1)
        if self.axis is not None:
            idx[self.axis] = self.index
        return (self.l,) + tuple(idx)


def _arr(op):
    return op.stacked if isinstance(op, _Layer) else op


def _const_spec(op):
    if isinstance(op, _Layer):
        index = op.block_index
        return pl.BlockSpec((None,) + op.shape, lambda *_: index, pipeline_mode=pl.Buffered(1))
    nd = len(op.shape)
    return pl.BlockSpec(op.shape, lambda *_: (0,) * nd, pipeline_mode=pl.Buffered(1))


def _layer_norm(z, g, b):
    mu = jnp.mean(z, axis=-1, keepdims=True)
    zc = z - mu
    var = jnp.mean(zc * zc, axis=-1, keepdims=True)
    return zc * lax.rsqrt(var + EPS) * g + b


def _inproj_kernel(x_ref, wqk_ref, wvo_ref, wswg_ref, cw_ref, cb_ref, gb_ref,
                   qk_ref, vot_ref, rows_ref, cols_ref, sw_ref, xb_ref, p_ref, *, chunk):
    tm = x_ref.shape[1]
    n_slabs = p_ref.shape[0]
    halo = SUBLANES

    @pl.when(pl.program_id(1) == 0)
    def _():
        p_ref[:, 0:halo, :] = jnp.zeros((n_slabs, halo, LANES), f32)

    xb_ref[...] = x_ref[0].astype(bf16)

    def project(w_ref, c0):
        return jnp.dot(xb_ref[...], w_ref[:, c0:c0 + PROJ_COLS], preferred_element_type=f32)

    def qk_piece(c0):
        acc = project(wqk_ref, c0)
        for li in range(PROJ_COLS // LANES):
            sl = c0 // LANES + li
            lanes = slice(sl * LANES, (sl + 1) * LANES)
            p_ref[sl, halo:halo + tm, :] = acc[:, li * LANES:(li + 1) * LANES]
            taps = [cw_ref[j:j + 1, lanes] for j in range(ML_CONV)]
            for r0 in range(0, tm, PROJ_ROWS):
                y = cb_ref[:, lanes]
                for j in range(ML_CONV):
                    first = r0 + halo - (ML_CONV - 1) + j
                    y = y + p_ref[sl, first:first + PROJ_ROWS, :] * taps[j]
                hy = 0.5 * y
                qk_ref[0, r0:r0 + PROJ_ROWS, lanes] = (hy + hy * jnp.tanh(hy)).astype(bf16)
            p_ref[sl, 0:halo, :] = p_ref[sl, tm:tm + halo, :]

    def vo_piece(c0):
        acc = project(wvo_ref, c0)
        for li in range(PROJ_COLS // LANES):
            lanes = slice(c0 + li * LANES, c0 + (li + 1) * LANES)
            vot_ref[0, lanes, :] = acc[:, li * LANES:(li + 1) * LANES].T.astype(bf16)

    def swg_piece(c0):
        sw_w = sw_ref.shape[2]
        acc = project(wswg_ref, c0)
        if c0 + PROJ_COLS <= sw_w:
            sw_ref[0, :, c0:c0 + PROJ_COLS] = acc.astype(bf16)
        else:
            gates = acc[:, sw_w - c0:sw_w - c0 + LANES].T[0:SUBLANES, :]
            _gate_stats(gates + gb_ref[...], rows_ref, cols_ref, chunk)

    swg = [functools.partial(swg_piece, c0) for c0 in range(0, wswg_ref.shape[1], PROJ_COLS)]
    swg.pop()()
    others = [functools.partial(vo_piece, c0) for c0 in range(0, wvo_ref.shape[1], PROJ_COLS)] + swg
    n_qk = wqk_ref.shape[1] // PROJ_COLS
    per_qk = -(-len(others) // n_qk)
    for i in range(n_qk):
        qk_piece(i * PROJ_COLS)
        for piece in others[i * per_qk:(i + 1) * per_qk]:
            piece()


def _inproj(x, wqk, wvo, wswg, cw, cb, gbias):
    B, S, D = x.shape
    tm = min(TM_PROJ, S)
    chunk = min(ML_CHUNK, S)
    assert tm % chunk == 0
    grid = (B, S // tm)
    sw_w = SWA_WIDTH + 4 * SWA_HEAD_DIM
    tok = lambda w: pl.BlockSpec((1, tm, w), lambda b, s: (b, s, 0))
    return pl.pallas_call(
        functools.partial(_inproj_kernel, chunk=chunk),
        grid=grid,
        in_specs=[tok(D)] + [_const_spec(a) for a in (wqk, wvo, wswg, cw, cb, gbias)],
        out_specs=[tok(wqk.shape[1]),
                   pl.BlockSpec((1, wvo.shape[1], tm), lambda b, s: (b, 0, s)),
                   pl.BlockSpec((1, 3 * SUBLANES, tm), lambda b, s: (b, 0, s)),
                   tok(LANES),
                   tok(sw_w)],
        out_shape=[jax.ShapeDtypeStruct((B, S, wqk.shape[1]), bf16),
                   jax.ShapeDtypeStruct((B, wvo.shape[1], S), bf16),
                   jax.ShapeDtypeStruct((B, 3 * SUBLANES, S), f32),
                   jax.ShapeDtypeStruct((B, S, LANES), f32),
                   jax.ShapeDtypeStruct((B, S, sw_w), bf16)],
        scratch_shapes=[pltpu.VMEM((tm, D), bf16),
                        pltpu.VMEM((wqk.shape[1] // LANES, tm + SUBLANES, LANES), f32)],
        compiler_params=_params(2),
        name="inproj",
    )(x, *map(_arr, (wqk, wvo, wswg, cw, cb, gbias)))


C_ROW, CM_ROW, B_ROW = 0, SUBLANES, 2 * SUBLANES


def _scan_chunks(x, op, fill, chunk):
    pos = lax.broadcasted_iota(jnp.int32, x.shape, 1) % chunk
    sh = 1
    while sh < chunk:
        x = op(x, jnp.where(pos >= sh, pltpu.roll(x, sh, 1), fill))
        sh *= 2
    return x


def _gate_stats(g, rows_ref, cols_ref, chunk):
    ig = jnp.concatenate([g[0:4], g[0:4]], axis=0)
    fg = jnp.concatenate([g[4:8], g[4:8]], axis=0)
    lf = jnp.minimum(fg, 0.0) - jnp.log1p(jnp.exp(-jnp.abs(fg)))
    b = _scan_chunks(lf, jnp.add, 0.0, chunk)
    c = ig - b
    cm = _scan_chunks(c, jnp.maximum, -jnp.inf, chunk)
    rows_ref[0] = jnp.concatenate([c, cm, b], axis=0)
    stack = jnp.concatenate([c, jnp.zeros((LANES - SUBLANES, c.shape[1]), f32)], axis=0)
    cols_ref[0] = stack.T


def _mlstm_kernel(qk_ref, vot_ref, rows_ref, cols_ref, ngt_ref, out_ref, ct_ref, m_ref):
    L = ML_CHUNK if qk_ref.shape[1] % ML_CHUNK == 0 else qk_ref.shape[1]

    @pl.when(pl.program_id(1) == 0)
    def _():
        ct_ref[...] = jnp.zeros(ct_ref.shape, f32)
        m_ref[...] = jnp.zeros(m_ref.shape, f32)

    heads = range(ML_HEADS)
    m_all = m_ref[...]
    state = ([ct_ref[h] for h in heads], [m_all[h:h + 1, 0:1] for h in heads])
    for c in range(qk_ref.shape[1] // L):
        state = _mlstm_chunk(slice(c * L, (c + 1) * L), state, qk_ref, vot_ref, rows_ref, cols_ref,
                             ngt_ref, out_ref)
    for h in heads:
        ct_ref[h] = state[0][h]
    m_ref[0:ML_HEADS, :] = jnp.concatenate(
        [jnp.broadcast_to(m, (1, LANES)) for m in state[1]], axis=0)


def _mlstm_chunk(ts, state, qk_ref, vot_ref, rows_ref, cols_ref, ngt_ref, out_ref):
    ct, m_prev = state
    L = ts.stop - ts.start
    dh = ML_HEAD_DIM
    scale = dh ** -0.5
    reps = L // dh
    nt = (((1,), (1,)), ((), ()))
    half = L // 2
    src = lax.broadcasted_iota(jnp.int32, (L, half), 0)
    tgt = lax.broadcasted_iota(jnp.int32, (L, half), 1)
    causal = [(src <= tgt)[:half], src <= tgt + half]
    ones_t = jnp.ones((dh, L), bf16)
    heads = range(ML_HEADS)
    hs = lambda h: slice(h * dh, (h + 1) * dh)
    ks = lambda h: slice(ML_WIDTH + h * dh, ML_WIDTH + (h + 1) * dh)
    q = [qk_ref[0, ts, hs(h)] for h in heads]
    k = [qk_ref[0, ts, ks(h)] for h in heads]

    tcols = [slice(0, half), slice(half, L)]
    srows = [slice(0, half), slice(0, L)]
    s_t = [[lax.dot_general(k[h][srows[j]], q[h][tcols[j]], nt,
                            preferred_element_type=f32) for j in range(2)]
           for h in heads]
    inter = [lax.dot_general(ct[h].astype(bf16), q[h], nt, preferred_element_type=f32)
             for h in heads]

    c_row, cm_row, v_aug, intra = [], [], [], []
    for h in heads:
        c_row.append(rows_ref[0, C_ROW + h:C_ROW + h + 1, ts])
        cm_row.append(rows_ref[0, CM_ROW + h:CM_ROW + h + 1, ts])
        c_b = jnp.broadcast_to(cols_ref[0, ts, h:h + 1], (L, dh))
        c_wide = jnp.concatenate([c_b] * (half // dh), axis=1) + math.log(scale)
        v_aug.append(jnp.concatenate([vot_ref[0, hs(h), ts], ones_t], axis=0))
        parts = []
        for j in range(2):
            p = jnp.where(causal[j], jnp.exp(c_wide[srows[j]] - cm_row[h][:, tcols[j]]), 0.0)
            s_mat = (s_t[h][j] * p).astype(bf16)
            parts.append(jnp.dot(v_aug[h][:, srows[j]], s_mat, preferred_element_type=f32))
        intra.append(jnp.concatenate(parts, axis=1))

    big_m, m_last, vw = [], [], []
    for h in heads:
        big_m.append(jnp.maximum(m_prev[h], cm_row[h]))
        m_last.append(big_m[h][:, L - 1:L])
        wk = jnp.exp(c_row[h] - m_last[h]) * scale
        vw.append((v_aug[h].astype(f32) * wk).astype(bf16))
    upd = [jnp.dot(vw[h], k[h], preferred_element_type=f32) for h in heads]

    ct_new, m_new = [], []
    for h in heads:
        b_row = rows_ref[0, B_ROW + h:B_ROW + h + 1, ts]
        w_inter = jnp.exp(m_prev[h] - big_m[h])
        w_intra = jnp.exp(cm_row[h] - big_m[h])
        nd = w_inter * inter[h] + w_intra * intra[h]
        den = nd[dh:dh + 1, :]
        inv = 1.0 / jnp.maximum(jnp.abs(den), jnp.exp(-(b_row + big_m[h])))
        hh = nd[:dh, :] * inv
        mu = jnp.mean(hh, axis=0, keepdims=True)
        hc = hh - mu
        var = jnp.mean(hc * hc, axis=0, keepdims=True)
        ng = jnp.concatenate([ngt_ref[hs(h), :]] * reps, axis=1)
        hn = hc * lax.rsqrt(var + EPS) * ng
        gate = jax.nn.sigmoid(vot_ref[0, ks(h), ts].astype(f32))
        out_ref[0, ts, hs(h)] = (gate * hn).T.astype(bf16)
        ct_new.append(jnp.exp(m_prev[h] - m_last[h]) * ct[h] + upd[h])
        m_new.append(b_row[:, L - 1:L] + m_last[h])
    return ct_new, m_new


def _mlstm(qk, vot, rows, cols, norm_gt):
    B, S, _ = qk.shape
    chunk = min(ML_CHUNK, S)
    L = min(ML_STEP_CHUNKS * chunk, S)
    grid = (B, S // L)
    return pl.pallas_call(
        _mlstm_kernel,
        grid=grid,
        in_specs=[pl.BlockSpec((1, L, 2 * ML_WIDTH), lambda b, s: (b, s, 0)),
                  pl.BlockSpec((1, 2 * ML_WIDTH, L), lambda b, s: (b, 0, s)),
                  pl.BlockSpec((1, 3 * SUBLANES, L), lambda b, s: (b, 0, s)),
                  pl.BlockSpec((1, L, LANES), lambda b, s: (b, s, 0)),
                  _const_spec(norm_gt)],
        out_specs=pl.BlockSpec((1, L, ML_WIDTH), lambda b, s: (b, s, 0)),
        out_shape=jax.ShapeDtypeStruct((B, S, ML_WIDTH), bf16),
        scratch_shapes=[pltpu.VMEM((ML_HEADS, 2 * ML_HEAD_DIM, ML_HEAD_DIM), f32),
                        pltpu.VMEM((SUBLANES, LANES), f32)],
        compiler_params=_params(2),
        name="mlstm",
    )(qk, vot, rows, cols, _arr(norm_gt))


def _t5_bucket_table():
    assert WINDOW == BLOCK
    r = np.arange(BLOCK)[:, None]
    c = np.arange(BLOCK)[None, :]
    n = np.where(c > r, BLOCK + r - c, r - c)
    max_exact = REL_BUCKETS // 2
    nf = np.maximum(n, 1).astype(np.float32)
    large = max_exact + (np.log(nf / np.float32(max_exact)) / np.float32(math.log(REL_MAX_DIST / max_exact))
                         * np.float32(REL_BUCKETS - max_exact)).astype(np.int32)
    large = np.minimum(large, REL_BUCKETS - 1)
    return np.where(n < max_exact, n, large).astype(np.int32)


def _bias_kernel(bucket_ref, rb_ref, out_ref):
    bucket = bucket_ref[...]
    for h in range(SWA_HEADS):
        acc = jnp.zeros(bucket.shape, f32)
        for i in range(REL_BUCKETS):
            acc = jnp.where(bucket == i, rb_ref[i, h], acc)
        out_ref[h] = acc


def _swa_bias(rel_bias):
    bucket = jnp.asarray(_t5_bucket_table())
    return pl.pallas_call(
        _bias_kernel,
        in_specs=[pl.BlockSpec(memory_space=pltpu.VMEM), pl.BlockSpec(memory_space=pltpu.SMEM)],
        out_specs=pl.BlockSpec(memory_space=pltpu.VMEM),
        out_shape=jax.ShapeDtypeStruct((SWA_HEADS, BLOCK, BLOCK), f32),
        name="swa_bias",
    )(bucket, rel_bias)


def _swa_kernel(sw_ref, prev_ref, bias_ref, sink_ref, out_ref, *, layer):
    tq = sw_ref.shape[1]
    scale = SWA_HEAD_DIM ** -0.5
    pair = 2 * SWA_HEAD_DIM
    kc = slice(SWA_WIDTH, SWA_WIDTH + pair)
    vc = slice(SWA_WIDTH + pair, SWA_WIDTH + 2 * pair)
    first_tile = pl.program_id(1) == 0
    lo = lax.broadcasted_iota(jnp.int32, (BLOCK, pair), 1) < SWA_HEAD_DIM
    use_prev = (lax.broadcasted_iota(jnp.int32, (BLOCK, BLOCK), 1)
                > lax.broadcasted_iota(jnp.int32, (BLOCK, BLOCK), 0))
    zero_p = jnp.zeros((BLOCK, BLOCK), bf16)

    for j in range(tq // BLOCK):
        rows = slice(j * BLOCK, (j + 1) * BLOCK)
        prows = slice((j - 1) * BLOCK, j * BLOCK)
        if j == 0:
            k_prev, v_prev = prev_ref[0, :, 0:pair], prev_ref[0, :, pair:2 * pair]
        else:
            k_prev, v_prev = sw_ref[0, prows, kc], sw_ref[0, prows, vc]
        kband = jnp.concatenate([k_prev, sw_ref[0, rows, kc]], axis=0)
        vband = jnp.concatenate([v_prev, sw_ref[0, rows, vc]], axis=0)
        qs, heads = [], []
        for p in range(SWA_GROUP):
            qp = sw_ref[0, rows, p * pair:(p + 1) * pair] * scale
            zero = jnp.zeros_like(qp)
            qs += [jnp.where(lo, qp, zero), jnp.where(lo, zero, qp)]
            heads += [p, p + SWA_GROUP]
        logits = lax.dot_general(jnp.concatenate(qs, axis=0), kband, (((1,), (1,)), ((), ())),
                                 preferred_element_type=f32)
        ps, dens = [], []
        for n, head in enumerate(heads):
            lg2 = logits[n * BLOCK:(n + 1) * BLOCK]
            lg = jnp.where(use_prev, lg2[:, :BLOCK], lg2[:, BLOCK:]) + bias_ref[head]
            if j == 0:
                lg = jnp.where(jnp.logical_and(first_tile, use_prev), NEG_BIG, lg)
            sink = sink_ref[layer, head]
            mx = jnp.maximum(jnp.max(lg, axis=-1, keepdims=True), sink)
            pr = jnp.exp(lg - mx)
            dens.append(jnp.sum(pr, axis=-1, keepdims=True) + jnp.exp(sink - mx))
            pb = pr.astype(bf16)
            ps.append(jnp.concatenate([jnp.where(use_prev, pb, zero_p),
                                       jnp.where(use_prev, zero_p, pb)], axis=1))
        pv = jnp.dot(jnp.concatenate(ps, axis=0), vband, preferred_element_type=f32)
        o = [pv[n * BLOCK:(n + 1) * BLOCK] / dens[n] for n in range(len(heads))]
        for p in range(SWA_GROUP):
            out_ref[0, rows, p * pair:(p + 1) * pair] = jnp.where(lo, o[2 * p], o[2 * p + 1]).astype(bf16)


def _swa(sw, bias, sinks, layer):
    B, S, W = sw.shape
    tq = min(TQ_SWA, S)
    nblk = tq // BLOCK
    kvw = W - SWA_WIDTH
    assert kvw == 4 * SWA_HEAD_DIM and SWA_WIDTH % kvw == 0, "k/v group = [k0|k1|v0|v1]"
    return pl.pallas_call(
        functools.partial(_swa_kernel, layer=layer),
        grid=(B, S // tq),
        in_specs=[pl.BlockSpec((1, tq, W), lambda b, s: (b, s, 0)),
                  pl.BlockSpec((1, BLOCK, kvw),
                               lambda b, s: (b, jnp.maximum(s * nblk - 1, 0), SWA_WIDTH // kvw)),
                  _const_spec(bias),
                  pl.BlockSpec(memory_space=pltpu.SMEM)],
        out_specs=pl.BlockSpec((1, tq, SWA_WIDTH), lambda b, s: (b, s, 0)),
        out_shape=jax.ShapeDtypeStruct((B, S, SWA_WIDTH), bf16),
        compiler_params=_params(2),
        name="swa",
    )(sw, sw, bias, sinks)


def _memkv_kernel(mem_ref, wkv_ref, k_ref, v_ref):
    d = k_ref.shape[1]
    mb = mem_ref[...].astype(bf16)
    for c0 in range(0, 2 * d, PROJ_COLS):
        piece = jnp.dot(mb, wkv_ref[:, c0:c0 + PROJ_COLS], preferred_element_type=f32).astype(bf16)
        if c0 < d:
            k_ref[:, c0:c0 + PROJ_COLS] = piece
        else:
            v_ref[:, c0 - d:c0 - d + PROJ_COLS] = piece


def _memkv(mem, wkv):
    B, M, D = mem.shape
    rows = B * M
    tm = math.gcd(rows, TM_MEMKV)
    blk = pl.BlockSpec((tm, D), lambda i: (i, 0))
    k, v = pl.pallas_call(
        _memkv_kernel,
        grid=(rows // tm,),
        in_specs=[blk, _const_spec(wkv)],
        out_specs=[blk, blk],
        out_shape=[jax.ShapeDtypeStruct((rows, D), bf16)] * 2,
        compiler_params=_params(1),
        name="memkv",
    )(mem.reshape(rows, D), _arr(wkv))
    return k.reshape(B, M, D), v.reshape(B, M, D)


def _mix_kernel(hml_ref, hsw_ref, x_ref, woml_ref, wosw_ref, g1_ref, b1_ref,
                wq_ref, k_ref, v_ref, wo_ref, g2_ref, b2_ref, out_ref):
    tm, d = x_ref.shape[1], x_ref.shape[2]
    dh = d // XA_HEADS
    rows = [slice(i * tm // MIX_STREAMS, (i + 1) * tm // MIX_STREAMS) for i in range(MIX_STREAMS)]
    streams = range(MIX_STREAMS)
    cs = lambda a: slice(a * dh, (a + 1) * dh)

    h = [jnp.dot(hml_ref[0, r, :], woml_ref[...], preferred_element_type=f32)
         + jnp.dot(hsw_ref[0, r, :], wosw_ref[...], preferred_element_type=f32) for r in rows]
    x1, q = [], []
    for i in streams:
        x1.append(_layer_norm(ALPHA * x_ref[0, rows[i], :] + h[i], g1_ref[...], b1_ref[...]))
        q.append(jnp.dot(x1[i].astype(bf16), wq_ref[...], preferred_element_type=f32).astype(bf16))

    items = [(a, i) for a in range(XA_HEADS) for i in streams]
    lag = MIX_STREAMS
    prob, den, outs = {}, {}, {}

    def logits(a, i):
        lg = lax.dot_general(q[i][:, cs(a)], k_ref[0, :, cs(a)], (((1,), (1,)), ((), ())),
                             preferred_element_type=f32) * (dh ** -0.5)
        p = jnp.exp(lg - jnp.max(lg, axis=-1, keepdims=True))
        den[a, i] = jnp.sum(p, axis=-1, keepdims=True)
        prob[a, i] = p.astype(bf16)

    def values(a, i):
        o = jnp.dot(prob[a, i], v_ref[0, :, cs(a)], preferred_element_type=f32) / den[a, i]
        outs[a, i] = o.astype(bf16)

    for n, item in enumerate(items):
        logits(*item)
        if n >= lag:
            values(*items[n - lag])
    for item in items[-lag:]:
        values(*item)

    h2 = [jnp.dot(jnp.concatenate([outs[a, i] for a in range(XA_HEADS)], axis=1), wo_ref[...],
                  preferred_element_type=f32) for i in streams]
    for i in streams:
        out_ref[0, rows[i], :] = _layer_norm(ALPHA * x1[i] + h2[i], g2_ref[...], b2_ref[...])


def _mix(hml, hsw, x, woml, wosw, g1, b1, wq, k, v, wo, g2, b2):
    B, S, D = x.shape
    tm = min(TM_MIX, S)
    M = k.shape[1]
    tok = lambda w: pl.BlockSpec((1, tm, w), lambda b, s: (b, s, 0))
    memblk = pl.BlockSpec((1, M, D), lambda b, s: (b, 0, 0))
    consts = [woml, wosw, g1, b1, wq]
    return pl.pallas_call(
        _mix_kernel,
        grid=(B, S // tm),
        in_specs=[tok(hml.shape[2]), tok(hsw.shape[2]), tok(D)]
                 + [_const_spec(a) for a in consts]
                 + [memblk, memblk] + [_const_spec(a) for a in (wo, g2, b2)],
        out_specs=tok(D),
        out_shape=jax.ShapeDtypeStruct((B, S, D), f32),
        compiler_params=_params(2),
        name="mix",
    )(hml, hsw, x, *map(_arr, consts), k, v, *map(_arr, (wo, g2, b2)))


GELU_C0 = math.sqrt(2.0 / math.pi)
GELU_C1 = 0.044715 * GELU_C0


def _ffn_kernel(x_ref, wup_ref, cw_ref, cb_ref, wdn_ref, g_ref, b_ref, out_ref,
                carry_ref, xb_ref, u00_ref, u01_ref, u10_ref, u11_ref, h_ref):
    tm = x_ref.shape[1]
    d_ff = wdn_ref.shape[0]
    cwid = FF_CHUNK
    n_chunks = d_ff // cwid
    n_slabs = cwid // LANES
    halo = SUBLANES
    u_refs = ((u00_ref, u01_ref), (u10_ref, u11_ref))

    @pl.when(pl.program_id(1) == 0)
    def _():
        carry_ref[...] = jnp.zeros(carry_ref.shape, f32)

    xb_ref[...] = x_ref[0].astype(bf16)

    def up(ci):
        for u_ref, c0 in zip(u_refs[ci % 2], (ci * cwid, d_ff + ci * cwid)):
            u = jnp.dot(xb_ref[...], wup_ref[:, c0:c0 + cwid], preferred_element_type=f32)
            for sl in range(n_slabs):
                cols = slice(c0 + sl * LANES, c0 + (sl + 1) * LANES)
                u_ref[sl, 0:halo, :] = carry_ref[:, cols]
                u_ref[sl, halo:halo + tm, :] = u[:, sl * LANES:(sl + 1) * LANES]
                carry_ref[:, cols] = u_ref[sl, tm:tm + halo, :]

    def act(ci):
        gu_ref, vu_ref = u_refs[ci % 2]
        for sl in range(n_slabs):
            gc = slice(ci * cwid + sl * LANES, ci * cwid + (sl + 1) * LANES)
            vc = slice(d_ff + gc.start, d_ff + gc.stop)
            gb, gw = cb_ref[:, gc], [cw_ref[j:j + 1, gc] for j in range(FFN_CONV)]
            vb, vw = 0.5 * cb_ref[:, vc], [0.5 * cw_ref[j:j + 1, vc] for j in range(FFN_CONV)]
            for r0 in range(0, tm, FF_ROWS):
                g, v = gb, vb
                for j in range(FFN_CONV):
                    rows = slice(r0 + halo - (FFN_CONV - 1) + j, r0 + halo - (FFN_CONV - 1) + j + FF_ROWS)
                    g = g + gu_ref[sl, rows, :] * gw[j]
                    v = v + vu_ref[sl, rows, :] * vw[j]
                t = jnp.tanh(g * (GELU_C0 + GELU_C1 * (g * g)))
                h_ref[r0:r0 + FF_ROWS, gc] = ((g * v) * (1.0 + t)).astype(bf16)

    up(0)
    for ci in range(n_chunks - 1):
        up(ci + 1)
        act(ci)
    top, bot = slice(0, tm // 2), slice(tm // 2, tm)
    k1 = (n_chunks - 1) * cwid
    h_top = jnp.dot(h_ref[top, :k1], wdn_ref[:k1, :], preferred_element_type=f32)
    act(n_chunks - 1)
    h_top = h_top + jnp.dot(h_ref[top, k1:], wdn_ref[k1:, :], preferred_element_type=f32)
    h_bot = jnp.dot(h_ref[bot, :], wdn_ref[...], preferred_element_type=f32)
    out_ref[0, top, :] = _layer_norm(ALPHA * x_ref[0, top, :] + h_top, g_ref[...], b_ref[...])
    out_ref[0, bot, :] = _layer_norm(ALPHA * x_ref[0, bot, :] + h_bot, g_ref[...], b_ref[...])


def _ffn(x, wup, cw, cb, wdn, g, b):
    B, S, D = x.shape
    tm = min(TM_FFN, S)
    cwid = FF_CHUNK
    assert wdn.shape[0] % cwid == 0 and wup.shape[1] == 2 * wdn.shape[0]
    tok = pl.BlockSpec((1, tm, D), lambda b_, s: (b_, s, 0))
    return pl.pallas_call(
        _ffn_kernel,
        grid=(B, S // tm),
        in_specs=[tok] + [_const_spec(a) for a in (wup, cw, cb, wdn, g, b)],
        out_specs=tok,
        out_shape=jax.ShapeDtypeStruct((B, S, D), f32),
        scratch_shapes=[pltpu.VMEM((SUBLANES, wup.shape[1]), f32),
                        pltpu.VMEM((tm, D), bf16),
                        ]
                       + [pltpu.VMEM((cwid // LANES, tm + SUBLANES, LANES), f32)] * 4
                       + [pltpu.VMEM((tm, wdn.shape[0]), bf16)],
        compiler_params=_params(2),
        name="ffn",
    )(x, *map(_arr, (wup, cw, cb, wdn, g, b)))


PAIR_ORDER = [h for p in range(SWA_GROUP) for h in (p, p + SWA_GROUP)]


def _swa_gate_columns(w):
    depth, D = w.shape[:2]
    g0 = 4 * ML_WIDTH
    q0 = g0 + 2 * ML_HEADS
    kv0 = q0 + SWA_WIDTH
    heads = [w[..., q0 + h * SWA_HEAD_DIM:q0 + (h + 1) * SWA_HEAD_DIM] for h in PAIR_ORDER]
    used = SWA_WIDTH + (w.shape[-1] - kv0) + 2 * ML_HEADS
    pad = jnp.zeros((depth, D, -used % PROJ_COLS), bf16)
    return jnp.concatenate(heads + [w[..., kv0:], w[..., g0:q0], pad], axis=-1)


def _pair_rows(w_out_b):
    r0 = ML_WIDTH
    return jnp.concatenate(
        [w_out_b[:, r0 + h * SWA_HEAD_DIM:r0 + (h + 1) * SWA_HEAD_DIM] for h in PAIR_ORDER], axis=1)


def kernel(x, mem, rel_bias, w_in, ml_conv_w, ml_conv_b, ml_i_bias, ml_f_bias, ml_norm_g, swa_sinks, w_out, ln1_g, ln1_b, xa_wq, xa_wkv, xa_wo, ln2_g, ln2_b, ffn_w_up, ffn_conv_w, ffn_conv_b, ffn_w_down, ln3_g, ln3_b):
    depth = w_in.shape[0]
    assert depth == DEPTH, "ALPHA is the DeepNorm constant of a DEPTH-layer trunk"
    rows = lambda a: a.reshape(depth, 1, -1)
    w_in_b, w_out_b = w_in.astype(bf16), w_out.astype(bf16)
    wswg, wo_sw = _swa_gate_columns(w_in_b), _pair_rows(w_out_b)
    wq, wkv, wo = xa_wq.astype(bf16), xa_wkv.astype(bf16), xa_wo.astype(bf16)
    wup, wdn = ffn_w_up.astype(bf16), ffn_w_down.astype(bf16)
    gbias = jnp.concatenate([ml_i_bias, ml_f_bias], axis=1).reshape(depth, 2 * ML_HEADS, 1)
    small = [rows(a) for a in (ml_conv_b, ln1_g, ln1_b, ln2_g, ln2_b, ffn_conv_b, ln3_g, ln3_b)]
    norm_gt = jnp.broadcast_to(ml_norm_g[:, :, None], ml_norm_g.shape + (LANES,))
    bias = _swa_bias(rel_bias)
    for l in range(depth):
        at = lambda a: _Layer(a, l)
        conv_b, g1, b1, g2, b2, ffn_cb, g3, b3 = map(at, small)
        wqk, wvo = (_Layer(w_in_b, l, axis=1, size=2 * ML_WIDTH, index=i) for i in (0, 1))
        wo_ml = _Layer(w_out_b, l, axis=0, size=ML_WIDTH, index=0)
        qk, vot, rows, cols, sw = _inproj(x, wqk, wvo, at(wswg), at(ml_conv_w), conv_b, at(gbias))
        hml = _mlstm(qk, vot, rows, cols, at(norm_gt))
        hsw = _swa(sw, bias, swa_sinks, l)
        k, v = _memkv(mem, at(wkv))
        x = _mix(hml, hsw, x, wo_ml, at(wo_sw), g1, b1, at(wq), k, v, at(wo), g2, b2)
        x = _ffn(x, at(wup), at(ffn_conv_w), ffn_cb, at(wdn), g3, b3)
    return x
```

```python
import functools
import math

import numpy as np
import jax
import jax.numpy as jnp
from jax import lax
from jax.experimental import pallas as pl
from jax.experimental.pallas import tpu as pltpu

f32 = jnp.float32
bf16 = jnp.bfloat16

ML_HEADS = 4
ML_HEAD_DIM = 128
ML_WIDTH = ML_HEADS * ML_HEAD_DIM
ML_CONV = 4
SWA_HEADS = 8
SWA_KV_HEADS = 2
SWA_GROUP = SWA_HEADS // SWA_KV_HEADS
SWA_HEAD_DIM = 64
SWA_WIDTH = SWA_HEADS * SWA_HEAD_DIM
WINDOW = 128
BLOCK = 128
REL_BUCKETS = 32
REL_MAX_DIST = 128
XA_HEADS = 4
FFN_CONV = 3
DEPTH = 2
ALPHA = (2.0 * DEPTH) ** 0.25
EPS = 1e-5

LANES = 128
SUBLANES = 8
TM_PROJ = 1024
PROJ_ROWS = 128
PROJ_COLS = 256
ML_CHUNK = 256
ML_STEP_CHUNKS = 16
TQ_SWA = 4096
TM_MEMKV = 1024
TM_MIX = 1024
MIX_STREAMS = 4
TM_FFN = 512
FF_CHUNK = 256
FF_ROWS = 64
VMEM_LIMIT = 56 * 1024 * 1024
NEG_BIG = -1e30


def _params(n_axes, flags=None):
    return pltpu.CompilerParams(
        dimension_semantics=("arbitrary",) * n_axes, vmem_limit_bytes=VMEM_LIMIT, flags=flags)


class _Layer:
    def __init__(self, stacked, l, axis=None, size=None, index=0):
        self.stacked, self.l, self.axis, self.size, self.index = stacked, l, axis, size, index

    @property
    def shape(self):
        shape = list(self.stacked.shape[1:])
        if self.axis is not None:
            shape[self.axis] = self.size
        return tuple(shape)

    @property
    def block_index(self):
        idx = [0] * (self.stacked.ndim - 1)
        if self.axis is not None:
            idx[self.axis] = self.index
        return (self.l,) + tuple(idx)


def _arr(op):
    return op.stacked if isinstance(op, _Layer) else op


def _const_spec(op):
    if isinstance(op, _Layer):
        index = op.block_index
        return pl.BlockSpec((None,) + op.shape, lambda *_: index, pipeline_mode=pl.Buffered(1))
    nd = len(op.shape)
    return pl.BlockSpec(op.shape, lambda *_: (0,) * nd, pipeline_mode=pl.Buffered(1))


def _layer_norm(z, g, b):
    mu = jnp.mean(z, axis=-1, keepdims=True)
    zc = z - mu
    var = jnp.mean(zc * zc, axis=-1, keepdims=True)
    return zc * lax.rsqrt(var + EPS) * g + b


def _inproj_kernel(x_ref, wqk_ref, wvo_ref, wswg_ref, cw_ref, cb_ref, gb_ref,
                   qk_ref, vot_ref, rows_ref, cols_ref, sw_ref, xb_ref, p_ref, *, chunk):
    tm = x_ref.shape[1]
    n_slabs = p_ref.shape[0]
    halo = SUBLANES

    @pl.when(pl.program_id(1) == 0)
    def _():
        p_ref[:, 0:halo, :] = jnp.zeros((n_slabs, halo, LANES), f32)

    xb_ref[...] = x_ref[0].astype(bf16)

    def project(w_ref, c0):
        return jnp.dot(xb_ref[...], w_ref[:, c0:c0 + PROJ_COLS], preferred_element_type=f32)

    def qk_piece(c0):
        acc = project(wqk_ref, c0)
        for li in range(PROJ_COLS // LANES):
            sl = c0 // LANES + li
            lanes = slice(sl * LANES, (sl + 1) * LANES)
            p_ref[sl, halo:halo + tm, :] = acc[:, li * LANES:(li + 1) * LANES]
            taps = [cw_ref[j:j + 1, lanes] for j in range(ML_CONV)]
            for r0 in range(0, tm, PROJ_ROWS):
                y = cb_ref[:, lanes]
                for j in range(ML_CONV):
                    first = r0 + halo - (ML_CONV - 1) + j
                    y = y + p_ref[sl, first:first + PROJ_ROWS, :] * taps[j]
                hy = 0.5 * y
                qk_ref[0, r0:r0 + PROJ_ROWS, lanes] = (hy + hy * jnp.tanh(hy)).astype(bf16)
            p_ref[sl, 0:halo, :] = p_ref[sl, tm:tm + halo, :]

    def vo_piece(c0):
        acc = project(wvo_ref, c0)
        for li in range(PROJ_COLS // LANES):
            lanes = slice(c0 + li * LANES, c0 + (li + 1) * LANES)
            vot_ref[0, lanes, :] = acc[:, li * LANES:(li + 1) * LANES].astype(bf16).T

    def swg_piece(c0):
        sw_w = sw_ref.shape[2]
        acc = project(wswg_ref, c0)
        if c0 + PROJ_COLS <= sw_w:
            sw_ref[0, :, c0:c0 + PROJ_COLS] = acc.astype(bf16)
        else:
            gates = acc[:, sw_w - c0:sw_w - c0 + LANES].T[0:SUBLANES, :]
            _gate_stats(gates + gb_ref[...], rows_ref, cols_ref, chunk)

    swg = [functools.partial(swg_piece, c0) for c0 in range(0, wswg_ref.shape[1], PROJ_COLS)]
    swg.pop()()
    others = [functools.partial(vo_piece, c0) for c0 in range(0, wvo_ref.shape[1], PROJ_COLS)] + swg
    n_qk = wqk_ref.shape[1] // PROJ_COLS
    per_qk = -(-len(others) // n_qk)
    for i in range(n_qk):
        qk_piece(i * PROJ_COLS)
        for piece in others[i * per_qk:(i + 1) * per_qk]:
            piece()


def _inproj(x, wqk, wvo, wswg, cw, cb, gbias):
    B, S, D = x.shape
    tm = min(TM_PROJ, S)
    chunk = min(ML_CHUNK, S)
    assert tm % chunk == 0
    grid = (B, S // tm)
    sw_w = SWA_WIDTH + 4 * SWA_HEAD_DIM
    tok = lambda w: pl.BlockSpec((1, tm, w), lambda b, s: (b, s, 0))
    return pl.pallas_call(
        functools.partial(_inproj_kernel, chunk=chunk),
        grid=grid,
        in_specs=[tok(D)] + [_const_spec(a) for a in (wqk, wvo, wswg, cw, cb, gbias)],
        out_specs=[tok(wqk.shape[1]),
                   pl.BlockSpec((1, wvo.shape[1], tm), lambda b, s: (b, 0, s)),
                   pl.BlockSpec((1, 3 * SUBLANES, tm), lambda b, s: (b, 0, s)),
                   tok(LANES),
                   tok(sw_w)],
        out_shape=[jax.ShapeDtypeStruct((B, S, wqk.shape[1]), bf16),
                   jax.ShapeDtypeStruct((B, wvo.shape[1], S), bf16),
                   jax.ShapeDtypeStruct((B, 3 * SUBLANES, S), f32),
                   jax.ShapeDtypeStruct((B, S, LANES), f32),
                   jax.ShapeDtypeStruct((B, S, sw_w), bf16)],
        scratch_shapes=[pltpu.VMEM((tm, D), bf16),
                        pltpu.VMEM((wqk.shape[1] // LANES, tm + SUBLANES, LANES), f32)],
        compiler_params=_params(2),
        name="inproj",
    )(x, *map(_arr, (wqk, wvo, wswg, cw, cb, gbias)))


C_ROW, CM_ROW, B_ROW = 0, SUBLANES, 2 * SUBLANES


def _scan_chunks(x, op, fill, chunk):
    pos = lax.broadcasted_iota(jnp.int32, x.shape, 1) % chunk
    sh = 1
    while sh < chunk:
        x = op(x, jnp.where(pos >= sh, pltpu.roll(x, sh, 1), fill))
        sh *= 2
    return x


def _gate_stats(g, rows_ref, cols_ref, chunk):
    ig = jnp.concatenate([g[0:4], g[0:4]], axis=0)
    fg = jnp.concatenate([g[4:8], g[4:8]], axis=0)
    lf = jnp.minimum(fg, 0.0) - jnp.log1p(jnp.exp(-jnp.abs(fg)))
    b = _scan_chunks(lf, jnp.add, 0.0, chunk)
    c = ig - b
    cm = _scan_chunks(c, jnp.maximum, -jnp.inf, chunk)
    rows_ref[0] = jnp.concatenate([c, cm, b], axis=0)
    stack = jnp.concatenate([c, jnp.zeros((LANES - SUBLANES, c.shape[1]), f32)], axis=0)
    cols_ref[0] = stack.T


def _mlstm_kernel(qk_ref, vot_ref, rows_ref, cols_ref, ngt_ref, out_ref, ct_ref, m_ref):
    L = ML_CHUNK if qk_ref.shape[1] % ML_CHUNK == 0 else qk_ref.shape[1]

    @pl.when(pl.program_id(1) == 0)
    def _():
        ct_ref[...] = jnp.zeros(ct_ref.shape, f32)
        m_ref[...] = jnp.zeros(m_ref.shape, f32)

    heads = range(ML_HEADS)
    m_all = m_ref[...]
    state = ([ct_ref[h] for h in heads], [m_all[h:h + 1, 0:1] for h in heads])
    for c in range(qk_ref.shape[1] // L):
        state = _mlstm_chunk(slice(c * L, (c + 1) * L), state, qk_ref, vot_ref, rows_ref, cols_ref,
                             ngt_ref, out_ref)
    for h in heads:
        ct_ref[h] = state[0][h]
    m_ref[0:ML_HEADS, :] = jnp.concatenate(
        [jnp.broadcast_to(m, (1, LANES)) for m in state[1]], axis=0)


def _mlstm_chunk(ts, state, qk_ref, vot_ref, rows_ref, cols_ref, ngt_ref, out_ref):
    ct, m_prev = state
    L = ts.stop - ts.start
    dh = ML_HEAD_DIM
    scale = dh ** -0.5
    reps = L // dh
    nt = (((1,), (1,)), ((), ()))
    half = L // 2
    src = lax.broadcasted_iota(jnp.int32, (L, half), 0)
    tgt = lax.broadcasted_iota(jnp.int32, (L, half), 1)
    causal = [(src <= tgt)[:half], src <= tgt + half]
    ones_t = jnp.ones((dh, L), bf16)
    heads = range(ML_HEADS)
    hs = lambda h: slice(h * dh, (h + 1) * dh)
    ks = lambda h: slice(ML_WIDTH + h * dh, ML_WIDTH + (h + 1) * dh)
    q = [qk_ref[0, ts, hs(h)] for h in heads]
    k = [qk_ref[0, ts, ks(h)] for h in heads]

    tcols = [slice(0, half), slice(half, L)]
    srows = [slice(0, half), slice(0, L)]
    s_t = [[lax.dot_general(k[h][srows[j]], q[h][tcols[j]], nt,
                            preferred_element_type=f32) for j in range(2)]
           for h in heads]
    inter = [lax.dot_general(ct[h].astype(bf16), q[h], nt, preferred_element_type=f32)
             for h in heads]

    c_row, cm_row, v_aug, intra = [], [], [], []
    for h in heads:
        c_row.append(rows_ref[0, C_ROW + h:C_ROW + h + 1, ts])
        cm_row.append(rows_ref[0, CM_ROW + h:CM_ROW + h + 1, ts])
        c_b = jnp.broadcast_to(cols_ref[0, ts, h:h + 1], (L, dh))
        c_wide = jnp.concatenate([c_b] * (half // dh), axis=1) + math.log(scale)
        v_aug.append(jnp.concatenate([vot_ref[0, hs(h), ts], ones_t], axis=0))
        parts = []
        for j in range(2):
            p = jnp.where(causal[j], jnp.exp(c_wide[srows[j]] - cm_row[h][:, tcols[j]]), 0.0)
            s_mat = (s_t[h][j] * p).astype(bf16)
            parts.append(jnp.dot(v_aug[h][:, srows[j]], s_mat, preferred_element_type=f32))
        intra.append(jnp.concatenate(parts, axis=1))

    big_m, m_last, vw = [], [], []
    for h in heads:
        big_m.append(jnp.maximum(m_prev[h], cm_row[h]))
        m_last.append(big_m[h][:, L - 1:L])
        wk = jnp.exp(c_row[h] - m_last[h]) * scale
        vw.append((v_aug[h].astype(f32) * wk).astype(bf16))
    upd = [jnp.dot(vw[h], k[h], preferred_element_type=f32) for h in heads]

    ct_new, m_new = [], []
    for h in heads:
        b_row = rows_ref[0, B_ROW + h:B_ROW + h + 1, ts]
        w_inter = jnp.exp(m_prev[h] - big_m[h])
        w_intra = jnp.exp(cm_row[h] - big_m[h])
        nd = w_inter * inter[h] + w_intra * intra[h]
        den = nd[dh:dh + 1, :]
        inv = 1.0 / jnp.maximum(jnp.abs(den), jnp.exp(-(b_row + big_m[h])))
        hh = nd[:dh, :] * inv
        mu = jnp.mean(hh, axis=0, keepdims=True)
        hc = hh - mu
        var = jnp.mean(hc * hc, axis=0, keepdims=True)
        ng = jnp.concatenate([ngt_ref[hs(h), :]] * reps, axis=1)
        hn = hc * lax.rsqrt(var + EPS) * ng
        gate = jax.nn.sigmoid(vot_ref[0, ks(h), ts].astype(f32))
        out_ref[0, ts, hs(h)] = (gate * hn).T.astype(bf16)
        ct_new.append(jnp.exp(m_prev[h] - m_last[h]) * ct[h] + upd[h])
        m_new.append(b_row[:, L - 1:L] + m_last[h])
    return ct_new, m_new


def _mlstm(qk, vot, rows, cols, norm_gt):
    B, S, _ = qk.shape
    chunk = min(ML_CHUNK, S)
    L = min(ML_STEP_CHUNKS * chunk, S)
    grid = (B, S // L)
    return pl.pallas_call(
        _mlstm_kernel,
        grid=grid,
        in_specs=[pl.BlockSpec((1, L, 2 * ML_WIDTH), lambda b, s: (b, s, 0)),
                  pl.BlockSpec((1, 2 * ML_WIDTH, L), lambda b, s: (b, 0, s)),
                  pl.BlockSpec((1, 3 * SUBLANES, L), lambda b, s: (b, 0, s)),
                  pl.BlockSpec((1, L, LANES), lambda b, s: (b, s, 0)),
                  _const_spec(norm_gt)],
        out_specs=pl.BlockSpec((1, L, ML_WIDTH), lambda b, s: (b, s, 0)),
        out_shape=jax.ShapeDtypeStruct((B, S, ML_WIDTH), bf16),
        scratch_shapes=[pltpu.VMEM((ML_HEADS, 2 * ML_HEAD_DIM, ML_HEAD_DIM), f32),
                        pltpu.VMEM((SUBLANES, LANES), f32)],
        compiler_params=_params(2),
        name="mlstm",
    )(qk, vot, rows, cols, _arr(norm_gt))


def _t5_bucket_table():
    assert WINDOW == BLOCK
    r = np.arange(BLOCK)[:, None]
    c = np.arange(BLOCK)[None, :]
    n = np.where(c > r, BLOCK + r - c, r - c)
    max_exact = REL_BUCKETS // 2
    nf = np.maximum(n, 1).astype(np.float32)
    large = max_exact + (np.log(nf / np.float32(max_exact)) / np.float32(math.log(REL_MAX_DIST / max_exact))
                         * np.float32(REL_BUCKETS - max_exact)).astype(np.int32)
    large = np.minimum(large, REL_BUCKETS - 1)
    return np.where(n < max_exact, n, large).astype(np.int32)


def _bias_kernel(bucket_ref, rb_ref, out_ref):
    bucket = bucket_ref[...]
    for h in range(SWA_HEADS):
        acc = jnp.zeros(bucket.shape, f32)
        for i in range(REL_BUCKETS):
            acc = jnp.where(bucket == i, rb_ref[i, h], acc)
        out_ref[h] = acc


def _swa_bias(rel_bias):
    bucket = jnp.asarray(_t5_bucket_table())
    return pl.pallas_call(
        _bias_kernel,
        in_specs=[pl.BlockSpec(memory_space=pltpu.VMEM), pl.BlockSpec(memory_space=pltpu.SMEM)],
        out_specs=pl.BlockSpec(memory_space=pltpu.VMEM),
        out_shape=jax.ShapeDtypeStruct((SWA_HEADS, BLOCK, BLOCK), f32),
        name="swa_bias",
    )(bucket, rel_bias)


def _swa_kernel(sw_ref, prev_ref, bias_ref, sink_ref, out_ref, *, layer):
    tq = sw_ref.shape[1]
    scale = SWA_HEAD_DIM ** -0.5
    pair = 2 * SWA_HEAD_DIM
    kc = slice(SWA_WIDTH, SWA_WIDTH + pair)
    vc = slice(SWA_WIDTH + pair, SWA_WIDTH + 2 * pair)
    first_tile = pl.program_id(1) == 0
    lo = lax.broadcasted_iota(jnp.int32, (BLOCK, pair), 1) < SWA_HEAD_DIM
    use_prev = (lax.broadcasted_iota(jnp.int32, (BLOCK, BLOCK), 1)
                > lax.broadcasted_iota(jnp.int32, (BLOCK, BLOCK), 0))
    zero_p = jnp.zeros((BLOCK, BLOCK), bf16)

    for j in range(tq // BLOCK):
        rows = slice(j * BLOCK, (j + 1) * BLOCK)
        prows = slice((j - 1) * BLOCK, j * BLOCK)
        if j == 0:
            k_prev, v_prev = prev_ref[0, :, 0:pair], prev_ref[0, :, pair:2 * pair]
        else:
            k_prev, v_prev = sw_ref[0, prows, kc], sw_ref[0, prows, vc]
        kband = jnp.concatenate([k_prev, sw_ref[0, rows, kc]], axis=0)
        vband = jnp.concatenate([v_prev, sw_ref[0, rows, vc]], axis=0)
        qs, heads = [], []
        for p in range(SWA_GROUP):
            qp = sw_ref[0, rows, p * pair:(p + 1) * pair] * scale
            zero = jnp.zeros_like(qp)
            qs += [jnp.where(lo, qp, zero), jnp.where(lo, zero, qp)]
            heads += [p, p + SWA_GROUP]
        logits = lax.dot_general(jnp.concatenate(qs, axis=0), kband, (((1,), (1,)), ((), ())),
                                 preferred_element_type=f32)
        ps, dens = [], []
        for n, head in enumerate(heads):
            lg2 = logits[n * BLOCK:(n + 1) * BLOCK]
            lg = jnp.where(use_prev, lg2[:, :BLOCK], lg2[:, BLOCK:]) + bias_ref[head]
            if j == 0:
                lg = jnp.where(jnp.logical_and(first_tile, use_prev), NEG_BIG, lg)
            sink = sink_ref[layer, head]
            mx = jnp.maximum(jnp.max(lg, axis=-1, keepdims=True), sink)
            pr = jnp.exp(lg - mx)
            dens.append(jnp.sum(pr, axis=-1, keepdims=True) + jnp.exp(sink - mx))
            pb = pr.astype(bf16)
            ps.append(jnp.concatenate([jnp.where(use_prev, pb, zero_p),
                                       jnp.where(use_prev, zero_p, pb)], axis=1))
        pv = jnp.dot(jnp.concatenate(ps, axis=0), vband, preferred_element_type=f32)
        o = [pv[n * BLOCK:(n + 1) * BLOCK] / dens[n] for n in range(len(heads))]
        for p in range(SWA_GROUP):
            out_ref[0, rows, p * pair:(p + 1) * pair] = jnp.where(lo, o[2 * p], o[2 * p + 1]).astype(bf16)


def _swa(sw, bias, sinks, layer):
    B, S, W = sw.shape
    tq = min(TQ_SWA, S)
    nblk = tq // BLOCK
    kvw = W - SWA_WIDTH
    assert kvw == 4 * SWA_HEAD_DIM and SWA_WIDTH % kvw == 0, "k/v group = [k0|k1|v0|v1]"
    return pl.pallas_call(
        functools.partial(_swa_kernel, layer=layer),
        grid=(B, S // tq),
        in_specs=[pl.BlockSpec((1, tq, W), lambda b, s: (b, s, 0)),
                  pl.BlockSpec((1, BLOCK, kvw),
                               lambda b, s: (b, jnp.maximum(s * nblk - 1, 0), SWA_WIDTH // kvw)),
                  _const_spec(bias),
                  pl.BlockSpec(memory_space=pltpu.SMEM)],
        out_specs=pl.BlockSpec((1, tq, SWA_WIDTH), lambda b, s: (b, s, 0)),
        out_shape=jax.ShapeDtypeStruct((B, S, SWA_WIDTH), bf16),
        compiler_params=_params(2),
        name="swa",
    )(sw, sw, bias, sinks)


def _memkv_kernel(mem_ref, wkv_ref, k_ref, v_ref):
    d = k_ref.shape[1]
    mb = mem_ref[...].astype(bf16)
    for c0 in range(0, 2 * d, PROJ_COLS):
        piece = jnp.dot(mb, wkv_ref[:, c0:c0 + PROJ_COLS], preferred_element_type=f32).astype(bf16)
        if c0 < d:
            k_ref[:, c0:c0 + PROJ_COLS] = piece
        else:
            v_ref[:, c0 - d:c0 - d + PROJ_COLS] = piece


def _memkv(mem, wkv):
    B, M, D = mem.shape
    rows = B * M
    tm = math.gcd(rows, TM_MEMKV)
    blk = pl.BlockSpec((tm, D), lambda i: (i, 0))
    k, v = pl.pallas_call(
        _memkv_kernel,
        grid=(rows // tm,),
        in_specs=[blk, _const_spec(wkv)],
        out_specs=[blk, blk],
        out_shape=[jax.ShapeDtypeStruct((rows, D), bf16)] * 2,
        compiler_params=_params(1),
        name="memkv",
    )(mem.reshape(rows, D), _arr(wkv))
    return k.reshape(B, M, D), v.reshape(B, M, D)


def _mix_kernel(hml_ref, hsw_ref, x_ref, woml_ref, wosw_ref, g1_ref, b1_ref,
                wq_ref, k_ref, v_ref, wo_ref, g2_ref, b2_ref, out_ref):
    tm, d = x_ref.shape[1], x_ref.shape[2]
    dh = d // XA_HEADS
    rows = [slice(i * tm // MIX_STREAMS, (i + 1) * tm // MIX_STREAMS) for i in range(MIX_STREAMS)]
    streams = range(MIX_STREAMS)
    cs = lambda a: slice(a * dh, (a + 1) * dh)

    h = [jnp.dot(hml_ref[0, r, :], woml_ref[...], preferred_element_type=f32)
         + jnp.dot(hsw_ref[0, r, :], wosw_ref[...], preferred_element_type=f32) for r in rows]
    x1, q = [], []
    for i in streams:
        x1.append(_layer_norm(ALPHA * x_ref[0, rows[i], :] + h[i], g1_ref[...], b1_ref[...]))
        q.append(jnp.dot(x1[i].astype(bf16), wq_ref[...], preferred_element_type=f32).astype(bf16))

    items = [(a, i) for a in range(XA_HEADS) for i in streams]
    lag = MIX_STREAMS
    prob, den, outs = {}, {}, {}

    def logits(a, i):
        lg = lax.dot_general(q[i][:, cs(a)], k_ref[0, :, cs(a)], (((1,), (1,)), ((), ())),
                             preferred_element_type=f32) * (dh ** -0.5)
        p = jnp.exp(lg - jnp.max(lg, axis=-1, keepdims=True))
        den[a, i] = jnp.sum(p, axis=-1, keepdims=True)
        prob[a, i] = p.astype(bf16)

    def values(a, i):
        o = jnp.dot(prob[a, i], v_ref[0, :, cs(a)], preferred_element_type=f32) / den[a, i]
        outs[a, i] = o.astype(bf16)

    for n, item in enumerate(items):
        logits(*item)
        if n >= lag:
            values(*items[n - lag])
    for item in items[-lag:]:
        values(*item)

    h2 = [jnp.dot(jnp.concatenate([outs[a, i] for a in range(XA_HEADS)], axis=1), wo_ref[...],
                  preferred_element_type=f32) for i in streams]
    for i in streams:
        out_ref[0, rows[i], :] = _layer_norm(ALPHA * x1[i] + h2[i], g2_ref[...], b2_ref[...])


def _mix(hml, hsw, x, woml, wosw, g1, b1, wq, k, v, wo, g2, b2):
    B, S, D = x.shape
    tm = min(TM_MIX, S)
    M = k.shape[1]
    tok = lambda w: pl.BlockSpec((1, tm, w), lambda b, s: (b, s, 0))
    memblk = pl.BlockSpec((1, M, D), lambda b, s: (b, 0, 0))
    consts = [woml, wosw, g1, b1, wq]
    return pl.pallas_call(
        _mix_kernel,
        grid=(B, S // tm),
        in_specs=[tok(hml.shape[2]), tok(hsw.shape[2]), tok(D)]
                 + [_const_spec(a) for a in consts]
                 + [memblk, memblk] + [_const_spec(a) for a in (wo, g2, b2)],
        out_specs=tok(D),
        out_shape=jax.ShapeDtypeStruct((B, S, D), f32),
        compiler_params=_params(2),
        name="mix",
    )(hml, hsw, x, *map(_arr, consts), k, v, *map(_arr, (wo, g2, b2)))


GELU_C0 = math.sqrt(2.0 / math.pi)
GELU_C1 = 0.044715 * GELU_C0


def _ffn_kernel(x_ref, wup_ref, cw_ref, cb_ref, wdn_ref, g_ref, b_ref, out_ref,
                carry_ref, xb_ref, u00_ref, u01_ref, u10_ref, u11_ref, h_ref):
    tm = x_ref.shape[1]
    d_ff = wdn_ref.shape[0]
    cwid = FF_CHUNK
    n_chunks = d_ff // cwid
    n_slabs = cwid // LANES
    halo = SUBLANES
    u_refs = ((u00_ref, u01_ref), (u10_ref, u11_ref))

    @pl.when(pl.program_id(1) == 0)
    def _():
        carry_ref[...] = jnp.zeros(carry_ref.shape, f32)

    xb_ref[...] = x_ref[0].astype(bf16)

    def up(ci):
        for u_ref, c0 in zip(u_refs[ci % 2], (ci * cwid, d_ff + ci * cwid)):
            u = jnp.dot(xb_ref[...], wup_ref[:, c0:c0 + cwid], preferred_element_type=f32)
            for sl in range(n_slabs):
                cols = slice(c0 + sl * LANES, c0 + (sl + 1) * LANES)
                u_ref[sl, 0:halo, :] = carry_ref[:, cols]
                u_ref[sl, halo:halo + tm, :] = u[:, sl * LANES:(sl + 1) * LANES]
                carry_ref[:, cols] = u_ref[sl, tm:tm + halo, :]

    def act(ci):
        gu_ref, vu_ref = u_refs[ci % 2]
        for sl in range(n_slabs):
            gc = slice(ci * cwid + sl * LANES, ci * cwid + (sl + 1) * LANES)
            vc = slice(d_ff + gc.start, d_ff + gc.stop)
            gb, gw = cb_ref[:, gc], [cw_ref[j:j + 1, gc] for j in range(FFN_CONV)]
            vb, vw = 0.5 * cb_ref[:, vc], [0.5 * cw_ref[j:j + 1, vc] for j in range(FFN_CONV)]
            for r0 in range(0, tm, FF_ROWS):
                g, v = gb, vb
                for j in range(FFN_CONV):
                    rows = slice(r0 + halo - (FFN_CONV - 1) + j, r0 + halo - (FFN_CONV - 1) + j + FF_ROWS)
                    g = g + gu_ref[sl, rows, :] * gw[j]
                    v = v + vu_ref[sl, rows, :] * vw[j]
                t = jnp.tanh(g * (GELU_C0 + GELU_C1 * (g * g)))
                h_ref[r0:r0 + FF_ROWS, gc] = ((g * v) * (1.0 + t)).astype(bf16)

    up(0)
    for ci in range(n_chunks - 1):
        up(ci + 1)
        act(ci)
    top, bot = slice(0, tm // 2), slice(tm // 2, tm)
    k1 = (n_chunks - 1) * cwid
    h_top = jnp.dot(h_ref[top, :k1], wdn_ref[:k1, :], preferred_element_type=f32)
    act(n_chunks - 1)
    h_top = h_top + jnp.dot(h_ref[top, k1:], wdn_ref[k1:, :], preferred_element_type=f32)
    h_bot = jnp.dot(h_ref[bot, :], wdn_ref[...], preferred_element_type=f32)
    out_ref[0, top, :] = _layer_norm(ALPHA * x_ref[0, top, :] + h_top, g_ref[...], b_ref[...])
    out_ref[0, bot, :] = _layer_norm(ALPHA * x_ref[0, bot, :] + h_bot, g_ref[...], b_ref[...])


def _ffn(x, wup, cw, cb, wdn, g, b):
    B, S, D = x.shape
    tm = min(TM_FFN, S)
    cwid = FF_CHUNK
    assert wdn.shape[0] % cwid == 0 and wup.shape[1] == 2 * wdn.shape[0]
    tok = pl.BlockSpec((1, tm, D), lambda b_, s: (b_, s, 0))
    return pl.pallas_call(
        _ffn_kernel,
        grid=(B, S // tm),
        in_specs=[tok] + [_const_spec(a) for a in (wup, cw, cb, wdn, g, b)],
        out_specs=tok,
        out_shape=jax.ShapeDtypeStruct((B, S, D), f32),
        scratch_shapes=[pltpu.VMEM((SUBLANES, wup.shape[1]), f32),
                        pltpu.VMEM((tm, D), bf16),
                        ]
                       + [pltpu.VMEM((cwid // LANES, tm + SUBLANES, LANES), f32)] * 4
                       + [pltpu.VMEM((tm, wdn.shape[0]), bf16)],
        compiler_params=_params(2),
        name="ffn",
    )(x, *map(_arr, (wup, cw, cb, wdn, g, b)))


PAIR_ORDER = [h for p in range(SWA_GROUP) for h in (p, p + SWA_GROUP)]


def _swa_gate_columns(w):
    depth, D = w.shape[:2]
    g0 = 4 * ML_WIDTH
    q0 = g0 + 2 * ML_HEADS
    kv0 = q0 + SWA_WIDTH
    heads = [w[..., q0 + h * SWA_HEAD_DIM:q0 + (h + 1) * SWA_HEAD_DIM] for h in PAIR_ORDER]
    used = SWA_WIDTH + (w.shape[-1] - kv0) + 2 * ML_HEADS
    pad = jnp.zeros((depth, D, -used % PROJ_COLS), bf16)
    return jnp.concatenate(heads + [w[..., kv0:], w[..., g0:q0], pad], axis=-1)


def _pair_rows(w_out_b):
    r0 = ML_WIDTH
    return jnp.concatenate(
        [w_out_b[:, r0 + h * SWA_HEAD_DIM:r0 + (h + 1) * SWA_HEAD_DIM] for h in PAIR_ORDER], axis=1)


def kernel(x, mem, rel_bias, w_in, ml_conv_w, ml_conv_b, ml_i_bias, ml_f_bias, ml_norm_g, swa_sinks, w_out, ln1_g, ln1_b, xa_wq, xa_wkv, xa_wo, ln2_g, ln2_b, ffn_w_up, ffn_conv_w, ffn_conv_b, ffn_w_down, ln3_g, ln3_b):
    depth = w_in.shape[0]
    assert depth == DEPTH, "ALPHA is the DeepNorm constant of a DEPTH-layer trunk"
    rows = lambda a: a.reshape(depth, 1, -1)
    w_in_b, w_out_b = w_in.astype(bf16), w_out.astype(bf16)
    wswg, wo_sw = _swa_gate_columns(w_in_b), _pair_rows(w_out_b)
    wq, wkv, wo = xa_wq.astype(bf16), xa_wkv.astype(bf16), xa_wo.astype(bf16)
    wup, wdn = ffn_w_up.astype(bf16), ffn_w_down.astype(bf16)
    gbias = jnp.concatenate([ml_i_bias, ml_f_bias], axis=1).reshape(depth, 2 * ML_HEADS, 1)
    small = [rows(a) for a in (ml_conv_b, ln1_g, ln1_b, ln2_g, ln2_b, ffn_conv_b, ln3_g, ln3_b)]
    norm_gt = jnp.broadcast_to(ml_norm_g[:, :, None], ml_norm_g.shape + (LANES,))
    bias = _swa_bias(rel_bias)
    for l in range(depth):
        at = lambda a: _Layer(a, l)
        conv_b, g1, b1, g2, b2, ffn_cb, g3, b3 = map(at, small)
        wqk, wvo = (_Layer(w_in_b, l, axis=1, size=2 * ML_WIDTH, index=i) for i in (0, 1))
        wo_ml = _Layer(w_out_b, l, axis=0, size=ML_WIDTH, index=0)
        qk, vot, rows, cols, sw = _inproj(x, wqk, wvo, at(wswg), at(ml_conv_w), conv_b, at(gbias))
        hml = _mlstm(qk, vot, rows, cols, at(norm_gt))
        hsw = _swa(sw, bias, swa_sinks, l)
        k, v = _memkv(mem, at(wkv))
        x = _mix(hml, hsw, x, wo_ml, at(wo_sw), g1, b1, at(wq), k, v, at(wo), g2, b2)
        x = _ffn(x, at(wup), at(ffn_conv_w), ffn_cb, at(wdn), g3, b3)
    return x
```
